```python
import jax, jax.numpy as jnp
from jax import lax
import numpy as np

D_MODEL = 2048
BATCH = 4
SEQ = 2048
DEPTH = 2
DEC_BATCH = 128
DEC_SEQ = 8
PAST_LEN = 16384
PAGE_SIZE = 128

HEAD_DIM = 128
N_HEADS_MIX = D_MODEL // HEAD_DIM
H_A = N_HEADS_MIX // 2
H_B = N_HEADS_MIX // 4
H_C = N_HEADS_MIX - H_A - H_B
Q_RANK = D_MODEL // 4
KV_RANK = D_MODEL // 8
NOPE_DIM = HEAD_DIM
ROPE_DIM = 64
V_DIM_A = HEAD_DIM
DK_B = HEAD_DIM
DV_B = HEAD_DIM
CONV_B = 4
GDN_CHUNK = 64
DK_C = HEAD_DIM
DV_C = HEAD_DIM
HG_CHUNK = 16
D_FF = 11 * D_MODEL // 4
FFN_CONV = 3
ROPE_THETA = 10000.0
EPS = 1e-6
NEG = -1e30
LB_FLOOR = 1e-30
Q_BLOCK = 128
W_A = H_A * V_DIM_A
W_B = H_B * DV_B
W_C = H_C * DV_C
MIX_WIDTH = W_A + W_B + W_C
IN_SPLITS = (Q_RANK, KV_RANK, ROPE_DIM, 3 * W_B, H_B, H_B, W_B, H_C * DK_C, H_C * DK_C, W_C, W_C)
IN_WIDTH = sum(IN_SPLITS)

kernel_name = 'hymba_mla_gdn_hgrn2_convffn_step'


def _rmsnorm(x, g):
    xf = x.astype(jnp.float32)
    y = xf * lax.rsqrt(jnp.mean(xf * xf, axis=-1, keepdims=True) + EPS)
    return (y * g.astype(jnp.float32)).astype(x.dtype)


def _l2norm(x):
    xf = x.astype(jnp.float32)
    return xf * lax.rsqrt(jnp.sum(xf * xf, axis=-1, keepdims=True) + EPS)


def _rope(x, pos):
    half = ROPE_DIM // 2
    inv = ROPE_THETA ** (-jnp.arange(half, dtype=jnp.float32) / half)
    ang = pos.astype(jnp.float32)[:, None] * inv[None, :]
    shp = (1, pos.shape[0]) + (1,) * (x.ndim - 3) + (half,)
    cos, sin = jnp.cos(ang).reshape(shp), jnp.sin(ang).reshape(shp)
    xf = x.astype(jnp.float32)
    x1, x2 = xf[..., :half], xf[..., half:]
    return jnp.concatenate([x1 * cos - x2 * sin, x2 * cos + x1 * sin], axis=-1).astype(x.dtype)


def _causal_dwconv(x, buf, w):
    k_w = w.shape[0]
    t = x.shape[1]
    xx = jnp.concatenate([buf.astype(x.dtype), x], axis=1)
    y = sum(xx[:, j:j + t] * w[j] for j in range(k_w))
    return y, xx[:, t:]


def _mla_attention(q_lat, q_pe, ckv, kpe, pos, past_ckv, past_kpe):
    b, t, h, r = q_lat.shape
    qb = min(Q_BLOCK, t)
    nb = t // qb
    scale = (NOPE_DIM + ROPE_DIM) ** -0.5

    def block(args):
        ql, qp, qpos = args
        s_new = (jnp.einsum('bthr,bsr->bhts', ql, ckv) + jnp.einsum('bthe,bse->bhts', qp, kpe)).astype(jnp.float32) * scale
        s_new = jnp.where(pos[None, :] <= qpos[:, None], s_new, NEG)
        if past_ckv is None:
            p = jax.nn.softmax(s_new, axis=-1).astype(ckv.dtype)
            return jnp.einsum('bhts,bsr->bthr', p, ckv)
        s_past = (jnp.einsum('bthr,bsr->bhts', ql, past_ckv) + jnp.einsum('bthe,bse->bhts', qp, past_kpe)).astype(jnp.float32) * scale
        n_past = past_ckv.shape[1]
        p = jax.nn.softmax(jnp.concatenate([s_past, s_new], axis=-1), axis=-1).astype(ckv.dtype)
        return jnp.einsum('bhts,bsr->bthr', p[..., :n_past], past_ckv) + jnp.einsum('bhts,bsr->bthr', p[..., n_past:], ckv)

    xs = (q_lat.reshape(b, nb, qb, h, r).swapaxes(0, 1),
          q_pe.reshape(b, nb, qb, h, ROPE_DIM).swapaxes(0, 1),
          pos.reshape(nb, qb))
    out = lax.map(block, xs)
    return out.swapaxes(0, 1).reshape(b, t, h, r)


def _gated_delta_rule(q, k, v, g, beta, s0):
    out_dtype = v.dtype
    q, k, v, g, beta = (a.astype(jnp.float32) for a in (q, k, v, g, beta))
    b, l, h, dk = q.shape
    dv = v.shape[-1]
    c = min(GDN_CHUNK, l)
    n = -(-l // c)
    pad = n * c - l

    def blk(a):
        a = jnp.pad(a, [(0, 0), (0, pad)] + [(0, 0)] * (a.ndim - 2))
        return a.reshape((b, n, c) + a.shape[2:])

    q, k, v = (blk(a).transpose(0, 3, 1, 2, 4) for a in (q, k, v))
    g, beta = (blk(a).transpose(0, 3, 1, 2) for a in (g, beta))
    gc = jnp.cumsum(g, axis=-1)
    causal = jnp.tril(jnp.ones((c, c), bool))
    strict = jnp.tril(jnp.ones((c, c), bool), -1)
    decay = jnp.exp(jnp.where(causal, gc[..., :, None] - gc[..., None, :], NEG))
    kb = k * beta[..., None]
    m = jnp.where(strict, jnp.einsum('bhnid,bhnjd->bhnij', kb, k) * decay, 0.0)
    rhs = jnp.concatenate([v * beta[..., None], kb * jnp.exp(gc)[..., None]], axis=-1)
    sol = lax.linalg.triangular_solve(m, rhs, left_side=True, lower=True, unit_diagonal=True)
    u, w = sol[..., :dv], sol[..., dv:]
    qk = jnp.einsum('bhnid,bhnjd->bhnij', q, k) * decay
    q_dec = q * jnp.exp(gc)[..., None]
    k_dec = k * jnp.exp(gc[..., -1:] - gc)[..., None]
    g_last = jnp.exp(gc[..., -1])

    def step(s, xs):
        u_n, w_n, qk_n, qd_n, kd_n, gl_n = xs
        v_new = u_n - jnp.einsum('bhcd,bhde->bhce', w_n, s)
        o = jnp.einsum('bhcd,bhde->bhce', qd_n, s) + jnp.einsum('bhij,bhje->bhie', qk_n, v_new)
        s = s * gl_n[..., None, None] + jnp.einsum('bhcd,bhce->bhde', kd_n, v_new)
        return s, o

    xs = tuple(jnp.moveaxis(a, 2, 0) for a in (u, w, qk, q_dec, k_dec, g_last))
    s, o = lax.scan(step, s0.astype(jnp.float32), xs)
    o = o.transpose(1, 0, 3, 2, 4).reshape(b, n * c, h, dv)[:, :l]
    return o.astype(out_dtype), s.astype(s0.dtype)


def _hgrn2(q, k, v, logf, s0):
    out_dtype = v.dtype
    q, k, v, logf = (a.astype(jnp.float32) for a in (q, k, v, logf))
    b, l, h, dk = q.shape
    dv = v.shape[-1]
    c = min(HG_CHUNK, l)
    n = -(-l // c)
    pad = n * c - l

    def blk(a):
        a = jnp.pad(a, [(0, 0), (0, pad), (0, 0), (0, 0)])
        return a.reshape(b, n, c, h, a.shape[-1]).transpose(1, 0, 3, 2, 4)

    causal = jnp.tril(jnp.ones((c, c), bool))

    def step(s, xs):
        qn, kn, vn, lf = xs
        bc = jnp.cumsum(lf, axis=-2)
        dec = jnp.exp(jnp.where(causal[:, :, None], bc[..., :, None, :] - bc[..., None, :, :], NEG))
        a = jnp.einsum('bhtd,bhsd,bhtsd->bhts', qn, kn, dec)
        o = jnp.einsum('bhts,bhse->bhte', a, vn) + jnp.einsum('bhtd,bhde->bhte', qn * jnp.exp(bc), s)
        b_last = bc[..., -1:, :]
        s = s * jnp.exp(b_last)[..., 0, :, None] + jnp.einsum('bhsd,bhse->bhde', kn * jnp.exp(b_last - bc), vn)
        return s, o

    s, o = lax.scan(step, s0.astype(jnp.float32), (blk(q), blk(k), blk(v), blk(logf)))
    o = o.transpose(1, 0, 3, 2, 4).reshape(b, n * c, h, dv)[:, :l]
    return o.astype(out_dtype), s.astype(s0.dtype)


def _layer(x, pos, past_ckv, past_kpe, gdn_s0, gdn_conv0, hg_s0, ffn_conv0, lb, lw):
    (norm_mix, w_in, q_norm, kv_norm, w_uq, w_uk, w_uv, gdn_conv_w, gdn_a_log, gdn_dt_bias,
     gdn_norm, hg_norm, w_out, norm_ffn, w_up, ffn_conv_w, w_down) = lw
    bsz, t, _ = x.shape
    h = _rmsnorm(x, norm_mix)
    proj = h @ w_in
    cuts = [int(i) for i in np.cumsum(IN_SPLITS)[:-1]]
    c_q, c_kv, k_pe, qkv_b, a_b, b_b, z_b, q_c, f_c, i_c, z_c = jnp.split(proj, cuts, axis=-1)

    q = (_rmsnorm(c_q, q_norm) @ w_uq).reshape(bsz, t, H_A, NOPE_DIM + ROPE_DIM)
    q_nope, q_pe = q[..., :NOPE_DIM], _rope(q[..., NOPE_DIM:], pos)
    ckv = _rmsnorm(c_kv, kv_norm)
    kpe = _rope(k_pe, pos)
    q_lat = jnp.einsum('bthd,rhd->bthr', q_nope, w_uk)
    o_lat = _mla_attention(q_lat, q_pe, ckv, kpe, pos, past_ckv, past_kpe)
    o_a = jnp.einsum('bthr,rhd->bthd', o_lat, w_uv).reshape(bsz, t, W_A)

    qkv, gdn_conv_new = _causal_dwconv(qkv_b, gdn_conv0, gdn_conv_w)
    qkv = jax.nn.silu(qkv)
    qg, kg, vg = jnp.split(qkv, 3, axis=-1)
    qg = _l2norm(qg.reshape(bsz, t, H_B, DK_B)) * (DK_B ** -0.5)
    kg = _l2norm(kg.reshape(bsz, t, H_B, DK_B))
    vg = vg.reshape(bsz, t, H_B, DV_B)
    g = -jnp.exp(gdn_a_log.astype(jnp.float32)) * jax.nn.softplus(a_b.astype(jnp.float32) + gdn_dt_bias.astype(jnp.float32))
    beta = jax.nn.sigmoid(b_b.astype(jnp.float32))
    o_b, gdn_s = _gated_delta_rule(qg, kg, vg, g, beta, gdn_s0)
    o_b = (_rmsnorm(o_b, gdn_norm) * jax.nn.silu(z_b.reshape(bsz, t, H_B, DV_B))).reshape(bsz, t, W_B)

    qh = jax.nn.silu(q_c).reshape(bsz, t, H_C, DK_C)
    fr = f_c.astype(jnp.float32).reshape(bsz, t, H_C, DK_C)
    lbh = lb.reshape(H_C, DK_C)
    logf = jnp.logaddexp(jnp.log(jnp.maximum(lbh, LB_FLOOR)), jnp.log1p(-lbh) + jax.nn.log_sigmoid(fr))
    kh = (1.0 - lbh) * jax.nn.sigmoid(-fr)
    o_c, hg_s = _hgrn2(qh, kh, i_c.reshape(bsz, t, H_C, DV_C), logf, hg_s0)
    o_c = (_rmsnorm(o_c, hg_norm) * jax.nn.silu(z_c.reshape(bsz, t, H_C, DV_C))).reshape(bsz, t, W_C)

    x = x + jnp.concatenate([o_a, o_b, o_c], axis=-1) @ w_out

    u, ffn_conv_new = _causal_dwconv(_rmsnorm(x, norm_ffn) @ w_up, ffn_conv0, ffn_conv_w)
    gate, up = jnp.split(u, 2, axis=-1)
    x = x + (jax.nn.silu(gate) * up) @ w_down
    return x, (ckv, kpe, gdn_s, gdn_conv_new, hg_s, ffn_conv_new)


def setup_inputs(seed: int = 0) -> dict:
    key = jax.random.key(seed)
    ks = iter(jax.random.split(key, 40))

    def nrm(shape, scale):
        return jax.random.normal(next(ks), shape, jnp.float32) * scale

    def gain(shape):
        return 1.0 + nrm(shape, 0.02)

    n_pages = PAST_LEN // PAGE_SIZE
    n_used = DEC_BATCH * n_pages
    n_pool = n_used + max(1, n_used // 4)
    x_prompt = nrm((BATCH, SEQ, D_MODEL), 1.0)
    x_sample = nrm((DEC_BATCH, DEC_SEQ, D_MODEL), 1.0)
    cache_ckv = nrm((DEPTH, n_pool, PAGE_SIZE, KV_RANK), 1.0)
    cache_kpe = nrm((DEPTH, n_pool, PAGE_SIZE, ROPE_DIM), 1.0)
    page_table = jax.random.permutation(next(ks), n_pool)[:n_used].reshape(DEC_BATCH, n_pages).astype(jnp.int32)
    state_gdn_S = nrm((DEPTH, DEC_BATCH, H_B, DK_B, DV_B), 0.5)
    state_gdn_conv = nrm((DEPTH, DEC_BATCH, CONV_B - 1, 3 * W_B), 1.0)
    state_hgrn_S = nrm((DEPTH, DEC_BATCH, H_C, DK_C, DV_C), 0.5)
    state_ffn_conv = nrm((DEPTH, DEC_BATCH, FFN_CONV - 1, 2 * D_FF), 1.0)
    gdn_a_log = jnp.log(jax.random.uniform(next(ks), (DEPTH, H_B), jnp.float32, 1.0, 16.0))
    dt = jnp.exp(jax.random.uniform(next(ks), (DEPTH, H_B), jnp.float32, float(np.log(1e-3)), float(np.log(1e-1))))
    gdn_dt_bias = dt + jnp.log(-jnp.expm1(-dt))
    return {
        'x_prompt': x_prompt,
        'x_sample': x_sample,
        'cache_ckv': cache_ckv,
        'cache_kpe': cache_kpe,
        'page_table': page_table,
        'state_gdn_S': state_gdn_S,
        'state_gdn_conv': state_gdn_conv,
        'state_hgrn_S': state_hgrn_S,
        'state_ffn_conv': state_ffn_conv,
        'norm_mix': gain((DEPTH, D_MODEL)),
        'w_in': nrm((DEPTH, D_MODEL, IN_WIDTH), D_MODEL ** -0.5),
        'mla_q_norm': gain((DEPTH, Q_RANK)),
        'mla_kv_norm': gain((DEPTH, KV_RANK)),
        'mla_w_uq': nrm((DEPTH, Q_RANK, H_A * (NOPE_DIM + ROPE_DIM)), Q_RANK ** -0.5),
        'mla_w_uk': nrm((DEPTH, KV_RANK, H_A, NOPE_DIM), KV_RANK ** -0.5),
        'mla_w_uv': nrm((DEPTH, KV_RANK, H_A, V_DIM_A), KV_RANK ** -0.5),
        'gdn_conv_w': nrm((DEPTH, CONV_B, 3 * W_B), CONV_B ** -0.5),
        'gdn_a_log': gdn_a_log,
        'gdn_dt_bias': gdn_dt_bias,
        'gdn_norm': gain((DEPTH, DV_B)),
        'hgrn_lb': nrm((DEPTH, H_C * DK_C), 0.1),
        'hgrn_norm': gain((DEPTH, DV_C)),
        'w_out': nrm((DEPTH, MIX_WIDTH, D_MODEL), MIX_WIDTH ** -0.5),
        'norm_ffn': gain((DEPTH, D_MODEL)),
        'ffn_w_up': nrm((DEPTH, D_MODEL, 2 * D_FF), D_MODEL ** -0.5),
        'ffn_conv_w': nrm((DEPTH, FFN_CONV, 2 * D_FF), FFN_CONV ** -0.5),
        'ffn_w_down': nrm((DEPTH, D_FF, D_MODEL), D_FF ** -0.5),
        'norm_final': gain((D_MODEL,)),
    }


def reference(x_prompt, x_sample, cache_ckv, cache_kpe, page_table, state_gdn_S, state_gdn_conv,
              state_hgrn_S, state_ffn_conv, norm_mix, w_in, mla_q_norm, mla_kv_norm, mla_w_uq,
              mla_w_uk, mla_w_uv, gdn_conv_w, gdn_a_log, gdn_dt_bias, gdn_norm, hgrn_lb, hgrn_norm,
              w_out, norm_ffn, ffn_w_up, ffn_conv_w, ffn_w_down, norm_final):
    bp, tp, _ = x_prompt.shape
    bs, ts, _ = x_sample.shape
    n_pages = page_table.shape[1]
    past_len = n_pages * PAGE_SIZE
    pos_p = jnp.arange(tp, dtype=jnp.int32)
    pos_s = past_len + jnp.arange(ts, dtype=jnp.int32)
    p_lb = jax.nn.softmax(hgrn_lb.astype(jnp.float32), axis=0)
    lb_all = jnp.cumsum(p_lb, axis=0) - p_lb[0:1]
    xp, xs = x_prompt, x_sample
    new_p, new_s = [], []
    for l in range(DEPTH):
        lw = (norm_mix[l], w_in[l], mla_q_norm[l], mla_kv_norm[l], mla_w_uq[l], mla_w_uk[l], mla_w_uv[l],
              gdn_conv_w[l], gdn_a_log[l], gdn_dt_bias[l], gdn_norm[l], hgrn_norm[l], w_out[l],
              norm_ffn[l], ffn_w_up[l], ffn_conv_w[l], ffn_w_down[l])
        xp, st_p = _layer(xp, pos_p, None, None,
                          jnp.zeros((bp, H_B, DK_B, DV_B), state_gdn_S.dtype),
                          jnp.zeros((bp, CONV_B - 1, 3 * W_B), xp.dtype),
                          jnp.zeros((bp, H_C, DK_C, DV_C), state_hgrn_S.dtype),
                          jnp.zeros((bp, FFN_CONV - 1, 2 * D_FF), xp.dtype),
                          lb_all[l], lw)
        past_ckv = cache_ckv[l][page_table].reshape(bs, past_len, KV_RANK)
        past_kpe = cache_kpe[l][page_table].reshape(bs, past_len, ROPE_DIM)
        xs, st_s = _layer(xs, pos_s, past_ckv, past_kpe, state_gdn_S[l], state_gdn_conv[l],
                          state_hgrn_S[l], state_ffn_conv[l], lb_all[l], lw)
        new_p.append(st_p)
        new_s.append(st_s)
    y_prompt = _rmsnorm(xp, norm_final)
    y_sample = _rmsnorm(xs, norm_final)
    ckv_p, kpe_p, gdn_S_p, gdn_conv_p, hgrn_S_p, ffn_conv_p = (jnp.stack(a) for a in zip(*new_p))
    ckv_s, kpe_s, gdn_S_s, gdn_conv_s, hgrn_S_s, ffn_conv_s = (jnp.stack(a) for a in zip(*new_s))
    return (y_prompt, y_sample, ckv_p, kpe_p, gdn_S_p, gdn_conv_p, hgrn_S_p, ffn_conv_p,
            ckv_s, kpe_s, gdn_S_s, gdn_conv_s, hgrn_S_s, ffn_conv_s)
```

```python
import functools

import jax
import jax.numpy as jnp
import numpy as np
from jax import lax
from jax.experimental import pallas as pl
from jax.experimental.pallas import tpu as pltpu

F32 = jnp.float32
BF16 = jnp.bfloat16

D_MODEL = 2048
PAGE_SIZE = 128
HEAD_DIM = 128
H_A = 8
H_B = 4
H_C = 4
Q_RANK = 512
KV_RANK = 256
NOPE_DIM = 128
ROPE_DIM = 64
W_B = H_B * HEAD_DIM
W_C = H_C * HEAD_DIM
CONV_B = 4
D_FF = 5632
FFN_CONV = 3
ROPE_THETA = 10000.0
EPS = 1e-6
NEG = -1e30
LB_FLOOR = 1e-30

PROJ_TILE = 512
COL_QKV = 0
COL_CQ = 3
COL_MISC = 4
COL_ZB = 5
COL_QC = 6
COL_FC = 7
COL_IC = 8
COL_ZC = 9
PROJ_WIDTH = 10 * PROJ_TILE
MISC_KPE = 256
MISC_AB = 384

SUBLANES = 8
LANES = 128
CHUNK = 128
HG_CHUNK = 16
VMEM_LIMIT = 56 * 1024 * 1024


def _params(*sem):
    return pltpu.CompilerParams(dimension_semantics=sem, vmem_limit_bytes=VMEM_LIMIT)


def _dot(a, b):
    return jnp.dot(a, b, preferred_element_type=F32)


def _dot_nt(a, b):
    return lax.dot_general(a, b, (((1,), (1,)), ((), ())), preferred_element_type=F32)


def _bdot(a, b):
    return _dot(a.astype(BF16), b.astype(BF16))


def _split2(a):
    hi = a.astype(BF16)
    lo = (a - hi.astype(F32)).astype(BF16)
    return hi, lo


def _dot3(a, b):
    ah, al = _split2(a)
    bh, bl = _split2(b)
    return _dot(ah, bh) + (_dot(ah, bl) + _dot(al, bh))


def _dot_exact_lhs(a01, b):
    a = a01.astype(BF16)
    b1 = b.astype(BF16)
    r1 = b - b1.astype(F32)
    b2 = r1.astype(BF16)
    b3 = (r1 - b2.astype(F32)).astype(BF16)
    return _dot(a, b1) + (_dot(a, b2) + _dot(a, b3))


def _sigmoid(x):
    return 1.0 / (1.0 + jnp.exp(-x))


def _silu(x):
    return x * _sigmoid(x)


def _softplus(x):
    return jnp.maximum(x, 0.0) + jnp.log1p(jnp.exp(-jnp.abs(x)))


def _rms(x, g):
    ms = jnp.mean(x * x, axis=-1, keepdims=True)
    return x * lax.rsqrt(ms + EPS) * g


def _swap_halves(x, lane):
    w = x.shape[-1]
    half = ROPE_DIM // 2
    fwd = pltpu.roll(x, w - half, axis=x.ndim - 1)
    bwd = pltpu.roll(x, half, axis=x.ndim - 1)
    return jnp.where((lane % ROPE_DIM) < half, fwd, bwd)


def _rope(x, cos, sin_signed):
    lane = lax.broadcasted_iota(jnp.int32, x.shape, x.ndim - 1)
    return x * cos + _swap_halves(x, lane) * sin_signed


def _norm_matmul_kernel(x_ref, g_ref, w_ref, o_ref, xn_ref):
    @pl.when(pl.program_id(1) == 0)
    def _():
        xn_ref[...] = _rms(x_ref[...], g_ref[...]).astype(BF16)

    o_ref[...] = _dot(xn_ref[...], w_ref[...])


def _norm_matmul(x, g, w, tm, tn):
    m, k = x.shape
    n = w.shape[1]
    return pl.pallas_call(
        _norm_matmul_kernel,
        grid=(m // tm, n // tn),
        in_specs=[pl.BlockSpec((tm, k), lambda i, j: (i, 0)),
                  pl.BlockSpec((1, k), lambda i, j: (0, 0)),
                  pl.BlockSpec((k, tn), lambda i, j: (0, j))],
        out_specs=pl.BlockSpec((tm, tn), lambda i, j: (i, j)),
        out_shape=jax.ShapeDtypeStruct((m, n), F32),
        scratch_shapes=[pltpu.VMEM((tm, k), BF16)],
        compiler_params=_params("parallel", "arbitrary"),
        name="in_proj",
    )(x, g.reshape(1, k), w)


def _mla_prep_kernel(cq_ref, misc_ref, cos_ref, sin_ref, qn_ref, kvn_ref, wuq_ref, wuk_ref,
                     qlat_ref, qpe_ref, ckv_ref, kpe_ref, *, scale):
    cos = cos_ref[...]
    sin = sin_ref[...]
    misc = misc_ref[...]
    ckv_ref[...] = _rms(misc[:, :KV_RANK], kvn_ref[...])
    kpe = _rope(misc[:, MISC_KPE:MISC_KPE + LANES], cos, sin)
    kpe_ref[...] = kpe[:, :ROPE_DIM]

    cqn = _rms(cq_ref[...], qn_ref[...]).astype(BF16)
    q = _dot(cqn, wuq_ref[...])
    n_nope = H_A * NOPE_DIM
    reps = (H_A * ROPE_DIM) // LANES
    q_pe = _rope(q[:, n_nope:], jnp.tile(cos, (1, reps)), jnp.tile(sin, (1, reps))) * scale
    for h in range(H_A):
        qn = q[:, h * NOPE_DIM:(h + 1) * NOPE_DIM].astype(BF16)
        qlat_ref[h] = (_dot(qn, wuk_ref[h]) * scale).astype(qlat_ref.dtype)
        qpe_ref[h] = q_pe[:, h * ROPE_DIM:(h + 1) * ROPE_DIM].astype(qpe_ref.dtype)


def _mla_prep(proj, cos, sin, q_norm, kv_norm, w_uq, w_uk_t, tm, q_dtype):
    m = proj.shape[0]
    scale = float((NOPE_DIM + ROPE_DIM) ** -0.5)
    return pl.pallas_call(
        functools.partial(_mla_prep_kernel, scale=scale),
        grid=(m // tm,),
        in_specs=[pl.BlockSpec((tm, PROJ_TILE), lambda i: (i, COL_CQ)),
                  pl.BlockSpec((tm, PROJ_TILE), lambda i: (i, COL_MISC)),
                  pl.BlockSpec((tm, LANES), lambda i: (i, 0)),
                  pl.BlockSpec((tm, LANES), lambda i: (i, 0)),
                  pl.BlockSpec((1, Q_RANK), lambda i: (0, 0)),
                  pl.BlockSpec((1, KV_RANK), lambda i: (0, 0)),
                  pl.BlockSpec(w_uq.shape, lambda i: (0, 0)),
                  pl.BlockSpec(w_uk_t.shape, lambda i: (0, 0, 0))],
        out_specs=[pl.BlockSpec((H_A, tm, KV_RANK), lambda i: (0, i, 0)),
                   pl.BlockSpec((H_A, tm, ROPE_DIM), lambda i: (0, i, 0)),
                   pl.BlockSpec((tm, KV_RANK), lambda i: (i, 0)),
                   pl.BlockSpec((tm, ROPE_DIM), lambda i: (i, 0))],
        out_shape=[jax.ShapeDtypeStruct((H_A, m, KV_RANK), q_dtype),
                   jax.ShapeDtypeStruct((H_A, m, ROPE_DIM), q_dtype),
                   jax.ShapeDtypeStruct((m, KV_RANK), F32),
                   jax.ShapeDtypeStruct((m, ROPE_DIM), F32)],
        compiler_params=_params("parallel"),
        name="mla_prep",
    )(proj, proj, cos, sin, q_norm.reshape(1, -1), kv_norm.reshape(1, -1), w_uq, w_uk_t)


def _softmax_step(s, v, m_ref, l_ref, acc_ref):
    m_prev = m_ref[...]
    m_new = jnp.maximum(m_prev, jnp.max(s, axis=-1, keepdims=True))
    alpha = jnp.exp(m_prev - m_new)
    p = jnp.exp(s - m_new)
    l_ref[...] = alpha * l_ref[...] + jnp.sum(p, axis=-1, keepdims=True)
    acc_ref[...] = alpha * acc_ref[...] + _dot(p.astype(BF16), v)
    m_ref[...] = m_new


def _value_up(o, wuv_ref, o_ref, rows):
    for h in range(H_A):
        oh = o[h * rows:(h + 1) * rows].astype(BF16)
        o_ref[:, h * HEAD_DIM:(h + 1) * HEAD_DIM] = _dot(oh, wuv_ref[h]).astype(o_ref.dtype)


def _prompt_attn_kernel(qlat_ref, qpe_ref, ckv_ref, kpe_ref, wuv_ref, o_ref, m_ref, l_ref, acc_ref, *, tq):
    qi = pl.program_id(1)
    ki = pl.program_id(2)
    rows = H_A * tq

    @pl.when(ki == 0)
    def _():
        m_ref[...] = jnp.full(m_ref.shape, NEG, F32)
        l_ref[...] = jnp.zeros(l_ref.shape, F32)
        acc_ref[...] = jnp.zeros(acc_ref.shape, F32)

    @pl.when(ki <= qi)
    def _():
        ql = qlat_ref[...].reshape(rows, KV_RANK)
        qp = qpe_ref[...].reshape(rows, ROPE_DIM)
        kc = ckv_ref[...].astype(BF16)
        kp = kpe_ref[...].astype(BF16)
        s = _dot_nt(ql, kc) + _dot_nt(qp, kp)
        t_row = lax.broadcasted_iota(jnp.int32, s.shape, 0) % tq
        t_col = lax.broadcasted_iota(jnp.int32, s.shape, 1)
        s = jnp.where((ki < qi) | (t_col <= t_row), s, NEG)
        _softmax_step(s, kc, m_ref, l_ref, acc_ref)

    @pl.when(ki == qi)
    def _():
        _value_up(acc_ref[...] / l_ref[...], wuv_ref, o_ref, tq)


def _prompt_attn(qlat, qpe, ckv, kpe, w_uv_t, nseq, t, tq):
    nq = t // tq
    rows = H_A * tq
    kv_map = lambda b, qi, ki: (b * nq + jnp.minimum(ki, qi), 0)
    return pl.pallas_call(
        functools.partial(_prompt_attn_kernel, tq=tq),
        grid=(nseq, nq, nq),
        in_specs=[pl.BlockSpec((H_A, tq, KV_RANK), lambda b, qi, ki: (0, b * nq + qi, 0)),
                  pl.BlockSpec((H_A, tq, ROPE_DIM), lambda b, qi, ki: (0, b * nq + qi, 0)),
                  pl.BlockSpec((tq, KV_RANK), kv_map),
                  pl.BlockSpec((tq, ROPE_DIM), kv_map),
                  pl.BlockSpec(w_uv_t.shape, lambda b, qi, ki: (0, 0, 0))],
        out_specs=pl.BlockSpec((tq, H_A * HEAD_DIM), lambda b, qi, ki: (b * nq + qi, 0)),
        out_shape=jax.ShapeDtypeStruct((nseq * t, H_A * HEAD_DIM), BF16),
        scratch_shapes=[pltpu.VMEM((rows, 1), F32), pltpu.VMEM((rows, 1), F32),
                        pltpu.VMEM((rows, KV_RANK), F32)],
        compiler_params=_params("parallel", "parallel", "arbitrary"),
        name="prompt_attn",
    )(qlat, qpe, ckv, kpe, w_uv_t)


def _decode_attn_kernel(pt_ref, qlat_ref, qpe_ref, nckv_ref, nkpe_ref, wuv_ref, *rest, pps, ts):
    ckv_refs = rest[:pps]
    kpe_refs = rest[pps:2 * pps]
    o_ref, m_ref, l_ref, acc_ref = rest[2 * pps:]
    b = pl.program_id(0)
    j = pl.program_id(1)
    rows = H_A * ts
    ql = qlat_ref[...].reshape(rows, KV_RANK).astype(BF16)
    qp = qpe_ref[...].reshape(rows, ROPE_DIM).astype(BF16)

    @pl.when(j == 0)
    def _():
        kc = nckv_ref[...].astype(BF16)
        kp = nkpe_ref[...].astype(BF16)
        s = _dot_nt(ql, kc) + _dot_nt(qp, kp)
        nb = kc.shape[0] // ts
        t_row = lax.broadcasted_iota(jnp.int32, s.shape, 0) % ts
        col = lax.broadcasted_iota(jnp.int32, s.shape, 1)
        ok = (col // ts == b % nb) & (col % ts <= t_row)
        s = jnp.where(ok, s, NEG)
        m = jnp.max(s, axis=-1, keepdims=True)
        p = jnp.where(ok, jnp.exp(s - m), 0.0)
        m_ref[...] = m
        l_ref[...] = jnp.sum(p, axis=-1, keepdims=True)
        acc_ref[...] = _dot(p.astype(BF16), kc)

    kcs = [r[...].astype(BF16) for r in ckv_refs]
    kps = [r[...].astype(BF16) for r in kpe_refs]
    s = jnp.concatenate([_dot_nt(ql, kc) + _dot_nt(qp, kp) for kc, kp in zip(kcs, kps)], axis=1)
    m_prev = m_ref[...]
    m_new = jnp.maximum(m_prev, jnp.max(s, axis=-1, keepdims=True))
    alpha = jnp.exp(m_prev - m_new)
    p = jnp.exp(s - m_new).astype(BF16)
    l_ref[...] = alpha * l_ref[...] + jnp.sum(p.astype(F32), axis=-1, keepdims=True)
    pv = _dot(p[:, :PAGE_SIZE], kcs[0])
    for i in range(1, pps):
        pv += _dot(p[:, i * PAGE_SIZE:(i + 1) * PAGE_SIZE], kcs[i])
    acc_ref[...] = alpha * acc_ref[...] + pv
    m_ref[...] = m_new

    @pl.when(j == pl.num_programs(1) - 1)
    def _():
        _value_up(acc_ref[...] / l_ref[...], wuv_ref, o_ref, ts)


def _decode_attn(qlat, qpe, ckv_new, kpe_new, cache_ckv, cache_kpe, page_table, w_uv_t, layer, ts, pps):
    nseq, n_pages = page_table.shape
    rows = H_A * ts
    new_rows = min(LANES, nseq * ts)
    nb = new_rows // ts

    def page_spec(width, i):
        return pl.BlockSpec((None, None, PAGE_SIZE, width),
                            lambda b, j, pt: (layer, pt[b, j * pps + i], 0, 0))

    grid_spec = pltpu.PrefetchScalarGridSpec(
        num_scalar_prefetch=1,
        grid=(nseq, n_pages // pps),
        in_specs=[pl.BlockSpec((H_A, ts, KV_RANK), lambda b, j, pt: (0, b, 0)),
                  pl.BlockSpec((H_A, ts, ROPE_DIM), lambda b, j, pt: (0, b, 0)),
                  pl.BlockSpec((new_rows, KV_RANK), lambda b, j, pt: (b // nb, 0)),
                  pl.BlockSpec((new_rows, ROPE_DIM), lambda b, j, pt: (b // nb, 0)),
                  pl.BlockSpec(w_uv_t.shape, lambda b, j, pt: (0, 0, 0))]
                 + [page_spec(KV_RANK, i) for i in range(pps)]
                 + [page_spec(ROPE_DIM, i) for i in range(pps)],
        out_specs=pl.BlockSpec((ts, H_A * HEAD_DIM), lambda b, j, pt: (b, 0)),
        scratch_shapes=[pltpu.VMEM((rows, 1), F32), pltpu.VMEM((rows, 1), F32),
                        pltpu.VMEM((rows, KV_RANK), F32)],
    )
    return pl.pallas_call(
        functools.partial(_decode_attn_kernel, pps=pps, ts=ts),
        grid_spec=grid_spec,
        out_shape=jax.ShapeDtypeStruct((nseq * ts, H_A * HEAD_DIM), F32),
        compiler_params=_params("parallel", "arbitrary"),
        name="decode_attn",
    )(page_table, qlat, qpe, ckv_new, kpe_new, w_uv_t, *([cache_ckv] * pps), *([cache_kpe] * pps))


def _shifted(x, prev, j):
    if j == 0:
        return x
    rx = pltpu.roll(x, j, axis=1)
    rp = pltpu.roll(prev, j, axis=1)
    row = lax.broadcasted_iota(jnp.int32, prev.shape, 1)
    head = jnp.where(row < j, rp, rx[:, :SUBLANES])
    if x.shape[1] == SUBLANES:
        return head
    return jnp.concatenate([head, rx[:, SUBLANES:]], axis=1)


def _dwconv(x, prev, w):
    k = w.shape[0]
    y = x * w[k - 1:k]
    for j in range(k - 1):
        y = y + _shifted(x, prev, k - 1 - j) * w[j:j + 1]
    return y


def _gdn_prep_kernel(x_ref, prev_ref, misc_ref, cw_ref, alog_ref, dtb_ref,
                     q_ref, k_ref, v_ref, gb_ref, cs_ref, *, seq_start_every):
    x = x_ref[...]
    nb, r, c = x.shape
    prev = prev_ref[...]
    if seq_start_every:
        first = (pl.program_id(0) % seq_start_every) == 0
        prev = jnp.where(first, 0.0, prev)
    y = _silu(_dwconv(x, prev, cw_ref[...])).reshape(nb * r, c)
    cs_ref[...] = x[:, r - SUBLANES:, :]
    for h in range(H_B):
        sl = slice(h * HEAD_DIM, (h + 1) * HEAD_DIM)
        qh = y[:, h * HEAD_DIM:(h + 1) * HEAD_DIM]
        kh = y[:, W_B + h * HEAD_DIM:W_B + (h + 1) * HEAD_DIM]
        q_ref[:, sl] = qh * lax.rsqrt(jnp.sum(qh * qh, axis=-1, keepdims=True) + EPS) * (HEAD_DIM ** -0.5)
        k_ref[:, sl] = kh * lax.rsqrt(jnp.sum(kh * kh, axis=-1, keepdims=True) + EPS)
    v_ref[...] = y[:, 2 * W_B:]
    ab = misc_ref[...].reshape(nb * r, PROJ_TILE)[:, MISC_AB:MISC_AB + LANES]
    g = -jnp.exp(alog_ref[...]) * _softplus(ab + dtb_ref[...])
    lane = lax.broadcasted_iota(jnp.int32, ab.shape, 1)
    gb_ref[...] = jnp.where(lane < H_B, g, _sigmoid(ab))


def _gdn_prep(proj3, conv_prev, conv_w, a_log, dt_bias, nb, r):
    nseq, t, _ = proj3.shape
    c = 3 * W_B
    steps_per_seq = t // r
    m = nseq * t
    rows = nb * r
    if conv_prev is None:
        assert nb == 1
        rb = r // SUBLANES
        prev_arr = proj3
        prev_spec = pl.BlockSpec((1, SUBLANES, c), lambda i: (i // steps_per_seq,
                                                               jnp.maximum((i % steps_per_seq) * rb - 1, 0), COL_QKV))
        x_map = lambda i: (i // steps_per_seq, i % steps_per_seq, COL_QKV)
        misc_map = lambda i: (i // steps_per_seq, i % steps_per_seq, COL_MISC)
        grid = (nseq * steps_per_seq,)
        seq_start_every = steps_per_seq
    else:
        assert r == t
        prev_arr = conv_prev
        prev_spec = pl.BlockSpec((nb, SUBLANES, c), lambda i: (i, 0, 0))
        x_map = lambda i: (i, 0, COL_QKV)
        misc_map = lambda i: (i, 0, COL_MISC)
        grid = (nseq // nb,)
        seq_start_every = 0
    alog = jnp.zeros((1, LANES), F32).at[0, :H_B].set(a_log.astype(F32))
    dtb = jnp.zeros((1, LANES), F32).at[0, :H_B].set(dt_bias.astype(F32))
    row_spec = lambda w: pl.BlockSpec((rows, w), lambda i: (i, 0))
    q, k, v, gb, cs = pl.pallas_call(
        functools.partial(_gdn_prep_kernel, seq_start_every=seq_start_every),
        grid=grid,
        in_specs=[pl.BlockSpec((nb, r, c), x_map), prev_spec,
                  pl.BlockSpec((nb, r, PROJ_TILE), misc_map),
                  pl.BlockSpec((CONV_B, c), lambda i: (0, 0)),
                  pl.BlockSpec((1, LANES), lambda i: (0, 0)),
                  pl.BlockSpec((1, LANES), lambda i: (0, 0))],
        out_specs=[row_spec(W_B), row_spec(W_B), row_spec(W_B), row_spec(LANES),
                   pl.BlockSpec((nb, SUBLANES, c), lambda i: (i, 0, 0))],
        out_shape=[jax.ShapeDtypeStruct((m, W_B), F32), jax.ShapeDtypeStruct((m, W_B), F32),
                   jax.ShapeDtypeStruct((m, W_B), F32), jax.ShapeDtypeStruct((m, LANES), F32),
                   jax.ShapeDtypeStruct((grid[0] * nb, SUBLANES, c), F32)],
        compiler_params=_params("arbitrary"),
        name="gdn_prep",
    )(proj3, prev_arr, proj3, conv_w, alog, dtb)
    cs = cs.reshape(nseq, -1, SUBLANES, c)[:, -1]
    return q, k, v, gb, cs


def _gdn_kernel(q_ref, k_ref, v_ref, gb_ref, z_ref, gn_ref, *rest, nseq_blk, r, zero_init):
    if zero_init:
        o_ref, sout_ref, s_scr, wS_scr, qS_scr, vn_scr = rest
    else:
        s0_ref, o_ref, sout_ref, s_scr, wS_scr, qS_scr, vn_scr = rest
    c = nseq_blk * r
    ci = pl.program_id(1)

    @pl.when(ci == 0)
    def _():
        if zero_init:
            s_scr[...] = jnp.zeros(s_scr.shape, F32)
        else:
            s_scr[...] = s0_ref[...]

    row = lax.broadcasted_iota(jnp.int32, (c, c), 0)
    col = lax.broadcasted_iota(jnp.int32, (c, c), 1)
    same = (row // r) == (col // r)
    causal = same & (col <= row)
    strict = same & (col < row)
    eye = (row == col).astype(F32)

    gb = gb_ref[...]
    gcs = _dot_exact_lhs(causal.astype(F32), gb)
    gtot = _dot_exact_lhs(same.astype(F32), gb)
    gcs_t = gcs.T
    n_sq = max(int(np.ceil(np.log2(r))) - 1, 0)

    for h in range(H_B):
        sl = slice(h * HEAD_DIM, (h + 1) * HEAD_DIM)
        qh, kh, vh = q_ref[:, sl], k_ref[:, sl], v_ref[:, sl]
        gc = gcs[:, h:h + 1]
        gr = gcs_t[h:h + 1, :]
        gl = gtot[:, h:h + 1]
        beta = gb[:, H_B + h:H_B + h + 1]
        decay = jnp.exp(jnp.where(causal, gc - gr, NEG))
        kb = kh * beta
        mm = jnp.where(strict, _dot_nt(kb.astype(BF16), kh.astype(BF16)) * decay, 0.0)
        tinv = eye - mm
        pw = mm
        for _ in range(n_sq):
            pw = _dot3(pw, pw)
            tinv = tinv + _dot3(tinv, pw)
        egc = jnp.exp(gc)
        u = _dot3(tinv, vh * beta)
        w = _dot3(tinv, kb * egc)
        qk = _dot_nt(qh.astype(BF16), kh.astype(BF16)) * decay
        q_dec = (qh * egc).astype(BF16)
        k_dec_t = (kh * jnp.exp(gl - gc)).T
        w_b = w.astype(BF16)

        if nseq_blk == 1:
            sb = s_scr[0, h].astype(BF16)
            v_new = u - _dot(w_b, sb)
            o = _dot(q_dec, sb) + _dot(qk.astype(BF16), v_new.astype(BF16))
            s_scr[0, h] = s_scr[0, h] * jnp.exp(gl[c - 1:c, :]) + _dot(k_dec_t.astype(BF16), v_new.astype(BF16))
        else:
            rowc = lax.broadcasted_iota(jnp.int32, (c, HEAD_DIM), 0) // r
            colc = lax.broadcasted_iota(jnp.int32, (HEAD_DIM, c), 1) // r
            wS_scr[...] = jnp.zeros(wS_scr.shape, F32)
            qS_scr[...] = jnp.zeros(qS_scr.shape, F32)

            def read_state(p, _):
                sb = s_scr[p, h].astype(BF16)
                mine = rowc == p
                wS_scr[...] += _dot(jnp.where(mine, w_b, jnp.zeros_like(w_b)), sb)
                qS_scr[...] += _dot(jnp.where(mine, q_dec, jnp.zeros_like(q_dec)), sb)
                return 0

            lax.fori_loop(0, nseq_blk, read_state, 0)
            v_new = u - wS_scr[...]
            o = qS_scr[...] + _dot(qk.astype(BF16), v_new.astype(BF16))
            vn_scr[...] = v_new.astype(BF16)
            e_gl = jnp.exp(gl)

            def write_state(p, _):
                kd = jnp.where(colc == p, k_dec_t, 0.0).astype(BF16)
                g_last = jnp.max(jnp.where(rowc[:, :1] == p, e_gl, 0.0), axis=0, keepdims=True)
                s_scr[p, h] = s_scr[p, h] * g_last + _dot(kd, vn_scr[...])
                return 0

            lax.fori_loop(0, nseq_blk, write_state, 0)

        on = _rms(o, gn_ref[...])
        o_ref[:, sl] = (on * _silu(z_ref[:, sl])).astype(o_ref.dtype)

    @pl.when(ci == pl.num_programs(1) - 1)
    def _():
        sout_ref[...] = s_scr[...]


def _gdn(q, k, v, gb, proj, gdn_norm, s0, nseq, t):
    m = nseq * t
    if t >= CHUNK:
        nseq_blk, r = 1, CHUNK
    else:
        nseq_blk, r = CHUNK // t, t
    n_chunks = t // r
    n_blocks = nseq // nseq_blk
    row_map = lambda i, ci: (i * n_chunks + ci, 0)
    state_spec = pl.BlockSpec((nseq_blk, H_B, HEAD_DIM, HEAD_DIM), lambda i, ci: (i, 0, 0, 0))
    in_specs = [pl.BlockSpec((CHUNK, W_B), row_map), pl.BlockSpec((CHUNK, W_B), row_map),
                pl.BlockSpec((CHUNK, W_B), row_map), pl.BlockSpec((CHUNK, LANES), row_map),
                pl.BlockSpec((CHUNK, W_B), lambda i, ci: (i * n_chunks + ci, COL_ZB)),
                pl.BlockSpec((1, HEAD_DIM), lambda i, ci: (0, 0))]
    args = [q, k, v, gb, proj, gdn_norm.reshape(1, HEAD_DIM)]
    if s0 is not None:
        in_specs.append(state_spec)
        args.append(s0)
    return pl.pallas_call(
        functools.partial(_gdn_kernel, nseq_blk=nseq_blk, r=r, zero_init=s0 is None),
        grid=(n_blocks, n_chunks),
        in_specs=in_specs,
        out_specs=[pl.BlockSpec((CHUNK, W_B), row_map), state_spec],
        out_shape=[jax.ShapeDtypeStruct((m, W_B), BF16),
                   jax.ShapeDtypeStruct((nseq, H_B, HEAD_DIM, HEAD_DIM), F32)],
        scratch_shapes=[pltpu.VMEM((nseq_blk, H_B, HEAD_DIM, HEAD_DIM), F32),
                        pltpu.VMEM((CHUNK, HEAD_DIM), F32), pltpu.VMEM((CHUNK, HEAD_DIM), F32),
                        pltpu.VMEM((CHUNK, HEAD_DIM), BF16)],
        compiler_params=_params("parallel", "arbitrary"),
        name="gdn",
    )(*args)


def _hgrn_kernel(qc_ref, fc_ref, ic_ref, zc_ref, lb_ref, hn_ref, *rest, cs, per_chunk_state, zero_init):
    if zero_init:
        o_ref, sout_ref, st_scr = rest
    else:
        s0_ref, o_ref, sout_ref, st_scr = rest
    c = qc_ref.shape[0]
    n_ch = c // cs
    step = pl.program_id(1)

    if not per_chunk_state:
        @pl.when(step == 0)
        def _():
            st_scr[...] = jnp.zeros(st_scr.shape, F32)

    lb = lb_ref[...]
    fr = fc_ref[...]
    log_lb = jnp.log(jnp.maximum(lb, LB_FLOOR))
    a = log_lb
    b = jnp.log1p(-lb) - _softplus(-fr)
    mx = jnp.maximum(a, b)
    logf = mx + jnp.log1p(jnp.exp(-jnp.abs(a - b)))
    kk = (1.0 - lb) * _sigmoid(-fr)
    qq = _silu(qc_ref[...])

    row = lax.broadcasted_iota(jnp.int32, (c, c), 0)
    col = lax.broadcasted_iota(jnp.int32, (c, c), 1)
    same = (row // cs) == (col // cs)
    bc_all = _dot_exact_lhs((same & (col <= row)).astype(F32), logf)
    btot_all = _dot_exact_lhs(same.astype(F32), logf)
    t_idx = lax.broadcasted_iota(jnp.int32, (n_ch, cs, HEAD_DIM), 1)
    rowc = lax.broadcasted_iota(jnp.int32, (c, HEAD_DIM), 0) // cs

    for h in range(H_C):
        sl = slice(h * HEAD_DIM, (h + 1) * HEAD_DIM)
        q, k, v = qq[:, sl], kk[:, sl], ic_ref[:, sl]
        bc, btot = bc_all[:, sl], btot_all[:, sl]
        q3, k3, v3, bc3 = (x.reshape(n_ch, cs, HEAD_DIM) for x in (q, k, v, bc))
        o3 = jnp.zeros((n_ch, cs, HEAD_DIM), F32)
        for s in range(cs):
            dec = jnp.exp(jnp.where(t_idx >= s, bc3 - bc3[:, s:s + 1, :], NEG))
            a_col = jnp.sum(q3 * k3[:, s:s + 1, :] * dec, axis=-1, keepdims=True)
            o3 = o3 + a_col * v3[:, s:s + 1, :]
        o = o3.reshape(c, HEAD_DIM)

        q_dec = (q * jnp.exp(bc)).astype(BF16)
        k_dec = (k * jnp.exp(btot - bc)).astype(BF16)
        v_t = v.T.astype(BF16)
        e_tot = jnp.exp(btot)

        def chunk_step(n, o_acc):
            mine = rowc == n
            if per_chunk_state:
                st = s0_ref[n, h].T
            else:
                st = st_scr[h]
            o_acc = o_acc + _dot_nt(jnp.where(mine, q_dec, jnp.zeros_like(q_dec)), st.astype(BF16))
            dec_n = jnp.max(jnp.where(mine, e_tot, 0.0), axis=0, keepdims=True)
            st = st * dec_n + _dot(v_t, jnp.where(mine, k_dec, jnp.zeros_like(k_dec)))
            if per_chunk_state:
                sout_ref[n, h] = st.T
            else:
                st_scr[h] = st
            return o_acc

        o = lax.fori_loop(0, n_ch, chunk_step, o)
        on = _rms(o, hn_ref[...])
        o_ref[:, sl] = (on * _silu(zc_ref[:, sl])).astype(o_ref.dtype)

    if not per_chunk_state:
        @pl.when(step == pl.num_programs(1) - 1)
        def _():
            for h in range(H_C):
                sout_ref[0, h] = st_scr[h].T


def _hgrn(proj, lb, hg_norm, s0, nseq, t):
    m = nseq * t
    cs = min(HG_CHUNK, t)
    per_chunk_state = t <= HG_CHUNK
    if per_chunk_state:
        assert s0 is not None
        seq_blk = CHUNK // t
        grid = (nseq // seq_blk, 1)
        state_spec = pl.BlockSpec((seq_blk, H_C, HEAD_DIM, HEAD_DIM), lambda i, j: (i, 0, 0, 0))
        steps = 1
    else:
        assert s0 is None
        steps = t // CHUNK
        grid = (nseq, steps)
        state_spec = pl.BlockSpec((1, H_C, HEAD_DIM, HEAD_DIM), lambda i, j: (i, 0, 0, 0))
    col = lambda tile: pl.BlockSpec((CHUNK, W_C), lambda i, j: (i * steps + j, tile))
    in_specs = [col(COL_QC), col(COL_FC), col(COL_IC), col(COL_ZC),
                pl.BlockSpec((1, W_C), lambda i, j: (0, 0)),
                pl.BlockSpec((1, HEAD_DIM), lambda i, j: (0, 0))]
    args = [proj, proj, proj, proj, lb.reshape(1, W_C), hg_norm.reshape(1, HEAD_DIM)]
    if s0 is not None:
        in_specs.append(state_spec)
        args.append(s0)
    return pl.pallas_call(
        functools.partial(_hgrn_kernel, cs=cs, per_chunk_state=per_chunk_state, zero_init=s0 is None),
        grid=grid,
        in_specs=in_specs,
        out_specs=[pl.BlockSpec((CHUNK, W_C), lambda i, j: (i * steps + j, 0)), state_spec],
        out_shape=[jax.ShapeDtypeStruct((m, W_C), BF16),
                   jax.ShapeDtypeStruct((nseq, H_C, HEAD_DIM, HEAD_DIM), F32)],
        scratch_shapes=[pltpu.VMEM((H_C, HEAD_DIM, HEAD_DIM), F32)],
        compiler_params=_params("parallel", "arbitrary"),
        name="hgrn",
    )(*args)


def _out_proj_kernel(x_ref, oa_ref, ob_ref, oc_ref, w_ref, o_ref):
    wa = H_A * HEAD_DIM
    acc = _dot(oa_ref[...].astype(BF16), w_ref[:wa])
    acc += _dot(ob_ref[...], w_ref[wa:wa + W_B])
    acc += _dot(oc_ref[...], w_ref[wa + W_B:])
    o_ref[...] = x_ref[...] + acc


def _out_proj(x, oa, ob, oc, w, tm, tn):
    m, d = x.shape
    kdim = w.shape[0]
    return pl.pallas_call(
        _out_proj_kernel,
        grid=(m // tm, d // tn),
        in_specs=[pl.BlockSpec((tm, tn), lambda i, j: (i, j)),
                  pl.BlockSpec((tm, oa.shape[1]), lambda i, j: (i, 0)),
                  pl.BlockSpec((tm, ob.shape[1]), lambda i, j: (i, 0)),
                  pl.BlockSpec((tm, oc.shape[1]), lambda i, j: (i, 0)),
                  pl.BlockSpec((kdim, tn), lambda i, j: (0, j))],
        out_specs=pl.BlockSpec((tm, tn), lambda i, j: (i, j)),
        out_shape=jax.ShapeDtypeStruct((m, d), F32),
        compiler_params=_params("parallel", "arbitrary"),
        name="out_proj",
    )(x, oa, ob, oc, w)


def _ffn_kernel(x_ref, prev_ref, prevu_ref, g_ref, wg_ref, wu_ref, cwg_ref, cwu_ref, wd_ref, gf_ref,
                o_ref, csg_ref, csu_ref, xn_ref, pn_ref, acc_ref, *, seq_start_every, prev_is_state, final_norm):
    j = pl.program_id(1)
    nb, r, d = x_ref.shape
    rows = nb * r

    @pl.when(j == 0)
    def _():
        xn_ref[...] = _rms(x_ref[...].reshape(rows, d), g_ref[...]).astype(BF16)
        acc_ref[...] = jnp.zeros(acc_ref.shape, F32)
        if not prev_is_state:
            pn = _rms(prev_ref[...].reshape(SUBLANES, d), g_ref[...])
            pn_ref[...] = jnp.concatenate([pn, jnp.zeros_like(pn)], axis=0).astype(BF16)

    xn = xn_ref[...]
    tn = wg_ref.shape[1]
    ug = _dot(xn, wg_ref[...]).reshape(nb, r, tn)
    uu = _dot(xn, wu_ref[...]).reshape(nb, r, tn)
    if prev_is_state:
        pg = prev_ref[...]
        pu = prevu_ref[...]
    else:
        first = (pl.program_id(0) % seq_start_every) == 0
        pg = jnp.where(first, 0.0, _dot(pn_ref[...], wg_ref[...])[:SUBLANES]).reshape(1, SUBLANES, tn)
        pu = jnp.where(first, 0.0, _dot(pn_ref[...], wu_ref[...])[:SUBLANES]).reshape(1, SUBLANES, tn)
    csg_ref[...] = ug[:, r - SUBLANES:, :]
    csu_ref[...] = uu[:, r - SUBLANES:, :]
    gate = _dwconv(ug, pg, cwg_ref[...]).reshape(rows, tn)
    up = _dwconv(uu, pu, cwu_ref[...]).reshape(rows, tn)
    hidden = (_silu(gate) * up).astype(BF16)
    acc_ref[...] += _dot(hidden, wd_ref[...])

    @pl.when(j == pl.num_programs(1) - 1)
    def _():
        y = x_ref[...].reshape(rows, d) + acc_ref[...]
        if final_norm:
            y = _rms(y, gf_ref[...])
        o_ref[...] = y.reshape(nb, r, d)


def _ffn(x3, conv_prev, norm_g, w_up, conv_w, w_down, norm_final, nb, r, tn, final_norm):
    nseq, t, d = x3.shape
    nj = D_FF // tn
    steps_per_seq = t // r
    if conv_prev is None:
        assert nb == 1
        rb = r // SUBLANES
        prev_arr = x3
        prev_spec = pl.BlockSpec((1, SUBLANES, d), lambda i, j: (i // steps_per_seq,
                                                                 jnp.maximum((i % steps_per_seq) * rb - 1, 0), 0))
        prevu_spec = prev_spec
        x_map = lambda i, j: (i // steps_per_seq, i % steps_per_seq, 0)
        grid = (nseq * steps_per_seq, nj)
    else:
        assert r == t
        prev_arr = conv_prev
        prev_spec = pl.BlockSpec((nb, SUBLANES, tn), lambda i, j: (i, 0, j))
        prevu_spec = pl.BlockSpec((nb, SUBLANES, tn), lambda i, j: (i, 0, nj + j))
        x_map = lambda i, j: (i, 0, 0)
        grid = (nseq // nb, nj)
    rows = nb * r
    out, csg, csu = pl.pallas_call(
        functools.partial(_ffn_kernel, seq_start_every=steps_per_seq, prev_is_state=conv_prev is not None,
                          final_norm=final_norm),
        grid=grid,
        in_specs=[pl.BlockSpec((nb, r, d), x_map), prev_spec, prevu_spec,
                  pl.BlockSpec((1, d), lambda i, j: (0, 0)),
                  pl.BlockSpec((d, tn), lambda i, j: (0, j)),
                  pl.BlockSpec((d, tn), lambda i, j: (0, nj + j)),
                  pl.BlockSpec((FFN_CONV, tn), lambda i, j: (0, j)),
                  pl.BlockSpec((FFN_CONV, tn), lambda i, j: (0, nj + j)),
                  pl.BlockSpec((tn, d), lambda i, j: (j, 0)),
                  pl.BlockSpec((1, d), lambda i, j: (0, 0))],
        out_specs=[pl.BlockSpec((nb, r, d), x_map),
                   pl.BlockSpec((nb, SUBLANES, tn), lambda i, j: (i, 0, j)),
                   pl.BlockSpec((nb, SUBLANES, tn), lambda i, j: (i, 0, j))],
        out_shape=[jax.ShapeDtypeStruct((nseq, t, d), F32),
                   jax.ShapeDtypeStruct((grid[0] * nb, SUBLANES, D_FF), F32),
                   jax.ShapeDtypeStruct((grid[0] * nb, SUBLANES, D_FF), F32)],
        scratch_shapes=[pltpu.VMEM((rows, d), BF16), pltpu.VMEM((2 * SUBLANES, d), BF16),
                        pltpu.VMEM((rows, d), F32)],
        compiler_params=_params("arbitrary", "arbitrary"),
        name="conv_ffn",
    )(x3, prev_arr, prev_arr, norm_g.reshape(1, d), w_up, w_up, conv_w, conv_w, w_down, norm_final.reshape(1, d))
    csg = csg.reshape(nseq, -1, SUBLANES, D_FF)[:, -1]
    csu = csu.reshape(nseq, -1, SUBLANES, D_FF)[:, -1]
    return out, csg, csu


def _pack_w_in(w):
    d = w.shape[0]
    cuts = np.cumsum([Q_RANK, KV_RANK, ROPE_DIM, 3 * W_B, H_B, H_B, W_B, W_C, W_C, W_C, W_C])
    c_q, c_kv, k_pe, qkv, a_b, b_b, z_b, q_c, f_c, i_c, z_c = jnp.split(w, [int(c) for c in cuts[:-1]], axis=1)
    z = lambda n: jnp.zeros((d, n), w.dtype)
    misc = jnp.concatenate([c_kv, k_pe, z(MISC_AB - MISC_KPE - ROPE_DIM), a_b, b_b,
                            z(PROJ_TILE - MISC_AB - 2 * H_B)], axis=1)
    return jnp.concatenate([qkv, c_q, misc, z_b, q_c, f_c, i_c, z_c], axis=1).astype(BF16)


def _pack_w_uq(w):
    w3 = w.reshape(Q_RANK, H_A, NOPE_DIM + ROPE_DIM)
    return jnp.concatenate([w3[:, :, :NOPE_DIM].reshape(Q_RANK, -1),
                            w3[:, :, NOPE_DIM:].reshape(Q_RANK, -1)], axis=1).astype(BF16)


def _rope_tables(pos):
    half = ROPE_DIM // 2
    inv = ROPE_THETA ** (-jnp.arange(half, dtype=F32) / half)
    ang = pos.astype(F32)[:, None] * inv[None, :]
    cos, sin = jnp.cos(ang), jnp.sin(ang)
    reps = LANES // ROPE_DIM
    return (jnp.tile(jnp.concatenate([cos, cos], axis=1), (1, reps)),
            jnp.tile(jnp.concatenate([-sin, sin], axis=1), (1, reps)))


def _pad_rows_front(a, total):
    n, k, c = a.shape
    return jnp.concatenate([jnp.zeros((n, total - k, c), a.dtype), a], axis=1)


def _row_tile(m):
    for tm in (512, 256, 128, 64, 32, 16, 8):
        if m % tm == 0:
            return tm
    raise ValueError(m)


def _layer(x3, cos, sin, lw, lb, past, final_norm_g, is_last):
    nseq, t, d = x3.shape
    m = nseq * t
    tm = _row_tile(m)
    x = x3.reshape(m, d)
    proj = _norm_matmul(x, lw["norm_mix"], lw["w_in"], tm, PROJ_TILE)
    proj3 = proj.reshape(nseq, t, PROJ_WIDTH)

    qlat, qpe, ckv, kpe = _mla_prep(proj, cos, sin, lw["q_norm"], lw["kv_norm"], lw["w_uq"], lw["w_uk_t"],
                                    min(tm, 256), BF16 if past is None else F32)
    if past is None:
        o_a = _prompt_attn(qlat, qpe, ckv, kpe, lw["w_uv_t"], nseq, t, min(256, t))
        gdn_prev = hg_s0 = gdn_s0 = ffn_prev = None
        nb, r = 1, min(tm, t)
    else:
        n_pages = past["page_table"].shape[1]
        pps = 16 if n_pages % 16 == 0 else n_pages
        o_a = _decode_attn(qlat, qpe, ckv, kpe, past["cache_ckv"], past["cache_kpe"], past["page_table"],
                           lw["w_uv_t"], past["layer"], t, pps)
        gdn_prev = _pad_rows_front(past["gdn_conv"], SUBLANES)
        gdn_s0, hg_s0 = past["gdn_S"], past["hgrn_S"]
        nb, r = min(nseq, 512 // t), t
    q, k, v, gb, gdn_cs = _gdn_prep(proj3, gdn_prev, lw["gdn_conv_w"], lw["gdn_a_log"], lw["gdn_dt_bias"], nb, r)
    o_b, gdn_s = _gdn(q, k, v, gb, proj, lw["gdn_norm"], gdn_s0, nseq, t)
    o_c, hg_s = _hgrn(proj, lb, lw["hg_norm"], hg_s0, nseq, t)
    x = _out_proj(x, o_a, o_b, o_c, lw["w_out"], tm, 1024)

    tn = 512
    ffn_prev = None if past is None else _pad_rows_front(past["ffn_conv"], SUBLANES)
    x3, csg, csu = _ffn(x.reshape(nseq, t, d), ffn_prev, lw["norm_ffn"], lw["w_up"], lw["ffn_conv_w"],
                        lw["w_down"], final_norm_g, nb, r, tn, is_last)
    ffn_cs = jnp.concatenate([csg, csu], axis=2)[:, SUBLANES - (FFN_CONV - 1):]
    new_state = (ckv.reshape(nseq, t, KV_RANK), kpe.reshape(nseq, t, ROPE_DIM), gdn_s,
                 gdn_cs[:, SUBLANES - (CONV_B - 1):], hg_s, ffn_cs)
    return x3, new_state


def kernel(x_prompt, x_sample, cache_ckv, cache_kpe, page_table, state_gdn_S, state_gdn_conv, state_hgrn_S, state_ffn_conv, norm_mix, w_in, mla_q_norm, mla_kv_norm, mla_w_uq, mla_w_uk, mla_w_uv, gdn_conv_w, gdn_a_log, gdn_dt_bias, gdn_norm, hgrn_lb, hgrn_norm, w_out, norm_ffn, ffn_w_up, ffn_conv_w, ffn_w_down, norm_final):
    depth = w_in.shape[0]
    bp, tp, _ = x_prompt.shape
    bs, ts, _ = x_sample.shape
    past_len = page_table.shape[1] * PAGE_SIZE
    cos_p, sin_p = _rope_tables(jnp.tile(jnp.arange(tp, dtype=jnp.int32), bp))
    cos_s, sin_s = _rope_tables(jnp.tile(past_len + jnp.arange(ts, dtype=jnp.int32), bs))
    p_lb = jax.nn.softmax(hgrn_lb.astype(F32), axis=0)
    lb_all = jnp.cumsum(p_lb, axis=0) - p_lb[0:1]

    xp, xs = x_prompt, x_sample
    new_p, new_s = [], []
    for l in range(depth):
        lw = dict(norm_mix=norm_mix[l], w_in=_pack_w_in(w_in[l]), q_norm=mla_q_norm[l], kv_norm=mla_kv_norm[l],
                  w_uq=_pack_w_uq(mla_w_uq[l]),
                  w_uk_t=jnp.transpose(mla_w_uk[l], (1, 2, 0)).astype(BF16),
                  w_uv_t=jnp.transpose(mla_w_uv[l], (1, 0, 2)).astype(BF16),
                  gdn_conv_w=gdn_conv_w[l], gdn_a_log=gdn_a_log[l], gdn_dt_bias=gdn_dt_bias[l],
                  gdn_norm=gdn_norm[l], hg_norm=hgrn_norm[l], w_out=w_out[l].astype(BF16),
                  norm_ffn=norm_ffn[l], w_up=ffn_w_up[l].astype(BF16), ffn_conv_w=ffn_conv_w[l],
                  w_down=ffn_w_down[l].astype(BF16))
        last = l == depth - 1
        xp, st_p = _layer(xp, cos_p, sin_p, lw, lb_all[l], None, norm_final, last)
        past = dict(cache_ckv=cache_ckv, cache_kpe=cache_kpe, page_table=page_table, layer=l,
                    gdn_S=state_gdn_S[l], gdn_conv=state_gdn_conv[l], hgrn_S=state_hgrn_S[l],
                    ffn_conv=state_ffn_conv[l])
        xs, st_s = _layer(xs, cos_s, sin_s, lw, lb_all[l], past, norm_final, last)
        new_p.append(st_p)
        new_s.append(st_s)
    outs_p = tuple(jnp.stack(a) for a in zip(*new_p))
    outs_s = tuple(jnp.stack(a) for a in zip(*new_s))
    return (xp, xs) + outs_p + outs_s
```

```python
import functools

import jax
import jax.numpy as jnp
import numpy as np
from jax import lax
from jax.experimental import pallas as pl
from jax.experimental.pallas import tpu as pltpu

F32 = jnp.float32
BF16 = jnp.bfloat16

D_MODEL = 2048
PAGE_SIZE = 128
HEAD_DIM = 128
H_A = 8
H_B = 4
H_C = 4
Q_RANK = 512
KV_RANK = 256
NOPE_DIM = 128
ROPE_DIM = 64
W_B = H_B * HEAD_DIM
W_C = H_C * HEAD_DIM
CONV_B = 4
D_FF = 5632
FFN_CONV = 3
ROPE_THETA = 10000.0
EPS = 1e-6
NEG = -1e30
LB_FLOOR = 1e-30

PROJ_TILE = 512
COL_QKV = 0
COL_CQ = 3
COL_MISC = 4
COL_ZB = 5
COL_QC = 6
COL_FC = 7
COL_IC = 8
COL_ZC = 9
PROJ_WIDTH = 10 * PROJ_TILE
MISC_KPE = 256
MISC_AB = 384

SUBLANES = 8
LANES = 128
CHUNK = 128
HG_CHUNK = 16
FFN_SUB = 1
VMEM_LIMIT = 56 * 1024 * 1024


def _params(*sem):
    return pltpu.CompilerParams(dimension_semantics=sem, vmem_limit_bytes=VMEM_LIMIT)


def _dot(a, b):
    return jnp.dot(a, b, preferred_element_type=F32)


def _dot_nt(a, b):
    return lax.dot_general(a, b, (((1,), (1,)), ((), ())), preferred_element_type=F32)


def _bdot(a, b):
    return _dot(a.astype(BF16), b.astype(BF16))


def _split2(a):
    hi = a.astype(BF16)
    lo = (a - hi.astype(F32)).astype(BF16)
    return hi, lo


def _dot3(a, b):
    ah, al = _split2(a)
    bh, bl = _split2(b)
    return _dot(ah, bh) + (_dot(ah, bl) + _dot(al, bh))


def _dot_exact_lhs(a01, b):
    a = a01.astype(BF16)
    b1 = b.astype(BF16)
    r1 = b - b1.astype(F32)
    b2 = r1.astype(BF16)
    b3 = (r1 - b2.astype(F32)).astype(BF16)
    return _dot(a, b1) + (_dot(a, b2) + _dot(a, b3))


def _sigmoid(x):
    return 1.0 / (1.0 + jnp.exp(-x))


def _silu(x):
    return x * _sigmoid(x)


def _softplus(x):
    return jnp.maximum(x, 0.0) + jnp.log1p(jnp.exp(-jnp.abs(x)))


def _rms(x, g):
    ms = jnp.mean(x * x, axis=-1, keepdims=True)
    return x * lax.rsqrt(ms + EPS) * g


def _swap_halves(x, lane):
    w = x.shape[-1]
    half = ROPE_DIM // 2
    fwd = pltpu.roll(x, w - half, axis=x.ndim - 1)
    bwd = pltpu.roll(x, half, axis=x.ndim - 1)
    return jnp.where((lane % ROPE_DIM) < half, fwd, bwd)


def _rope(x, cos, sin_signed):
    lane = lax.broadcasted_iota(jnp.int32, x.shape, x.ndim - 1)
    return x * cos + _swap_halves(x, lane) * sin_signed


def _norm_matmul_kernel(x_ref, g_ref, w_ref, o_ref, xn_ref):
    @pl.when(pl.program_id(1) == 0)
    def _():
        xn_ref[...] = _rms(x_ref[...], g_ref[...]).astype(BF16)

    o_ref[...] = _dot(xn_ref[...], w_ref[...])


def _norm_matmul(x, g, w, tm, tn):
    m, k = x.shape
    n = w.shape[1]
    return pl.pallas_call(
        _norm_matmul_kernel,
        grid=(m // tm, n // tn),
        in_specs=[pl.BlockSpec((tm, k), lambda i, j: (i, 0)),
                  pl.BlockSpec((1, k), lambda i, j: (0, 0)),
                  pl.BlockSpec((k, tn), lambda i, j: (0, j))],
        out_specs=pl.BlockSpec((tm, tn), lambda i, j: (i, j)),
        out_shape=jax.ShapeDtypeStruct((m, n), F32),
        scratch_shapes=[pltpu.VMEM((tm, k), BF16)],
        compiler_params=_params("parallel", "arbitrary"),
        name="in_proj",
    )(x, g.reshape(1, k), w)


def _mla_prep_kernel(cq_ref, misc_ref, cos_ref, sin_ref, qn_ref, kvn_ref, wuq_ref, wuk_ref,
                     qlat_ref, qpe_ref, ckv_ref, kpe_ref, *, scale):
    cos = cos_ref[...]
    sin = sin_ref[...]
    misc = misc_ref[...]
    ckv_ref[...] = _rms(misc[:, :KV_RANK], kvn_ref[...])
    kpe = _rope(misc[:, MISC_KPE:MISC_KPE + LANES], cos, sin)
    kpe_ref[...] = kpe[:, :ROPE_DIM]

    cqn = _rms(cq_ref[...], qn_ref[...]).astype(BF16)
    q = _dot(cqn, wuq_ref[...])
    n_nope = H_A * NOPE_DIM
    reps = (H_A * ROPE_DIM) // LANES
    q_pe = _rope(q[:, n_nope:], jnp.tile(cos, (1, reps)), jnp.tile(sin, (1, reps))) * scale
    for h in range(H_A):
        qn = q[:, h * NOPE_DIM:(h + 1) * NOPE_DIM].astype(BF16)
        qlat_ref[h] = (_dot(qn, wuk_ref[h]) * scale).astype(qlat_ref.dtype)
        qpe_ref[h] = q_pe[:, h * ROPE_DIM:(h + 1) * ROPE_DIM].astype(qpe_ref.dtype)


def _mla_prep(proj, cos, sin, q_norm, kv_norm, w_uq, w_uk_t, tm, q_dtype):
    m = proj.shape[0]
    scale = float((NOPE_DIM + ROPE_DIM) ** -0.5)
    return pl.pallas_call(
        functools.partial(_mla_prep_kernel, scale=scale),
        grid=(m // tm,),
        in_specs=[pl.BlockSpec((tm, PROJ_TILE), lambda i: (i, COL_CQ)),
                  pl.BlockSpec((tm, PROJ_TILE), lambda i: (i, COL_MISC)),
                  pl.BlockSpec((tm, LANES), lambda i: (i, 0)),
                  pl.BlockSpec((tm, LANES), lambda i: (i, 0)),
                  pl.BlockSpec((1, Q_RANK), lambda i: (0, 0)),
                  pl.BlockSpec((1, KV_RANK), lambda i: (0, 0)),
                  pl.BlockSpec(w_uq.shape, lambda i: (0, 0)),
                  pl.BlockSpec(w_uk_t.shape, lambda i: (0, 0, 0))],
        out_specs=[pl.BlockSpec((H_A, tm, KV_RANK), lambda i: (0, i, 0)),
                   pl.BlockSpec((H_A, tm, ROPE_DIM), lambda i: (0, i, 0)),
                   pl.BlockSpec((tm, KV_RANK), lambda i: (i, 0)),
                   pl.BlockSpec((tm, ROPE_DIM), lambda i: (i, 0))],
        out_shape=[jax.ShapeDtypeStruct((H_A, m, KV_RANK), q_dtype),
                   jax.ShapeDtypeStruct((H_A, m, ROPE_DIM), q_dtype),
                   jax.ShapeDtypeStruct((m, KV_RANK), F32),
                   jax.ShapeDtypeStruct((m, ROPE_DIM), F32)],
        compiler_params=_params("parallel"),
        name="mla_prep",
    )(proj, proj, cos, sin, q_norm.reshape(1, -1), kv_norm.reshape(1, -1), w_uq, w_uk_t)


def _lanes(x, n):
    return x if n == LANES else jnp.concatenate([x] * (n // LANES), axis=1)


def _prompt_attn_kernel(qi_tab, ki_tab, qlat_ref, qpe_ref, ckv_ref, kpe_ref, wuv_ref, o_ref,
                        m_ref, l_ref, acc_ref):
    step = pl.program_id(1)
    qi = qi_tab[step]
    ki = ki_tab[step]
    tq, tk = qlat_ref.shape[1], ckv_ref.shape[0]

    @pl.when(ki == 0)
    def _():
        m_ref[...] = jnp.full(m_ref.shape, NEG, F32)
        l_ref[...] = jnp.zeros(l_ref.shape, F32)
        acc_ref[...] = jnp.zeros(acc_ref.shape, F32)

    def update(masked):
        kc = ckv_ref[...].astype(BF16)
        kp = kpe_ref[...].astype(BF16)
        if masked:
            keep = (lax.broadcasted_iota(jnp.int32, (tq, tk), 1)
                    <= lax.broadcasted_iota(jnp.int32, (tq, tk), 0))
        for h in range(H_A):
            s = _dot_nt(qlat_ref[h], kc) + _dot_nt(qpe_ref[h], kp)
            if masked:
                s = jnp.where(keep, s, NEG)
            m_prev = m_ref[h]
            m_new = jnp.maximum(m_prev, jnp.max(s, axis=-1, keepdims=True))
            alpha = jnp.exp(m_prev - m_new)
            p = jnp.exp(s - _lanes(m_new, tk))
            l_ref[h] = alpha * l_ref[h] + jnp.sum(p, axis=-1, keepdims=True)
            acc_ref[h] = _lanes(alpha, KV_RANK) * acc_ref[h] + _dot(p.astype(BF16), kc)
            m_ref[h] = m_new

    @pl.when(ki < qi)
    def _():
        update(False)

    @pl.when(ki == qi)
    def _():
        update(True)
        for h in range(H_A):
            o = (acc_ref[h] * _lanes(1.0 / l_ref[h], KV_RANK)).astype(BF16)
            o_ref[:, h * HEAD_DIM:(h + 1) * HEAD_DIM] = _dot(o, wuv_ref[h]).astype(o_ref.dtype)


def _prompt_attn(qlat, qpe, ckv, kpe, w_uv_t, nseq, t, tq):
    nq = t // tq
    pairs = [(qi, ki) for qi in range(nq) for ki in range(qi + 1)]
    qi_tab = jnp.asarray([p[0] for p in pairs], jnp.int32)
    ki_tab = jnp.asarray([p[1] for p in pairs], jnp.int32)
    q_map = lambda b, s, qt, kt: (0, b * nq + qt[s], 0)
    kv_map = lambda b, s, qt, kt: (b * nq + kt[s], 0)
    grid_spec = pltpu.PrefetchScalarGridSpec(
        num_scalar_prefetch=2,
        grid=(nseq, len(pairs)),
        in_specs=[pl.BlockSpec((H_A, tq, KV_RANK), q_map),
                  pl.BlockSpec((H_A, tq, ROPE_DIM), q_map),
                  pl.BlockSpec((tq, KV_RANK), kv_map),
                  pl.BlockSpec((tq, ROPE_DIM), kv_map),
                  pl.BlockSpec(w_uv_t.shape, lambda b, s, qt, kt: (0, 0, 0))],
        out_specs=pl.BlockSpec((tq, H_A * HEAD_DIM), lambda b, s, qt, kt: (b * nq + qt[s], 0)),
        scratch_shapes=[pltpu.VMEM((H_A, tq, LANES), F32), pltpu.VMEM((H_A, tq, LANES), F32),
                        pltpu.VMEM((H_A, tq, KV_RANK), F32)],
    )
    return pl.pallas_call(
        _prompt_attn_kernel,
        grid_spec=grid_spec,
        out_shape=jax.ShapeDtypeStruct((nseq * t, H_A * HEAD_DIM), BF16),
        compiler_params=_params("parallel", "arbitrary"),
        name="prompt_attn",
    )(qi_tab, ki_tab, qlat, qpe, ckv, kpe, w_uv_t)


def _decode_attn_kernel(pt_ref, qlat_ref, qpe_ref, nckv_ref, nkpe_ref, wuv_ref, ckv_hbm, kpe_t_hbm,
                        o_ref, kbuf, pbuf, sem, m_ref, l_ref, acc_ref, *, layer, pps, ts, n_chains):
    b = pl.program_id(0)
    j = pl.program_id(1)
    nj = pl.num_programs(1)
    step = b * nj + j
    last_step = pl.num_programs(0) * nj - 1
    slot = step % 2
    rows = H_A * ts

    def page_copies(bb, jj, sl, lookup=True):
        out = []
        for i in range(pps):
            page = pt_ref[bb, jj * pps + i] if lookup else 0
            out.append(pltpu.make_async_copy(ckv_hbm.at[layer, page], kbuf.at[sl, i], sem.at[sl, 0]))
            out.append(pltpu.make_async_copy(kpe_t_hbm.at[layer, page], pbuf.at[sl, i], sem.at[sl, 1]))
        return out

    @pl.when(step == 0)
    def _():
        for cp in page_copies(b, j, slot):
            cp.start()

    nxt = jnp.minimum(step + 1, last_step)
    for cp in page_copies(nxt // nj, nxt % nj, 1 - slot):
        cp.start()
    for cp in page_copies(b, j, slot, lookup=False):
        cp.wait()

    ql = qlat_ref[...].reshape(rows, KV_RANK).astype(BF16)
    qp = qpe_ref[...].reshape(rows, ROPE_DIM).astype(BF16)

    @pl.when(j == 0)
    def _():
        kc = nckv_ref[...].astype(BF16)
        kp = nkpe_ref[...].astype(BF16)
        s = _dot_nt(ql, kc) + _dot_nt(qp, kp)
        nb = kc.shape[0] // ts
        t_row = lax.broadcasted_iota(jnp.int32, s.shape, 0) % ts
        col = lax.broadcasted_iota(jnp.int32, s.shape, 1)
        ok = (col // ts == b % nb) & (col % ts <= t_row)
        s = jnp.where(ok, s, NEG)
        m = jnp.max(s, axis=-1, keepdims=True)
        p = jnp.where(ok, jnp.exp(s - m), 0.0)
        m_ref[0] = m
        l_ref[0] = jnp.sum(p, axis=-1, keepdims=True)
        acc_ref[0] = _dot(p.astype(BF16), kc)
        for c in range(1, n_chains):
            m_ref[c] = jnp.full((rows, 1), NEG, F32)
            l_ref[c] = jnp.zeros((rows, 1), F32)
            acc_ref[c] = jnp.zeros((rows, KV_RANK), F32)

    per = pps // n_chains
    for c in range(n_chains):
        kcs = [kbuf[slot, c * per + i].astype(BF16) for i in range(per)]
        kps = [pbuf[slot, c * per + i].astype(BF16) for i in range(per)]
        s = jnp.concatenate([_dot_nt(ql, kc) + _dot(qp, kp) for kc, kp in zip(kcs, kps)], axis=1)
        m_prev = m_ref[c]
        m_new = jnp.maximum(m_prev, jnp.max(s, axis=-1, keepdims=True))
        alpha = jnp.exp(m_prev - m_new)
        p = jnp.exp(s - m_new).astype(BF16)
        l_ref[c] = alpha * l_ref[c] + jnp.sum(p.astype(F32), axis=-1, keepdims=True)
        pv = _dot(p[:, :PAGE_SIZE], kcs[0])
        for i in range(1, per):
            pv += _dot(p[:, i * PAGE_SIZE:(i + 1) * PAGE_SIZE], kcs[i])
        acc_ref[c] = alpha * acc_ref[c] + pv
        m_ref[c] = m_new

    @pl.when(step == last_step)
    def _():
        for cp in page_copies(b, j, 1 - slot, lookup=False):
            cp.wait()

    @pl.when(j == nj - 1)
    def _():
        m = m_ref[0]
        for c in range(1, n_chains):
            m = jnp.maximum(m, m_ref[c])
        l = jnp.zeros((rows, 1), F32)
        acc = jnp.zeros((rows, KV_RANK), F32)
        for c in range(n_chains):
            w = jnp.exp(m_ref[c] - m)
            l += w * l_ref[c]
            acc += w * acc_ref[c]
        o = acc * (1.0 / l)
        for h in range(H_A):
            oh = o[h * ts:(h + 1) * ts].astype(BF16)
            o_ref[:, h * HEAD_DIM:(h + 1) * HEAD_DIM] = _dot(oh, wuv_ref[h]).astype(o_ref.dtype)


def _decode_attn(qlat, qpe, ckv_new, kpe_new, cache_ckv, cache_kpe_t, page_table, w_uv_t, layer, ts, pps, n_chains):
    nseq, n_pages = page_table.shape
    rows = H_A * ts
    new_rows = min(LANES, nseq * ts)
    nb = new_rows // ts

    grid_spec = pltpu.PrefetchScalarGridSpec(
        num_scalar_prefetch=1,
        grid=(nseq, n_pages // pps),
        in_specs=[pl.BlockSpec((H_A, ts, KV_RANK), lambda b, j, pt: (0, b, 0)),
                  pl.BlockSpec((H_A, ts, ROPE_DIM), lambda b, j, pt: (0, b, 0)),
                  pl.BlockSpec((new_rows, KV_RANK), lambda b, j, pt: (b // nb, 0)),
                  pl.BlockSpec((new_rows, ROPE_DIM), lambda b, j, pt: (b // nb, 0)),
                  pl.BlockSpec(w_uv_t.shape, lambda b, j, pt: (0, 0, 0)),
                  pl.BlockSpec(memory_space=pl.ANY),
                  pl.BlockSpec(memory_space=pl.ANY)],
        out_specs=pl.BlockSpec((ts, H_A * HEAD_DIM), lambda b, j, pt: (b, 0)),
        scratch_shapes=[pltpu.VMEM((2, pps, PAGE_SIZE, KV_RANK), F32),
                        pltpu.VMEM((2, pps, ROPE_DIM, PAGE_SIZE), F32),
                        pltpu.SemaphoreType.DMA((2, 2)),
                        pltpu.VMEM((n_chains, rows, 1), F32), pltpu.VMEM((n_chains, rows, 1), F32),
                        pltpu.VMEM((n_chains, rows, KV_RANK), F32)],
    )
    return pl.pallas_call(
        functools.partial(_decode_attn_kernel, layer=layer, pps=pps, ts=ts, n_chains=n_chains),
        grid_spec=grid_spec,
        out_shape=jax.ShapeDtypeStruct((nseq * ts, H_A * HEAD_DIM), F32),
        compiler_params=_params("arbitrary", "arbitrary"),
        name="decode_attn",
    )(page_table, qlat, qpe, ckv_new, kpe_new, w_uv_t, cache_ckv, cache_kpe_t)


def _shifted(x, prev, j):
    if j == 0:
        return x
    rx = pltpu.roll(x, j, axis=1)
    rp = pltpu.roll(prev, j, axis=1)
    row = lax.broadcasted_iota(jnp.int32, prev.shape, 1)
    head = jnp.where(row < j, rp, rx[:, :SUBLANES])
    if x.shape[1] == SUBLANES:
        return head
    return jnp.concatenate([head, rx[:, SUBLANES:]], axis=1)


def _dwconv(x, prev, w):
    k = w.shape[0]
    y = x * w[k - 1:k]
    for j in range(k - 1):
        y = y + _shifted(x, prev, k - 1 - j) * w[j:j + 1]
    return y


def _gdn_prep_kernel(x_ref, prev_ref, misc_ref, cw_ref, alog_ref, dtb_ref,
                     q_ref, k_ref, v_ref, gb_ref, cs_ref, *, seq_start_every):
    x = x_ref[...]
    nb, r, c = x.shape
    prev = prev_ref[...]
    if seq_start_every:
        first = (pl.program_id(0) % seq_start_every) == 0
        prev = jnp.where(first, 0.0, prev)
    y = _silu(_dwconv(x, prev, cw_ref[...])).reshape(nb * r, c)
    cs_ref[...] = x[:, r - SUBLANES:, :]
    for h in range(H_B):
        sl = slice(h * HEAD_DIM, (h + 1) * HEAD_DIM)
        qh = y[:, h * HEAD_DIM:(h + 1) * HEAD_DIM]
        kh = y[:, W_B + h * HEAD_DIM:W_B + (h + 1) * HEAD_DIM]
        q_ref[:, sl] = qh * lax.rsqrt(jnp.sum(qh * qh, axis=-1, keepdims=True) + EPS) * (HEAD_DIM ** -0.5)
        k_ref[:, sl] = kh * lax.rsqrt(jnp.sum(kh * kh, axis=-1, keepdims=True) + EPS)
    v_ref[...] = y[:, 2 * W_B:]
    ab = misc_ref[...].reshape(nb * r, PROJ_TILE)[:, MISC_AB:MISC_AB + LANES]
    g = -jnp.exp(alog_ref[...]) * _softplus(ab + dtb_ref[...])
    lane = lax.broadcasted_iota(jnp.int32, ab.shape, 1)
    gb_ref[...] = jnp.where(lane < H_B, g, _sigmoid(ab))


def _gdn_prep(proj3, conv_prev, conv_w, a_log, dt_bias, nb, r):
    nseq, t, _ = proj3.shape
    c = 3 * W_B
    steps_per_seq = t // r
    m = nseq * t
    rows = nb * r
    if conv_prev is None:
        assert nb == 1
        rb = r // SUBLANES
        prev_arr = proj3
        prev_spec = pl.BlockSpec((1, SUBLANES, c), lambda i: (i // steps_per_seq,
                                                               jnp.maximum((i % steps_per_seq) * rb - 1, 0), COL_QKV))
        x_map = lambda i: (i // steps_per_seq, i % steps_per_seq, COL_QKV)
        misc_map = lambda i: (i // steps_per_seq, i % steps_per_seq, COL_MISC)
        grid = (nseq * steps_per_seq,)
        seq_start_every = steps_per_seq
    else:
        assert r == t
        prev_arr = conv_prev
        prev_spec = pl.BlockSpec((nb, SUBLANES, c), lambda i: (i, 0, 0))
        x_map = lambda i: (i, 0, COL_QKV)
        misc_map = lambda i: (i, 0, COL_MISC)
        grid = (nseq // nb,)
        seq_start_every = 0
    alog = jnp.zeros((1, LANES), F32).at[0, :H_B].set(a_log.astype(F32))
    dtb = jnp.zeros((1, LANES), F32).at[0, :H_B].set(dt_bias.astype(F32))
    row_spec = lambda w: pl.BlockSpec((rows, w), lambda i: (i, 0))
    q, k, v, gb, cs = pl.pallas_call(
        functools.partial(_gdn_prep_kernel, seq_start_every=seq_start_every),
        grid=grid,
        in_specs=[pl.BlockSpec((nb, r, c), x_map), prev_spec,
                  pl.BlockSpec((nb, r, PROJ_TILE), misc_map),
                  pl.BlockSpec((CONV_B, c), lambda i: (0, 0)),
                  pl.BlockSpec((1, LANES), lambda i: (0, 0)),
                  pl.BlockSpec((1, LANES), lambda i: (0, 0))],
        out_specs=[row_spec(W_B), row_spec(W_B), row_spec(W_B), row_spec(LANES),
                   pl.BlockSpec((nb, SUBLANES, c), lambda i: (i, 0, 0))],
        out_shape=[jax.ShapeDtypeStruct((m, W_B), F32), jax.ShapeDtypeStruct((m, W_B), F32),
                   jax.ShapeDtypeStruct((m, W_B), F32), jax.ShapeDtypeStruct((m, LANES), F32),
                   jax.ShapeDtypeStruct((grid[0] * nb, SUBLANES, c), F32)],
        compiler_params=_params("arbitrary"),
        name="gdn_prep",
    )(proj3, prev_arr, proj3, conv_w, alog, dtb)
    cs = cs.reshape(nseq, -1, SUBLANES, c)[:, -1]
    return q, k, v, gb, cs


def _gdn_kernel(q_ref, k_ref, v_ref, gb_ref, z_ref, gn_ref, *rest, nseq_blk, r, zero_init):
    if zero_init:
        o_ref, sout_ref, s_scr, wS_scr, qS_scr, vn_scr = rest
    else:
        s0_ref, o_ref, sout_ref, s_scr, wS_scr, qS_scr, vn_scr = rest
    c = nseq_blk * r
    ci = pl.program_id(1)

    @pl.when(ci == 0)
    def _():
        if zero_init:
            s_scr[...] = jnp.zeros(s_scr.shape, F32)
        else:
            s_scr[...] = s0_ref[...]

    row = lax.broadcasted_iota(jnp.int32, (c, c), 0)
    col = lax.broadcasted_iota(jnp.int32, (c, c), 1)
    same = (row // r) == (col // r)
    causal = same & (col <= row)
    strict = same & (col < row)
    eye = (row == col).astype(F32)

    gb = gb_ref[...]
    gcs = _dot_exact_lhs(causal.astype(F32), gb)
    gtot = _dot_exact_lhs(same.astype(F32), gb)
    gcs_t = gcs.T
    n_sq = max(int(np.ceil(np.log2(r))) - 1, 0)

    for h in range(H_B):
        sl = slice(h * HEAD_DIM, (h + 1) * HEAD_DIM)
        qh, kh, vh = q_ref[:, sl], k_ref[:, sl], v_ref[:, sl]
        gc = gcs[:, h:h + 1]
        gr = gcs_t[h:h + 1, :]
        gl = gtot[:, h:h + 1]
        beta = gb[:, H_B + h:H_B + h + 1]
        decay = jnp.exp(jnp.where(causal, gc - gr, NEG))
        kb = kh * beta
        mm = jnp.where(strict, _dot_nt(kb.astype(BF16), kh.astype(BF16)) * decay, 0.0)
        tinv = eye - mm
        pw = mm
        for _ in range(n_sq):
            pw = _dot3(pw, pw)
            tinv = tinv + _dot3(tinv, pw)
        egc = jnp.exp(gc)
        u = _dot3(tinv, vh * beta)
        w = _dot3(tinv, kb * egc)
        qk = _dot_nt(qh.astype(BF16), kh.astype(BF16)) * decay
        q_dec = (qh * egc).astype(BF16)
        k_dec_t = (kh * jnp.exp(gl - gc)).T
        w_b = w.astype(BF16)

        if nseq_blk == 1:
            sb = s_scr[0, h].astype(BF16)
            v_new = u - _dot(w_b, sb)
            o = _dot(q_dec, sb) + _dot(qk.astype(BF16), v_new.astype(BF16))
            s_scr[0, h] = s_scr[0, h] * jnp.exp(gl[c - 1:c, :]) + _dot(k_dec_t.astype(BF16), v_new.astype(BF16))
        else:
            rowc = lax.broadcasted_iota(jnp.int32, (c, HEAD_DIM), 0) // r
            colc = lax.broadcasted_iota(jnp.int32, (HEAD_DIM, c), 1) // r
            wS_scr[...] = jnp.zeros(wS_scr.shape, F32)
            qS_scr[...] = jnp.zeros(qS_scr.shape, F32)

            def read_state(p, _):
                sb = s_scr[p, h].astype(BF16)
                mine = rowc == p
                wS_scr[...] += _dot(jnp.where(mine, w_b, jnp.zeros_like(w_b)), sb)
                qS_scr[...] += _dot(jnp.where(mine, q_dec, jnp.zeros_like(q_dec)), sb)
                return 0

            lax.fori_loop(0, nseq_blk, read_state, 0, unroll=4)
            v_new = u - wS_scr[...]
            o = qS_scr[...] + _dot(qk.astype(BF16), v_new.astype(BF16))
            vn_scr[...] = v_new.astype(BF16)
            e_gl = jnp.exp(gl)

            def write_state(p, _):
                kd = jnp.where(colc == p, k_dec_t, 0.0).astype(BF16)
                g_last = jnp.max(jnp.where(rowc[:, :1] == p, e_gl, 0.0), axis=0, keepdims=True)
                s_scr[p, h] = s_scr[p, h] * g_last + _dot(kd, vn_scr[...])
                return 0

            lax.fori_loop(0, nseq_blk, write_state, 0, unroll=4)

        on = _rms(o, gn_ref[...])
        o_ref[:, sl] = (on * _silu(z_ref[:, sl])).astype(o_ref.dtype)

    @pl.when(ci == pl.num_programs(1) - 1)
    def _():
        sout_ref[...] = s_scr[...]


def _gdn(q, k, v, gb, proj, gdn_norm, s0, nseq, t):
    m = nseq * t
    if t >= CHUNK:
        nseq_blk, r = 1, CHUNK
    else:
        nseq_blk, r = CHUNK // t, t
    n_chunks = t // r
    n_blocks = nseq // nseq_blk
    row_map = lambda i, ci: (i * n_chunks + ci, 0)
    state_spec = pl.BlockSpec((nseq_blk, H_B, HEAD_DIM, HEAD_DIM), lambda i, ci: (i, 0, 0, 0))
    in_specs = [pl.BlockSpec((CHUNK, W_B), row_map), pl.BlockSpec((CHUNK, W_B), row_map),
                pl.BlockSpec((CHUNK, W_B), row_map), pl.BlockSpec((CHUNK, LANES), row_map),
                pl.BlockSpec((CHUNK, W_B), lambda i, ci: (i * n_chunks + ci, COL_ZB)),
                pl.BlockSpec((1, HEAD_DIM), lambda i, ci: (0, 0))]
    args = [q, k, v, gb, proj, gdn_norm.reshape(1, HEAD_DIM)]
    if s0 is not None:
        in_specs.append(state_spec)
        args.append(s0)
    return pl.pallas_call(
        functools.partial(_gdn_kernel, nseq_blk=nseq_blk, r=r, zero_init=s0 is None),
        grid=(n_blocks, n_chunks),
        in_specs=in_specs,
        out_specs=[pl.BlockSpec((CHUNK, W_B), row_map), state_spec],
        out_shape=[jax.ShapeDtypeStruct((m, W_B), BF16),
                   jax.ShapeDtypeStruct((nseq, H_B, HEAD_DIM, HEAD_DIM), F32)],
        scratch_shapes=[pltpu.VMEM((nseq_blk, H_B, HEAD_DIM, HEAD_DIM), F32),
                        pltpu.VMEM((CHUNK, HEAD_DIM), F32), pltpu.VMEM((CHUNK, HEAD_DIM), F32),
                        pltpu.VMEM((CHUNK, HEAD_DIM), BF16)],
        compiler_params=_params("parallel", "arbitrary"),
        name="gdn",
    )(*args)


def _hgrn_kernel(qc_ref, fc_ref, ic_ref, zc_ref, lb_ref, hn_ref, *rest, cs, per_chunk_state, zero_init):
    if zero_init:
        o_ref, sout_ref, st_scr = rest
    else:
        s0_ref, o_ref, sout_ref, st_scr = rest
    c = qc_ref.shape[0]
    n_ch = c // cs
    step = pl.program_id(1)

    if not per_chunk_state:
        @pl.when(step == 0)
        def _():
            st_scr[...] = jnp.zeros(st_scr.shape, F32)

    lb = lb_ref[...]
    fr = fc_ref[...]
    log_lb = jnp.log(jnp.maximum(lb, LB_FLOOR))
    a = log_lb
    b = jnp.log1p(-lb) - _softplus(-fr)
    mx = jnp.maximum(a, b)
    logf = mx + jnp.log1p(jnp.exp(-jnp.abs(a - b)))
    kk = (1.0 - lb) * _sigmoid(-fr)
    qq = _silu(qc_ref[...])

    row = lax.broadcasted_iota(jnp.int32, (c, c), 0)
    col = lax.broadcasted_iota(jnp.int32, (c, c), 1)
    same = (row // cs) == (col // cs)
    bc_all = _dot_exact_lhs((same & (col <= row)).astype(F32), logf)
    btot_all = _dot_exact_lhs(same.astype(F32), logf)
    t_idx = lax.broadcasted_iota(jnp.int32, (n_ch, cs, HEAD_DIM), 1)
    rowc = lax.broadcasted_iota(jnp.int32, (c, HEAD_DIM), 0) // cs

    for h in range(H_C):
        sl = slice(h * HEAD_DIM, (h + 1) * HEAD_DIM)
        q, k, v = qq[:, sl], kk[:, sl], ic_ref[:, sl]
        bc, btot = bc_all[:, sl], btot_all[:, sl]
        q3, k3, v3, bc3 = (x.reshape(n_ch, cs, HEAD_DIM) for x in (q, k, v, bc))
        o3 = jnp.zeros((n_ch, cs, HEAD_DIM), F32)
        for s in range(cs):
            dec = jnp.exp(jnp.where(t_idx >= s, bc3 - bc3[:, s:s + 1, :], NEG))
            a_col = jnp.sum(q3 * k3[:, s:s + 1, :] * dec, axis=-1, keepdims=True)
            o3 = o3 + a_col * v3[:, s:s + 1, :]
        o = o3.reshape(c, HEAD_DIM)

        q_dec = (q * jnp.exp(bc)).astype(BF16)
        k_dec = (k * jnp.exp(btot - bc)).astype(BF16)
        v_t = v.T.astype(BF16)
        e_tot = jnp.exp(btot)

        if per_chunk_state:
            def chunk_step(n, o_acc):
                mine = rowc == n
                st = s0_ref[n, h].T
                o_acc = o_acc + _dot_nt(jnp.where(mine, q_dec, jnp.zeros_like(q_dec)), st.astype(BF16))
                dec_n = jnp.max(jnp.where(mine, e_tot, 0.0), axis=0, keepdims=True)
                st = st * dec_n + _dot(v_t, jnp.where(mine, k_dec, jnp.zeros_like(k_dec)))
                sout_ref[n, h] = st.T
                return o_acc

            o = lax.fori_loop(0, n_ch, chunk_step, o, unroll=4)
        else:
            st = st_scr[h]
            pieces = []
            for n in range(n_ch):
                rs = slice(n * cs, (n + 1) * cs)
                pieces.append(o[rs] + _dot_nt(q_dec[rs], st.astype(BF16)))
                kv = _dot(v_t, jnp.where(rowc == n, k_dec, jnp.zeros_like(k_dec)))
                st = st * e_tot[n * cs:n * cs + 1] + kv
            st_scr[h] = st
            o = jnp.concatenate(pieces, axis=0)
        on = _rms(o, hn_ref[...])
        o_ref[:, sl] = (on * _silu(zc_ref[:, sl])).astype(o_ref.dtype)

    if not per_chunk_state:
        @pl.when(step == pl.num_programs(1) - 1)
        def _():
            for h in range(H_C):
                sout_ref[0, h] = st_scr[h].T


def _hgrn(proj, lb, hg_norm, s0, nseq, t):
    m = nseq * t
    cs = min(HG_CHUNK, t)
    per_chunk_state = t <= HG_CHUNK
    if per_chunk_state:
        assert s0 is not None
        seq_blk = CHUNK // t
        grid = (nseq // seq_blk, 1)
        state_spec = pl.BlockSpec((seq_blk, H_C, HEAD_DIM, HEAD_DIM), lambda i, j: (i, 0, 0, 0))
        steps = 1
    else:
        assert s0 is None
        steps = t // CHUNK
        grid = (nseq, steps)
        state_spec = pl.BlockSpec((1, H_C, HEAD_DIM, HEAD_DIM), lambda i, j: (i, 0, 0, 0))
    col = lambda tile: pl.BlockSpec((CHUNK, W_C), lambda i, j: (i * steps + j, tile))
    in_specs = [col(COL_QC), col(COL_FC), col(COL_IC), col(COL_ZC),
                pl.BlockSpec((1, W_C), lambda i, j: (0, 0)),
                pl.BlockSpec((1, HEAD_DIM), lambda i, j: (0, 0))]
    args = [proj, proj, proj, proj, lb.reshape(1, W_C), hg_norm.reshape(1, HEAD_DIM)]
    if s0 is not None:
        in_specs.append(state_spec)
        args.append(s0)
    return pl.pallas_call(
        functools.partial(_hgrn_kernel, cs=cs, per_chunk_state=per_chunk_state, zero_init=s0 is None),
        grid=grid,
        in_specs=in_specs,
        out_specs=[pl.BlockSpec((CHUNK, W_C), lambda i, j: (i * steps + j, 0)), state_spec],
        out_shape=[jax.ShapeDtypeStruct((m, W_C), BF16),
                   jax.ShapeDtypeStruct((nseq, H_C, HEAD_DIM, HEAD_DIM), F32)],
        scratch_shapes=[pltpu.VMEM((H_C, HEAD_DIM, HEAD_DIM), F32)],
        compiler_params=_params("parallel", "arbitrary"),
        name="hgrn",
    )(*args)


def _out_proj_kernel(x_ref, oa_ref, ob_ref, oc_ref, w_ref, o_ref):
    wa = H_A * HEAD_DIM
    acc = _dot(oa_ref[...].astype(BF16), w_ref[:wa])
    acc += _dot(ob_ref[...], w_ref[wa:wa + W_B])
    acc += _dot(oc_ref[...], w_ref[wa + W_B:])
    o_ref[...] = x_ref[...] + acc


def _out_proj(x, oa, ob, oc, w, tm, tn):
    m, d = x.shape
    kdim = w.shape[0]
    return pl.pallas_call(
        _out_proj_kernel,
        grid=(m // tm, d // tn),
        in_specs=[pl.BlockSpec((tm, tn), lambda i, j: (i, j)),
                  pl.BlockSpec((tm, oa.shape[1]), lambda i, j: (i, 0)),
                  pl.BlockSpec((tm, ob.shape[1]), lambda i, j: (i, 0)),
                  pl.BlockSpec((tm, oc.shape[1]), lambda i, j: (i, 0)),
                  pl.BlockSpec((kdim, tn), lambda i, j: (0, j))],
        out_specs=pl.BlockSpec((tm, tn), lambda i, j: (i, j)),
        out_shape=jax.ShapeDtypeStruct((m, d), F32),
        compiler_params=_params("parallel", "arbitrary"),
        name="out_proj",
    )(x, oa, ob, oc, w)


def _ffn_kernel(x_ref, prev_ref, prevu_ref, g_ref, wg_ref, wu_ref, cwg_ref, cwu_ref, wd_ref, gf_ref,
                o_ref, csg_ref, csu_ref, xn_ref, pn_ref, acc_ref, *, seq_start_every, prev_is_state, final_norm):
    j = pl.program_id(1)
    nb, r, d = x_ref.shape
    rows = nb * r

    @pl.when(j == 0)
    def _():
        xn_ref[...] = _rms(x_ref[...].reshape(rows, d), g_ref[...]).astype(BF16)
        acc_ref[...] = jnp.zeros(acc_ref.shape, F32)
        if not prev_is_state:
            pn = _rms(prev_ref[...].reshape(SUBLANES, d), g_ref[...])
            pn_ref[...] = jnp.concatenate([pn, jnp.zeros_like(pn)], axis=0).astype(BF16)

    xn = xn_ref[...]
    tn = wg_ref.shape[1]
    if not prev_is_state:
        first = (pl.program_id(0) % seq_start_every) == 0
    ws = tn // FFN_SUB
    down = None
    for k in range(FFN_SUB):
        cs = slice(k * ws, (k + 1) * ws)
        ug = _dot(xn, wg_ref[:, cs]).reshape(nb, r, ws)
        uu = _dot(xn, wu_ref[:, cs]).reshape(nb, r, ws)
        if prev_is_state:
            pg = prev_ref[:, :, cs]
            pu = prevu_ref[:, :, cs]
        else:
            pg = jnp.where(first, 0.0, _dot(pn_ref[...], wg_ref[:, cs])[:SUBLANES]).reshape(1, SUBLANES, ws)
            pu = jnp.where(first, 0.0, _dot(pn_ref[...], wu_ref[:, cs])[:SUBLANES]).reshape(1, SUBLANES, ws)
        csg_ref[:, :, cs] = ug[:, r - SUBLANES:, :]
        csu_ref[:, :, cs] = uu[:, r - SUBLANES:, :]
        gate = _dwconv(ug, pg, cwg_ref[:, cs]).reshape(rows, ws)
        up = _dwconv(uu, pu, cwu_ref[:, cs]).reshape(rows, ws)
        hidden = (_silu(gate) * up).astype(BF16)
        part = _dot(hidden, wd_ref[cs, :])
        down = part if down is None else down + part
    acc_ref[...] += down

    @pl.when(j == pl.num_programs(1) - 1)
    def _():
        y = x_ref[...].reshape(rows, d) + acc_ref[...]
        if final_norm:
            y = _rms(y, gf_ref[...])
        o_ref[...] = y.reshape(nb, r, d)


def _ffn(x3, conv_prev, norm_g, w_up, conv_w, w_down, norm_final, nb, r, tn, final_norm):
    nseq, t, d = x3.shape
    nj = D_FF // tn
    steps_per_seq = t // r
    if conv_prev is None:
        assert nb == 1
        rb = r // SUBLANES
        prev_arr = x3
        prev_spec = pl.BlockSpec((1, SUBLANES, d), lambda i, j: (i // steps_per_seq,
                                                                 jnp.maximum((i % steps_per_seq) * rb - 1, 0), 0))
        prevu_spec = prev_spec
        x_map = lambda i, j: (i // steps_per_seq, i % steps_per_seq, 0)
        grid = (nseq * steps_per_seq, nj)
    else:
        assert r == t
        prev_arr = conv_prev
        prev_spec = pl.BlockSpec((nb, SUBLANES, tn), lambda i, j: (i, 0, j))
        prevu_spec = pl.BlockSpec((nb, SUBLANES, tn), lambda i, j: (i, 0, nj + j))
        x_map = lambda i, j: (i, 0, 0)
        grid = (nseq // nb, nj)
    rows = nb * r
    out, csg, csu = pl.pallas_call(
        functools.partial(_ffn_kernel, seq_start_every=steps_per_seq, prev_is_state=conv_prev is not None,
                          final_norm=final_norm),
        grid=grid,
        in_specs=[pl.BlockSpec((nb, r, d), x_map), prev_spec, prevu_spec,
                  pl.BlockSpec((1, d), lambda i, j: (0, 0)),
                  pl.BlockSpec((d, tn), lambda i, j: (0, j)),
                  pl.BlockSpec((d, tn), lambda i, j: (0, nj + j)),
                  pl.BlockSpec((FFN_CONV, tn), lambda i, j: (0, j)),
                  pl.BlockSpec((FFN_CONV, tn), lambda i, j: (0, nj + j)),
                  pl.BlockSpec((tn, d), lambda i, j: (j, 0)),
                  pl.BlockSpec((1, d), lambda i, j: (0, 0))],
        out_specs=[pl.BlockSpec((nb, r, d), x_map),
                   pl.BlockSpec((nb, SUBLANES, tn), lambda i, j: (i, 0, j)),
                   pl.BlockSpec((nb, SUBLANES, tn), lambda i, j: (i, 0, j))],
        out_shape=[jax.ShapeDtypeStruct((nseq, t, d), F32),
                   jax.ShapeDtypeStruct((grid[0] * nb, SUBLANES, D_FF), F32),
                   jax.ShapeDtypeStruct((grid[0] * nb, SUBLANES, D_FF), F32)],
        scratch_shapes=[pltpu.VMEM((rows, d), BF16), pltpu.VMEM((2 * SUBLANES, d), BF16),
                        pltpu.VMEM((rows, d), F32)],
        compiler_params=_params("arbitrary", "arbitrary"),
        name="conv_ffn",
    )(x3, prev_arr, prev_arr, norm_g.reshape(1, d), w_up, w_up, conv_w, conv_w, w_down, norm_final.reshape(1, d))
    csg = csg.reshape(nseq, -1, SUBLANES, D_FF)[:, -1]
    csu = csu.reshape(nseq, -1, SUBLANES, D_FF)[:, -1]
    return out, csg, csu


def _pack_w_in(w):
    d = w.shape[0]
    cuts = np.cumsum([Q_RANK, KV_RANK, ROPE_DIM, 3 * W_B, H_B, H_B, W_B, W_C, W_C, W_C, W_C])
    c_q, c_kv, k_pe, qkv, a_b, b_b, z_b, q_c, f_c, i_c, z_c = jnp.split(w, [int(c) for c in cuts[:-1]], axis=1)
    z = lambda n: jnp.zeros((d, n), w.dtype)
    misc = jnp.concatenate([c_kv, k_pe, z(MISC_AB - MISC_KPE - ROPE_DIM), a_b, b_b,
                            z(PROJ_TILE - MISC_AB - 2 * H_B)], axis=1)
    return jnp.concatenate([qkv, c_q, misc, z_b, q_c, f_c, i_c, z_c], axis=1).astype(BF16)


def _pack_w_uq(w):
    w3 = w.reshape(Q_RANK, H_A, NOPE_DIM + ROPE_DIM)
    return jnp.concatenate([w3[:, :, :NOPE_DIM].reshape(Q_RANK, -1),
                            w3[:, :, NOPE_DIM:].reshape(Q_RANK, -1)], axis=1).astype(BF16)


def _rope_tables(pos):
    half = ROPE_DIM // 2
    inv = ROPE_THETA ** (-jnp.arange(half, dtype=F32) / half)
    ang = pos.astype(F32)[:, None] * inv[None, :]
    cos, sin = jnp.cos(ang), jnp.sin(ang)
    reps = LANES // ROPE_DIM
    return (jnp.tile(jnp.concatenate([cos, cos], axis=1), (1, reps)),
            jnp.tile(jnp.concatenate([-sin, sin], axis=1), (1, reps)))


def _pad_rows_front(a, total):
    n, k, c = a.shape
    return jnp.concatenate([jnp.zeros((n, total - k, c), a.dtype), a], axis=1)


def _row_tile(m):
    for tm in (512, 256, 128, 64, 32, 16, 8):
        if m % tm == 0:
            return tm
    raise ValueError(m)


def _layer(x3, cos, sin, lw, lb, past, final_norm_g, is_last):
    nseq, t, d = x3.shape
    m = nseq * t
    tm = _row_tile(m)
    x = x3.reshape(m, d)
    proj = _norm_matmul(x, lw["norm_mix"], lw["w_in"], tm, 2 * PROJ_TILE)
    proj3 = proj.reshape(nseq, t, PROJ_WIDTH)

    qlat, qpe, ckv, kpe = _mla_prep(proj, cos, sin, lw["q_norm"], lw["kv_norm"], lw["w_uq"], lw["w_uk_t"],
                                    min(tm, 256), BF16 if past is None else F32)
    if past is None:
        o_a = _prompt_attn(qlat, qpe, ckv, kpe, lw["w_uv_t"], nseq, t, min(256, t))
        gdn_prev = hg_s0 = gdn_s0 = ffn_prev = None
        nb, r = 1, min(tm, t)
    else:
        n_pages = past["page_table"].shape[1]
        pps = 32 if n_pages % 32 == 0 else n_pages
        o_a = _decode_attn(qlat, qpe, ckv, kpe, past["cache_ckv"], past["cache_kpe_t"], past["page_table"],
                           lw["w_uv_t"], past["layer"], t, pps, 4)
        gdn_prev = _pad_rows_front(past["gdn_conv"], SUBLANES)
        gdn_s0, hg_s0 = past["gdn_S"], past["hgrn_S"]
        nb, r = min(nseq, 512 // t), t
    q, k, v, gb, gdn_cs = _gdn_prep(proj3, gdn_prev, lw["gdn_conv_w"], lw["gdn_a_log"], lw["gdn_dt_bias"], nb, r)
    o_b, gdn_s = _gdn(q, k, v, gb, proj, lw["gdn_norm"], gdn_s0, nseq, t)
    o_c, hg_s = _hgrn(proj, lb, lw["hg_norm"], hg_s0, nseq, t)
    x = _out_proj(x, o_a, o_b, o_c, lw["w_out"], tm, 1024)

    tn = 512
    ffn_prev = None if past is None else _pad_rows_front(past["ffn_conv"], SUBLANES)
    x3, csg, csu = _ffn(x.reshape(nseq, t, d), ffn_prev, lw["norm_ffn"], lw["w_up"], lw["ffn_conv_w"],
                        lw["w_down"], final_norm_g, nb, r, tn, is_last)
    ffn_cs = jnp.concatenate([csg, csu], axis=2)[:, SUBLANES - (FFN_CONV - 1):]
    new_state = (ckv.reshape(nseq, t, KV_RANK), kpe.reshape(nseq, t, ROPE_DIM), gdn_s,
                 gdn_cs[:, SUBLANES - (CONV_B - 1):], hg_s, ffn_cs)
    return x3, new_state


def kernel(x_prompt, x_sample, cache_ckv, cache_kpe, page_table, state_gdn_S, state_gdn_conv, state_hgrn_S, state_ffn_conv, norm_mix, w_in, mla_q_norm, mla_kv_norm, mla_w_uq, mla_w_uk, mla_w_uv, gdn_conv_w, gdn_a_log, gdn_dt_bias, gdn_norm, hgrn_lb, hgrn_norm, w_out, norm_ffn, ffn_w_up, ffn_conv_w, ffn_w_down, norm_final):
    depth = w_in.shape[0]
    bp, tp, _ = x_prompt.shape
    bs, ts, _ = x_sample.shape
    past_len = page_table.shape[1] * PAGE_SIZE
    cos_p, sin_p = _rope_tables(jnp.tile(jnp.arange(tp, dtype=jnp.int32), bp))
    cos_s, sin_s = _rope_tables(jnp.tile(past_len + jnp.arange(ts, dtype=jnp.int32), bs))
    p_lb = jax.nn.softmax(hgrn_lb.astype(F32), axis=0)
    lb_all = jnp.cumsum(p_lb, axis=0) - p_lb[0:1]
    cache_kpe_t = jnp.swapaxes(cache_kpe, 2, 3)

    xp, xs = x_prompt, x_sample
    new_p, new_s = [], []
    for l in range(depth):
        lw = dict(norm_mix=norm_mix[l], w_in=_pack_w_in(w_in[l]), q_norm=mla_q_norm[l], kv_norm=mla_kv_norm[l],
                  w_uq=_pack_w_uq(mla_w_uq[l]),
                  w_uk_t=jnp.transpose(mla_w_uk[l], (1, 2, 0)).astype(BF16),
                  w_uv_t=jnp.transpose(mla_w_uv[l], (1, 0, 2)).astype(BF16),
                  gdn_conv_w=gdn_conv_w[l], gdn_a_log=gdn_a_log[l], gdn_dt_bias=gdn_dt_bias[l],
                  gdn_norm=gdn_norm[l], hg_norm=hgrn_norm[l], w_out=w_out[l].astype(BF16),
                  norm_ffn=norm_ffn[l], w_up=ffn_w_up[l].astype(BF16), ffn_conv_w=ffn_conv_w[l],
                  w_down=ffn_w_down[l].astype(BF16))
        last = l == depth - 1
        xp, st_p = _layer(xp, cos_p, sin_p, lw, lb_all[l], None, norm_final, last)
        past = dict(cache_ckv=cache_ckv, cache_kpe_t=cache_kpe_t, page_table=page_table, layer=l,
                    gdn_S=state_gdn_S[l], gdn_conv=state_gdn_conv[l], hgrn_S=state_hgrn_S[l],
                    ffn_conv=state_ffn_conv[l])
        xs, st_s = _layer(xs, cos_s, sin_s, lw, lb_all[l], past, norm_final, last)
        new_p.append(st_p)
        new_s.append(st_s)
    outs_p = tuple(jnp.stack(a) for a in zip(*new_p))
    outs_s = tuple(jnp.stack(a) for a in zip(*new_s))
    return (xp, xs) + outs_p + outs_s
```

```python
import functools

import jax
import jax.numpy as jnp
import numpy as np
from jax import lax
from jax.experimental import pallas as pl
from jax.experimental.pallas import tpu as pltpu

F32 = jnp.float32
BF16 = jnp.bfloat16

D_MODEL = 2048
PAGE_SIZE = 128
HEAD_DIM = 128
H_A = 8
H_B = 4
H_C = 4
Q_RANK = 512
KV_RANK = 256
NOPE_DIM = 128
ROPE_DIM = 64
W_B = H_B * HEAD_DIM
W_C = H_C * HEAD_DIM
CONV_B = 4
D_FF = 5632
FFN_CONV = 3
ROPE_THETA = 10000.0
EPS = 1e-6
NEG = -1e30
LB_FLOOR = 1e-30

PROJ_TILE = 512
COL_QKV = 0
COL_CQ = 3
COL_MISC = 4
COL_ZB = 5
COL_QC = 6
COL_FC = 7
COL_IC = 8
COL_ZC = 9
PROJ_WIDTH = 10 * PROJ_TILE
MISC_KPE = 256
MISC_AB = 384

SUBLANES = 8
LANES = 128
CHUNK = 128
HG_CHUNK = 16
FFN_SUB = 1
VMEM_LIMIT = 56 * 1024 * 1024


def _params(*sem):
    return pltpu.CompilerParams(dimension_semantics=sem, vmem_limit_bytes=VMEM_LIMIT)


def _dot(a, b):
    return jnp.dot(a, b, preferred_element_type=F32)


def _dot_nt(a, b):
    return lax.dot_general(a, b, (((1,), (1,)), ((), ())), preferred_element_type=F32)


def _bdot(a, b):
    return _dot(a.astype(BF16), b.astype(BF16))


def _split2(a):
    hi = a.astype(BF16)
    lo = (a - hi.astype(F32)).astype(BF16)
    return hi, lo


def _dot3(a, b):
    ah, al = _split2(a)
    bh, bl = _split2(b)
    return _dot(ah, bh) + (_dot(ah, bl) + _dot(al, bh))


def _dot_exact_lhs(a01, b):
    a = a01.astype(BF16)
    b1 = b.astype(BF16)
    r1 = b - b1.astype(F32)
    b2 = r1.astype(BF16)
    b3 = (r1 - b2.astype(F32)).astype(BF16)
    return _dot(a, b1) + (_dot(a, b2) + _dot(a, b3))


def _sigmoid(x):
    return 1.0 / (1.0 + jnp.exp(-x))


def _silu(x):
    return x * _sigmoid(x)


def _softplus(x):
    return jnp.maximum(x, 0.0) + jnp.log1p(jnp.exp(-jnp.abs(x)))


def _rms(x, g):
    ms = jnp.mean(x * x, axis=-1, keepdims=True)
    return x * lax.rsqrt(ms + EPS) * g


def _swap_halves(x, lane):
    w = x.shape[-1]
    half = ROPE_DIM // 2
    fwd = pltpu.roll(x, w - half, axis=x.ndim - 1)
    bwd = pltpu.roll(x, half, axis=x.ndim - 1)
    return jnp.where((lane % ROPE_DIM) < half, fwd, bwd)


def _rope(x, cos, sin_signed):
    lane = lax.broadcasted_iota(jnp.int32, x.shape, x.ndim - 1)
    return x * cos + _swap_halves(x, lane) * sin_signed


def _norm_matmul_kernel(x_ref, g_ref, w_ref, o_ref, xn_ref):
    @pl.when(pl.program_id(1) == 0)
    def _():
        xn_ref[...] = _rms(x_ref[...], g_ref[...]).astype(BF16)

    o_ref[...] = _dot(xn_ref[...], w_ref[...])


def _norm_matmul(x, g, w, tm, tn):
    m, k = x.shape
    n = w.shape[1]
    return pl.pallas_call(
        _norm_matmul_kernel,
        grid=(m // tm, n // tn),
        in_specs=[pl.BlockSpec((tm, k), lambda i, j: (i, 0)),
                  pl.BlockSpec((1, k), lambda i, j: (0, 0)),
                  pl.BlockSpec((k, tn), lambda i, j: (0, j))],
        out_specs=pl.BlockSpec((tm, tn), lambda i, j: (i, j)),
        out_shape=jax.ShapeDtypeStruct((m, n), F32),
        scratch_shapes=[pltpu.VMEM((tm, k), BF16)],
        compiler_params=_params("parallel", "arbitrary"),
        name="in_proj",
    )(x, g.reshape(1, k), w)


def _mla_prep_kernel(cq_ref, misc_ref, cos_ref, sin_ref, qn_ref, kvn_ref, wuq_ref, wuk_ref,
                     qlat_ref, qpe_ref, ckv_ref, kpe_ref, *, scale):
    cos = cos_ref[...]
    sin = sin_ref[...]
    misc = misc_ref[...]
    ckv_ref[...] = _rms(misc[:, :KV_RANK], kvn_ref[...])
    kpe = _rope(misc[:, MISC_KPE:MISC_KPE + LANES], cos, sin)
    kpe_ref[...] = kpe[:, :ROPE_DIM]

    cqn = _rms(cq_ref[...], qn_ref[...]).astype(BF16)
    q = _dot(cqn, wuq_ref[...])
    n_nope = H_A * NOPE_DIM
    reps = (H_A * ROPE_DIM) // LANES
    q_pe = _rope(q[:, n_nope:], jnp.tile(cos, (1, reps)), jnp.tile(sin, (1, reps))) * scale
    for h in range(H_A):
        qn = q[:, h * NOPE_DIM:(h + 1) * NOPE_DIM].astype(BF16)
        qlat_ref[h] = (_dot(qn, wuk_ref[h]) * scale).astype(qlat_ref.dtype)
        qpe_ref[h] = q_pe[:, h * ROPE_DIM:(h + 1) * ROPE_DIM].astype(qpe_ref.dtype)


def _mla_prep(proj, cos, sin, q_norm, kv_norm, w_uq, w_uk_t, tm, q_dtype):
    m = proj.shape[0]
    scale = float((NOPE_DIM + ROPE_DIM) ** -0.5)
    return pl.pallas_call(
        functools.partial(_mla_prep_kernel, scale=scale),
        grid=(m // tm,),
        in_specs=[pl.BlockSpec((tm, PROJ_TILE), lambda i: (i, COL_CQ)),
                  pl.BlockSpec((tm, PROJ_TILE), lambda i: (i, COL_MISC)),
                  pl.BlockSpec((tm, LANES), lambda i: (i, 0)),
                  pl.BlockSpec((tm, LANES), lambda i: (i, 0)),
                  pl.BlockSpec((1, Q_RANK), lambda i: (0, 0)),
                  pl.BlockSpec((1, KV_RANK), lambda i: (0, 0)),
                  pl.BlockSpec(w_uq.shape, lambda i: (0, 0)),
                  pl.BlockSpec(w_uk_t.shape, lambda i: (0, 0, 0))],
        out_specs=[pl.BlockSpec((H_A, tm, KV_RANK), lambda i: (0, i, 0)),
                   pl.BlockSpec((H_A, tm, ROPE_DIM), lambda i: (0, i, 0)),
                   pl.BlockSpec((tm, KV_RANK), lambda i: (i, 0)),
                   pl.BlockSpec((tm, ROPE_DIM), lambda i: (i, 0))],
        out_shape=[jax.ShapeDtypeStruct((H_A, m, KV_RANK), q_dtype),
                   jax.ShapeDtypeStruct((H_A, m, ROPE_DIM), q_dtype),
                   jax.ShapeDtypeStruct((m, KV_RANK), F32),
                   jax.ShapeDtypeStruct((m, ROPE_DIM), F32)],
        compiler_params=_params("parallel"),
        name="mla_prep",
    )(proj, proj, cos, sin, q_norm.reshape(1, -1), kv_norm.reshape(1, -1), w_uq, w_uk_t)


def _lanes(x, n):
    return x if n == LANES else jnp.concatenate([x] * (n // LANES), axis=1)


def _prompt_attn_kernel(qi_tab, ki_tab, qlat_ref, qpe_ref, ckv_ref, kpe_ref, wuv_ref, o_ref,
                        m_ref, l_ref, acc_ref):
    step = pl.program_id(1)
    qi = qi_tab[step]
    ki = ki_tab[step]
    tq, tk = qlat_ref.shape[1], ckv_ref.shape[0]

    @pl.when(ki == 0)
    def _():
        m_ref[...] = jnp.full(m_ref.shape, NEG, F32)
        l_ref[...] = jnp.zeros(l_ref.shape, F32)
        acc_ref[...] = jnp.zeros(acc_ref.shape, F32)

    def update(masked):
        kc = ckv_ref[...].astype(BF16)
        kp = kpe_ref[...].astype(BF16)
        if masked:
            keep = (lax.broadcasted_iota(jnp.int32, (tq, tk), 1)
                    <= lax.broadcasted_iota(jnp.int32, (tq, tk), 0))
        scores = lambda h: _dot_nt(qlat_ref[h], kc) + _dot_nt(qpe_ref[h], kp)
        s_next = scores(0)
        for h in range(H_A):
            s = s_next
            if h + 1 < H_A:
                s_next = scores(h + 1)
            if masked:
                s = jnp.where(keep, s, NEG)
            m_prev = m_ref[h]
            m_new = jnp.maximum(m_prev, jnp.max(s, axis=-1, keepdims=True))
            alpha = jnp.exp(m_prev - m_new)
            p = jnp.exp(s - _lanes(m_new, tk))
            l_ref[h] = alpha * l_ref[h] + jnp.sum(p, axis=-1, keepdims=True)
            acc_ref[h] = _lanes(alpha, KV_RANK) * acc_ref[h] + _dot(p.astype(BF16), kc)
            m_ref[h] = m_new

    @pl.when(ki < qi)
    def _():
        update(False)

    @pl.when(ki == qi)
    def _():
        update(True)
        for h in range(H_A):
            o = (acc_ref[h] * _lanes(1.0 / l_ref[h], KV_RANK)).astype(BF16)
            o_ref[:, h * HEAD_DIM:(h + 1) * HEAD_DIM] = _dot(o, wuv_ref[h]).astype(o_ref.dtype)


def _prompt_attn(qlat, qpe, ckv, kpe, w_uv_t, nseq, t, tq):
    nq = t // tq
    pairs = [(qi, ki) for qi in range(nq) for ki in range(qi + 1)]
    qi_tab = jnp.asarray([p[0] for p in pairs], jnp.int32)
    ki_tab = jnp.asarray([p[1] for p in pairs], jnp.int32)
    q_map = lambda b, s, qt, kt: (0, b * nq + qt[s], 0)
    kv_map = lambda b, s, qt, kt: (b * nq + kt[s], 0)
    grid_spec = pltpu.PrefetchScalarGridSpec(
        num_scalar_prefetch=2,
        grid=(nseq, len(pairs)),
        in_specs=[pl.BlockSpec((H_A, tq, KV_RANK), q_map),
                  pl.BlockSpec((H_A, tq, ROPE_DIM), q_map),
                  pl.BlockSpec((tq, KV_RANK), kv_map),
                  pl.BlockSpec((tq, ROPE_DIM), kv_map),
                  pl.BlockSpec(w_uv_t.shape, lambda b, s, qt, kt: (0, 0, 0))],
        out_specs=pl.BlockSpec((tq, H_A * HEAD_DIM), lambda b, s, qt, kt: (b * nq + qt[s], 0)),
        scratch_shapes=[pltpu.VMEM((H_A, tq, LANES), F32), pltpu.VMEM((H_A, tq, LANES), F32),
                        pltpu.VMEM((H_A, tq, KV_RANK), F32)],
    )
    return pl.pallas_call(
        _prompt_attn_kernel,
        grid_spec=grid_spec,
        out_shape=jax.ShapeDtypeStruct((nseq * t, H_A * HEAD_DIM), BF16),
        compiler_params=_params("parallel", "arbitrary"),
        name="prompt_attn",
    )(qi_tab, ki_tab, qlat, qpe, ckv, kpe, w_uv_t)


def _decode_attn_kernel(pt_ref, qlat_ref, qpe_ref, nckv_ref, nkpe_ref, wuv_ref, ckv_hbm, kpe_t_hbm,
                        o_ref, kbuf, pbuf, sem, m_ref, l_ref, acc_ref, *, layer, pps, ts, n_chains):
    b = pl.program_id(0)
    j = pl.program_id(1)
    nj = pl.num_programs(1)
    step = b * nj + j
    last_step = pl.num_programs(0) * nj - 1
    slot = step % 2
    rows = H_A * ts

    def page_copies(bb, jj, sl, lookup=True):
        out = []
        for i in range(pps):
            page = pt_ref[bb, jj * pps + i] if lookup else 0
            out.append(pltpu.make_async_copy(ckv_hbm.at[layer, page], kbuf.at[sl, i], sem.at[sl, 0]))
            out.append(pltpu.make_async_copy(kpe_t_hbm.at[layer, page], pbuf.at[sl, i], sem.at[sl, 1]))
        return out

    @pl.when(step == 0)
    def _():
        for cp in page_copies(b, j, slot):
            cp.start()

    for cp in page_copies(b, j, slot, lookup=False):
        cp.wait()
    nxt = jnp.minimum(step + 1, last_step)
    prefetch = page_copies(nxt // nj, nxt % nj, 1 - slot)

    ql = qlat_ref[...].reshape(rows, KV_RANK).astype(BF16)
    qp = qpe_ref[...].reshape(rows, ROPE_DIM).astype(BF16)

    @pl.when(j == 0)
    def _():
        kc = nckv_ref[...].astype(BF16)
        kp = nkpe_ref[...].astype(BF16)
        s = _dot_nt(ql, kc) + _dot_nt(qp, kp)
        nb = kc.shape[0] // ts
        t_row = lax.broadcasted_iota(jnp.int32, s.shape, 0) % ts
        col = lax.broadcasted_iota(jnp.int32, s.shape, 1)
        ok = (col // ts == b % nb) & (col % ts <= t_row)
        s = jnp.where(ok, s, NEG)
        m = jnp.max(s, axis=-1, keepdims=True)
        p = jnp.where(ok, jnp.exp(s - m), 0.0)
        m_ref[0] = m
        l_ref[0] = jnp.sum(p, axis=-1, keepdims=True)
        acc_ref[0] = _dot(p.astype(BF16), kc)
        for c in range(1, n_chains):
            m_ref[c] = jnp.full((rows, 1), NEG, F32)
            l_ref[c] = jnp.zeros((rows, 1), F32)
            acc_ref[c] = jnp.zeros((rows, KV_RANK), F32)

    per = pps // n_chains
    def chain_scores(c):
        kcs, scores = [], []
        for i in range(c * per, (c + 1) * per):
            prefetch[2 * i].start()
            prefetch[2 * i + 1].start()
            kcs.append(kbuf[slot, i].astype(BF16))
            scores.append(_dot_nt(ql, kcs[-1]) + _dot(qp, pbuf[slot, i].astype(BF16)))
        return kcs, jnp.concatenate(scores, axis=1)

    nxt_chain = chain_scores(0)
    for c in range(n_chains):
        kcs, s = nxt_chain
        if c + 1 < n_chains:
            nxt_chain = chain_scores(c + 1)
        m_prev = m_ref[c]
        m_new = jnp.maximum(m_prev, jnp.max(s, axis=-1, keepdims=True))
        alpha = jnp.exp(m_prev - m_new)
        p = jnp.exp(s - m_new).astype(BF16)
        l_ref[c] = alpha * l_ref[c] + jnp.sum(p.astype(F32), axis=-1, keepdims=True)
        pv = _dot(p[:, :PAGE_SIZE], kcs[0])
        for i in range(1, per):
            pv += _dot(p[:, i * PAGE_SIZE:(i + 1) * PAGE_SIZE], kcs[i])
        acc_ref[c] = alpha * acc_ref[c] + pv
        m_ref[c] = m_new

    @pl.when(step == last_step)
    def _():
        for cp in page_copies(b, j, 1 - slot, lookup=False):
            cp.wait()

    @pl.when(j == nj - 1)
    def _():
        m = m_ref[0]
        for c in range(1, n_chains):
            m = jnp.maximum(m, m_ref[c])
        l = jnp.zeros((rows, 1), F32)
        acc = jnp.zeros((rows, KV_RANK), F32)
        for c in range(n_chains):
            w = jnp.exp(m_ref[c] - m)
            l += w * l_ref[c]
            acc += w * acc_ref[c]
        o = acc * (1.0 / l)
        for h in range(H_A):
            oh = o[h * ts:(h + 1) * ts].astype(BF16)
            o_ref[:, h * HEAD_DIM:(h + 1) * HEAD_DIM] = _dot(oh, wuv_ref[h]).astype(o_ref.dtype)


def _decode_attn(qlat, qpe, ckv_new, kpe_new, cache_ckv, cache_kpe_t, page_table, w_uv_t, layer, ts, pps, n_chains):
    nseq, n_pages = page_table.shape
    rows = H_A * ts
    new_rows = min(LANES, nseq * ts)
    nb = new_rows // ts

    grid_spec = pltpu.PrefetchScalarGridSpec(
        num_scalar_prefetch=1,
        grid=(nseq, n_pages // pps),
        in_specs=[pl.BlockSpec((H_A, ts, KV_RANK), lambda b, j, pt: (0, b, 0)),
                  pl.BlockSpec((H_A, ts, ROPE_DIM), lambda b, j, pt: (0, b, 0)),
                  pl.BlockSpec((new_rows, KV_RANK), lambda b, j, pt: (b // nb, 0)),
                  pl.BlockSpec((new_rows, ROPE_DIM), lambda b, j, pt: (b // nb, 0)),
                  pl.BlockSpec(w_uv_t.shape, lambda b, j, pt: (0, 0, 0)),
                  pl.BlockSpec(memory_space=pl.ANY),
                  pl.BlockSpec(memory_space=pl.ANY)],
        out_specs=pl.BlockSpec((ts, H_A * HEAD_DIM), lambda b, j, pt: (b, 0)),
        scratch_shapes=[pltpu.VMEM((2, pps, PAGE_SIZE, KV_RANK), F32),
                        pltpu.VMEM((2, pps, ROPE_DIM, PAGE_SIZE), F32),
                        pltpu.SemaphoreType.DMA((2, 2)),
                        pltpu.VMEM((n_chains, rows, 1), F32), pltpu.VMEM((n_chains, rows, 1), F32),
                        pltpu.VMEM((n_chains, rows, KV_RANK), F32)],
    )
    return pl.pallas_call(
        functools.partial(_decode_attn_kernel, layer=layer, pps=pps, ts=ts, n_chains=n_chains),
        grid_spec=grid_spec,
        out_shape=jax.ShapeDtypeStruct((nseq * ts, H_A * HEAD_DIM), F32),
        compiler_params=_params("arbitrary", "arbitrary"),
        name="decode_attn",
    )(page_table, qlat, qpe, ckv_new, kpe_new, w_uv_t, cache_ckv, cache_kpe_t)


def _shifted(x, prev, j):
    if j == 0:
        return x
    rx = pltpu.roll(x, j, axis=1)
    rp = pltpu.roll(prev, j, axis=1)
    row = lax.broadcasted_iota(jnp.int32, prev.shape, 1)
    head = jnp.where(row < j, rp, rx[:, :SUBLANES])
    if x.shape[1] == SUBLANES:
        return head
    return jnp.concatenate([head, rx[:, SUBLANES:]], axis=1)


def _dwconv(x, prev, w):
    k = w.shape[0]
    y = x * w[k - 1:k]
    for j in range(k - 1):
        y = y + _shifted(x, prev, k - 1 - j) * w[j:j + 1]
    return y


def _gdn_prep_kernel(x_ref, prev_ref, misc_ref, cw_ref, alog_ref, dtb_ref,
                     q_ref, k_ref, v_ref, gb_ref, cs_ref, *, seq_start_every):
    x = x_ref[...]
    nb, r, c = x.shape
    prev = prev_ref[...]
    if seq_start_every:
        first = (pl.program_id(0) % seq_start_every) == 0
        prev = jnp.where(first, 0.0, prev)
    y = _silu(_dwconv(x, prev, cw_ref[...])).reshape(nb * r, c)
    cs_ref[...] = x[:, r - SUBLANES:, :]
    for h in range(H_B):
        sl = slice(h * HEAD_DIM, (h + 1) * HEAD_DIM)
        qh = y[:, h * HEAD_DIM:(h + 1) * HEAD_DIM]
        kh = y[:, W_B + h * HEAD_DIM:W_B + (h + 1) * HEAD_DIM]
        q_ref[:, sl] = qh * lax.rsqrt(jnp.sum(qh * qh, axis=-1, keepdims=True) + EPS) * (HEAD_DIM ** -0.5)
        k_ref[:, sl] = kh * lax.rsqrt(jnp.sum(kh * kh, axis=-1, keepdims=True) + EPS)
    v_ref[...] = y[:, 2 * W_B:]
    ab = misc_ref[...].reshape(nb * r, PROJ_TILE)[:, MISC_AB:MISC_AB + LANES]
    g = -jnp.exp(alog_ref[...]) * _softplus(ab + dtb_ref[...])
    lane = lax.broadcasted_iota(jnp.int32, ab.shape, 1)
    gb_ref[...] = jnp.where(lane < H_B, g, _sigmoid(ab))


def _gdn_prep(proj3, conv_prev, conv_w, a_log, dt_bias, nb, r):
    nseq, t, _ = proj3.shape
    c = 3 * W_B
    steps_per_seq = t // r
    m = nseq * t
    rows = nb * r
    if conv_prev is None:
        assert nb == 1
        rb = r // SUBLANES
        prev_arr = proj3
        prev_spec = pl.BlockSpec((1, SUBLANES, c), lambda i: (i // steps_per_seq,
                                                               jnp.maximum((i % steps_per_seq) * rb - 1, 0), COL_QKV))
        x_map = lambda i: (i // steps_per_seq, i % steps_per_seq, COL_QKV)
        misc_map = lambda i: (i // steps_per_seq, i % steps_per_seq, COL_MISC)
        grid = (nseq * steps_per_seq,)
        seq_start_every = steps_per_seq
    else:
        assert r == t
        prev_arr = conv_prev
        prev_spec = pl.BlockSpec((nb, SUBLANES, c), lambda i: (i, 0, 0))
        x_map = lambda i: (i, 0, COL_QKV)
        misc_map = lambda i: (i, 0, COL_MISC)
        grid = (nseq // nb,)
        seq_start_every = 0
    alog = jnp.zeros((1, LANES), F32).at[0, :H_B].set(a_log.astype(F32))
    dtb = jnp.zeros((1, LANES), F32).at[0, :H_B].set(dt_bias.astype(F32))
    row_spec = lambda w: pl.BlockSpec((rows, w), lambda i: (i, 0))
    q, k, v, gb, cs = pl.pallas_call(
        functools.partial(_gdn_prep_kernel, seq_start_every=seq_start_every),
        grid=grid,
        in_specs=[pl.BlockSpec((nb, r, c), x_map), prev_spec,
                  pl.BlockSpec((nb, r, PROJ_TILE), misc_map),
                  pl.BlockSpec((CONV_B, c), lambda i: (0, 0)),
                  pl.BlockSpec((1, LANES), lambda i: (0, 0)),
                  pl.BlockSpec((1, LANES), lambda i: (0, 0))],
        out_specs=[row_spec(W_B), row_spec(W_B), row_spec(W_B), row_spec(LANES),
                   pl.BlockSpec((nb, SUBLANES, c), lambda i: (i, 0, 0))],
        out_shape=[jax.ShapeDtypeStruct((m, W_B), F32), jax.ShapeDtypeStruct((m, W_B), F32),
                   jax.ShapeDtypeStruct((m, W_B), F32), jax.ShapeDtypeStruct((m, LANES), F32),
                   jax.ShapeDtypeStruct((grid[0] * nb, SUBLANES, c), F32)],
        compiler_params=_params("arbitrary"),
        name="gdn_prep",
    )(proj3, prev_arr, proj3, conv_w, alog, dtb)
    cs = cs.reshape(nseq, -1, SUBLANES, c)[:, -1]
    return q, k, v, gb, cs


def _gdn_kernel(q_ref, k_ref, v_ref, gb_ref, z_ref, gn_ref, *rest, nseq_blk, r, zero_init):
    if zero_init:
        o_ref, sout_ref, s_scr, wS_scr, qS_scr, vn_scr = rest
    else:
        s0_ref, o_ref, sout_ref, s_scr, wS_scr, qS_scr, vn_scr = rest
    c = nseq_blk * r
    ci = pl.program_id(1)

    @pl.when(ci == 0)
    def _():
        if zero_init:
            s_scr[...] = jnp.zeros(s_scr.shape, F32)
        else:
            s_scr[...] = s0_ref[...]

    row = lax.broadcasted_iota(jnp.int32, (c, c), 0)
    col = lax.broadcasted_iota(jnp.int32, (c, c), 1)
    same = (row // r) == (col // r)
    causal = same & (col <= row)
    strict = same & (col < row)
    eye = (row == col).astype(F32)

    gb = gb_ref[...]
    gcs = _dot_exact_lhs(causal.astype(F32), gb)
    gtot = _dot_exact_lhs(same.astype(F32), gb)
    gcs_t = gcs.T
    n_sq = max(int(np.ceil(np.log2(r))) - 1, 0)

    heads = range(H_B)
    sls = [slice(h * HEAD_DIM, (h + 1) * HEAD_DIM) for h in heads]
    qs = [q_ref[:, sl] for sl in sls]
    ks = [k_ref[:, sl] for sl in sls]
    vs = [v_ref[:, sl] for sl in sls]
    gcl = [gcs[:, h:h + 1] for h in heads]
    gll = [gtot[:, h:h + 1] for h in heads]
    betas = [gb[:, H_B + h:H_B + h + 1] for h in heads]
    decays = [jnp.exp(jnp.where(causal, gcl[h] - gcs_t[h:h + 1, :], NEG)) for h in heads]
    kbs = [ks[h] * betas[h] for h in heads]
    mms = [jnp.where(strict, _dot_nt(kbs[h].astype(BF16), ks[h].astype(BF16)) * decays[h], 0.0) for h in heads]
    tinvs = [eye - mm for mm in mms]
    pws = list(mms)
    for _ in range(n_sq):
        pws = [_dot3(pw, pw) for pw in pws]
        tinvs = [tinv + _dot3(tinv, pw) for tinv, pw in zip(tinvs, pws)]
    egcs = [jnp.exp(gc) for gc in gcl]
    us = [_dot3(tinvs[h], vs[h] * betas[h]) for h in heads]
    ws = [_dot3(tinvs[h], kbs[h] * egcs[h]).astype(BF16) for h in heads]
    qks = [(_dot_nt(qs[h].astype(BF16), ks[h].astype(BF16)) * decays[h]).astype(BF16) for h in heads]
    q_decs = [(qs[h] * egcs[h]).astype(BF16) for h in heads]
    k_dec_ts = [(ks[h] * jnp.exp(gll[h] - gcl[h])).T for h in heads]

    if nseq_blk == 1:
        sbs = [s_scr[0, h].astype(BF16) for h in heads]
        v_news = [us[h] - _dot(ws[h], sbs[h]) for h in heads]
        vnb = [v.astype(BF16) for v in v_news]
        os_ = [_dot(q_decs[h], sbs[h]) + _dot(qks[h], vnb[h]) for h in heads]
        for h in heads:
            s_scr[0, h] = (s_scr[0, h] * jnp.exp(gll[h][c - 1:c, :])
                           + _dot(k_dec_ts[h].astype(BF16), vnb[h]))
        for h in heads:
            on = _rms(os_[h], gn_ref[...])
            o_ref[:, sls[h]] = (on * _silu(z_ref[:, sls[h]])).astype(o_ref.dtype)

    else:
        rowc = lax.broadcasted_iota(jnp.int32, (c, HEAD_DIM), 0) // r
        colc = lax.broadcasted_iota(jnp.int32, (HEAD_DIM, c), 1) // r
        wS_scr[...] = jnp.zeros(wS_scr.shape, F32)
        qS_scr[...] = jnp.zeros(qS_scr.shape, F32)

        def read_state(p, _):
            mine = rowc == p
            for h in heads:
                sb = s_scr[p, h].astype(BF16)
                wS_scr[h] += _dot(jnp.where(mine, ws[h], jnp.zeros_like(ws[h])), sb)
                qS_scr[h] += _dot(jnp.where(mine, q_decs[h], jnp.zeros_like(q_decs[h])), sb)
            return 0

        lax.fori_loop(0, nseq_blk, read_state, 0, unroll=2)
        for h in heads:
            vn_scr[h] = (us[h] - wS_scr[h]).astype(BF16)
            on = _rms(qS_scr[h] + _dot(qks[h], vn_scr[h]), gn_ref[...])
            o_ref[:, sls[h]] = (on * _silu(z_ref[:, sls[h]])).astype(o_ref.dtype)
        e_gls = [jnp.exp(gl) for gl in gll]

        def write_state(p, _):
            for h in heads:
                kd = jnp.where(colc == p, k_dec_ts[h], 0.0).astype(BF16)
                g_last = jnp.max(jnp.where(rowc[:, :1] == p, e_gls[h], 0.0), axis=0, keepdims=True)
                s_scr[p, h] = s_scr[p, h] * g_last + _dot(kd, vn_scr[h])
            return 0

        lax.fori_loop(0, nseq_blk, write_state, 0, unroll=2)

    @pl.when(ci == pl.num_programs(1) - 1)
    def _():
        sout_ref[...] = s_scr[...]


def _gdn(q, k, v, gb, proj, gdn_norm, s0, nseq, t):
    m = nseq * t
    if t >= CHUNK:
        nseq_blk, r = 1, CHUNK
    else:
        nseq_blk, r = CHUNK // t, t
    n_chunks = t // r
    n_blocks = nseq // nseq_blk
    row_map = lambda i, ci: (i * n_chunks + ci, 0)
    state_spec = pl.BlockSpec((nseq_blk, H_B, HEAD_DIM, HEAD_DIM), lambda i, ci: (i, 0, 0, 0))
    in_specs = [pl.BlockSpec((CHUNK, W_B), row_map), pl.BlockSpec((CHUNK, W_B), row_map),
                pl.BlockSpec((CHUNK, W_B), row_map), pl.BlockSpec((CHUNK, LANES), row_map),
                pl.BlockSpec((CHUNK, W_B), lambda i, ci: (i * n_chunks + ci, COL_ZB)),
                pl.BlockSpec((1, HEAD_DIM), lambda i, ci: (0, 0))]
    args = [q, k, v, gb, proj, gdn_norm.reshape(1, HEAD_DIM)]
    if s0 is not None:
        in_specs.append(state_spec)
        args.append(s0)
    return pl.pallas_call(
        functools.partial(_gdn_kernel, nseq_blk=nseq_blk, r=r, zero_init=s0 is None),
        grid=(n_blocks, n_chunks),
        in_specs=in_specs,
        out_specs=[pl.BlockSpec((CHUNK, W_B), row_map), state_spec],
        out_shape=[jax.ShapeDtypeStruct((m, W_B), BF16),
                   jax.ShapeDtypeStruct((nseq, H_B, HEAD_DIM, HEAD_DIM), F32)],
        scratch_shapes=[pltpu.VMEM((nseq_blk, H_B, HEAD_DIM, HEAD_DIM), F32),
                        pltpu.VMEM((H_B, CHUNK, HEAD_DIM), F32), pltpu.VMEM((H_B, CHUNK, HEAD_DIM), F32),
                        pltpu.VMEM((H_B, CHUNK, HEAD_DIM), BF16)],
        compiler_params=_params("parallel", "arbitrary"),
        name="gdn",
    )(*args)


def _hgrn_kernel(qc_ref, fc_ref, ic_ref, zc_ref, lb_ref, hn_ref, *rest, cs, per_chunk_state, zero_init):
    if zero_init:
        o_ref, sout_ref, st_scr = rest
    else:
        s0_ref, o_ref, sout_ref, st_scr = rest
    c = qc_ref.shape[0]
    n_ch = c // cs
    step = pl.program_id(1)

    if not per_chunk_state:
        @pl.when(step == 0)
        def _():
            st_scr[...] = jnp.zeros(st_scr.shape, F32)

    lb = lb_ref[...]
    fr = fc_ref[...]
    log_lb = jnp.log(jnp.maximum(lb, LB_FLOOR))
    a = log_lb
    b = jnp.log1p(-lb) - _softplus(-fr)
    mx = jnp.maximum(a, b)
    logf = mx + jnp.log1p(jnp.exp(-jnp.abs(a - b)))
    kk = (1.0 - lb) * _sigmoid(-fr)
    qq = _silu(qc_ref[...])

    row = lax.broadcasted_iota(jnp.int32, (c, c), 0)
    col = lax.broadcasted_iota(jnp.int32, (c, c), 1)
    same = (row // cs) == (col // cs)
    bc_all = _dot_exact_lhs((same & (col <= row)).astype(F32), logf)
    btot_all = _dot_exact_lhs(same.astype(F32), logf)
    t_idx = lax.broadcasted_iota(jnp.int32, (n_ch, cs, HEAD_DIM), 1)
    rowc = lax.broadcasted_iota(jnp.int32, (c, HEAD_DIM), 0) // cs

    def head(h):
        sl = slice(h * HEAD_DIM, (h + 1) * HEAD_DIM)
        return qq[:, sl], kk[:, sl], ic_ref[:, sl], bc_all[:, sl], btot_all[:, sl]

    def within_chunks(h):
        q, k, v, bc, _ = head(h)
        q3, k3, v3, bc3 = (x.reshape(n_ch, cs, HEAD_DIM) for x in (q, k, v, bc))
        o3 = jnp.zeros((n_ch, cs, HEAD_DIM), F32)
        for s in range(cs):
            dec = jnp.exp(jnp.where(t_idx >= s, bc3 - bc3[:, s:s + 1, :], NEG))
            a_col = jnp.sum(q3 * k3[:, s:s + 1, :] * dec, axis=-1, keepdims=True)
            o3 = o3 + a_col * v3[:, s:s + 1, :]
        return o3.reshape(c, HEAD_DIM)

    def across_chunks(h):
        q, k, v, bc, btot = head(h)
        q_dec = (q * jnp.exp(bc)).astype(BF16)
        k_dec = (k * jnp.exp(btot - bc)).astype(BF16)
        v_t = v.T.astype(BF16)
        e_tot = jnp.exp(btot)
        if per_chunk_state:
            def chunk_step(n, o_acc):
                mine = rowc == n
                st = s0_ref[n, h].T
                o_acc = o_acc + _dot_nt(jnp.where(mine, q_dec, jnp.zeros_like(q_dec)), st.astype(BF16))
                dec_n = jnp.max(jnp.where(mine, e_tot, 0.0), axis=0, keepdims=True)
                st = st * dec_n + _dot(v_t, jnp.where(mine, k_dec, jnp.zeros_like(k_dec)))
                sout_ref[n, h] = st.T
                return o_acc

            return lax.fori_loop(0, n_ch, chunk_step, jnp.zeros((c, HEAD_DIM), F32), unroll=4)
        st = st_scr[h]
        pieces = []
        for n in range(n_ch):
            rs = slice(n * cs, (n + 1) * cs)
            pieces.append(_dot_nt(q_dec[rs], st.astype(BF16)))
            kv = _dot(v_t, jnp.where(rowc == n, k_dec, jnp.zeros_like(k_dec)))
            st = st * e_tot[n * cs:n * cs + 1] + kv
        st_scr[h] = st
        return jnp.concatenate(pieces, axis=0)

    o_state = across_chunks(0)
    for h in range(H_C):
        sl = slice(h * HEAD_DIM, (h + 1) * HEAD_DIM)
        o = o_state
        if h + 1 < H_C:
            o_state = across_chunks(h + 1)
        on = _rms(o + within_chunks(h), hn_ref[...])
        o_ref[:, sl] = (on * _silu(zc_ref[:, sl])).astype(o_ref.dtype)

    if not per_chunk_state:
        @pl.when(step == pl.num_programs(1) - 1)
        def _():
            for h in range(H_C):
                sout_ref[0, h] = st_scr[h].T


def _hgrn(proj, lb, hg_norm, s0, nseq, t):
    m = nseq * t
    cs = min(HG_CHUNK, t)
    per_chunk_state = t <= HG_CHUNK
    if per_chunk_state:
        assert s0 is not None
        seq_blk = CHUNK // t
        grid = (nseq // seq_blk, 1)
        state_spec = pl.BlockSpec((seq_blk, H_C, HEAD_DIM, HEAD_DIM), lambda i, j: (i, 0, 0, 0))
        steps = 1
    else:
        assert s0 is None
        steps = t // CHUNK
        grid = (nseq, steps)
        state_spec = pl.BlockSpec((1, H_C, HEAD_DIM, HEAD_DIM), lambda i, j: (i, 0, 0, 0))
    col = lambda tile: pl.BlockSpec((CHUNK, W_C), lambda i, j: (i * steps + j, tile))
    in_specs = [col(COL_QC), col(COL_FC), col(COL_IC), col(COL_ZC),
                pl.BlockSpec((1, W_C), lambda i, j: (0, 0)),
                pl.BlockSpec((1, HEAD_DIM), lambda i, j: (0, 0))]
    args = [proj, proj, proj, proj, lb.reshape(1, W_C), hg_norm.reshape(1, HEAD_DIM)]
    if s0 is not None:
        in_specs.append(state_spec)
        args.append(s0)
    return pl.pallas_call(
        functools.partial(_hgrn_kernel, cs=cs, per_chunk_state=per_chunk_state, zero_init=s0 is None),
        grid=grid,
        in_specs=in_specs,
        out_specs=[pl.BlockSpec((CHUNK, W_C), lambda i, j: (i * steps + j, 0)), state_spec],
        out_shape=[jax.ShapeDtypeStruct((m, W_C), BF16),
                   jax.ShapeDtypeStruct((nseq, H_C, HEAD_DIM, HEAD_DIM), F32)],
        scratch_shapes=[pltpu.VMEM((H_C, HEAD_DIM, HEAD_DIM), F32)],
        compiler_params=_params("parallel", "arbitrary"),
        name="hgrn",
    )(*args)


def _out_proj_kernel(x_ref, oa_ref, ob_ref, oc_ref, w_ref, o_ref):
    wa = H_A * HEAD_DIM
    acc = _dot(oa_ref[...].astype(BF16), w_ref[:wa])
    acc += _dot(ob_ref[...], w_ref[wa:wa + W_B])
    acc += _dot(oc_ref[...], w_ref[wa + W_B:])
    o_ref[...] = x_ref[...] + acc


def _out_proj(x, oa, ob, oc, w, tm, tn):
    m, d = x.shape
    kdim = w.shape[0]
    return pl.pallas_call(
        _out_proj_kernel,
        grid=(m // tm, d // tn),
        in_specs=[pl.BlockSpec((tm, tn), lambda i, j: (i, j)),
                  pl.BlockSpec((tm, oa.shape[1]), lambda i, j: (i, 0)),
                  pl.BlockSpec((tm, ob.shape[1]), lambda i, j: (i, 0)),
                  pl.BlockSpec((tm, oc.shape[1]), lambda i, j: (i, 0)),
                  pl.BlockSpec((kdim, tn), lambda i, j: (0, j))],
        out_specs=pl.BlockSpec((tm, tn), lambda i, j: (i, j)),
        out_shape=jax.ShapeDtypeStruct((m, d), F32),
        compiler_params=_params("parallel", "arbitrary"),
        name="out_proj",
    )(x, oa, ob, oc, w)


def _ffn_kernel(x_ref, prev_ref, prevu_ref, g_ref, wg_ref, wu_ref, cwg_ref, cwu_ref, wd_ref, gf_ref,
                o_ref, csg_ref, csu_ref, xn_ref, pn_ref, acc_ref, *, seq_start_every, prev_is_state, final_norm):
    j = pl.program_id(1)
    nb, r, d = x_ref.shape
    rows = nb * r

    @pl.when(j == 0)
    def _():
        xn_ref[...] = _rms(x_ref[...].reshape(rows, d), g_ref[...]).astype(BF16)
        acc_ref[...] = jnp.zeros(acc_ref.shape, F32)
        if not prev_is_state:
            pn = _rms(prev_ref[...].reshape(SUBLANES, d), g_ref[...])
            pn_ref[...] = jnp.concatenate([pn, jnp.zeros_like(pn)], axis=0).astype(BF16)

    xn = xn_ref[...]
    tn = wg_ref.shape[1]
    if not prev_is_state:
        first = (pl.program_id(0) % seq_start_every) == 0
    ws = tn // FFN_SUB
    cols = [slice(k * ws, (k + 1) * ws) for k in range(FFN_SUB)]

    def up_proj(k):
        cs = cols[k]
        ug = _dot(xn, wg_ref[:, cs]).reshape(nb, r, ws)
        uu = _dot(xn, wu_ref[:, cs]).reshape(nb, r, ws)
        if prev_is_state:
            pg = prev_ref[:, :, cs]
            pu = prevu_ref[:, :, cs]
        else:
            pg = jnp.where(first, 0.0, _dot(pn_ref[...], wg_ref[:, cs])[:SUBLANES]).reshape(1, SUBLANES, ws)
            pu = jnp.where(first, 0.0, _dot(pn_ref[...], wu_ref[:, cs])[:SUBLANES]).reshape(1, SUBLANES, ws)
        return ug, uu, pg, pu

    nxt = up_proj(0)
    down = None
    for k in range(FFN_SUB):
        cs = cols[k]
        ug, uu, pg, pu = nxt
        if k + 1 < FFN_SUB:
            nxt = up_proj(k + 1)
        csg_ref[:, :, cs] = ug[:, r - SUBLANES:, :]
        csu_ref[:, :, cs] = uu[:, r - SUBLANES:, :]
        gate = _dwconv(ug, pg, cwg_ref[:, cs]).reshape(rows, ws)
        up = _dwconv(uu, pu, cwu_ref[:, cs]).reshape(rows, ws)
        hidden = (_silu(gate) * up).astype(BF16)
        part = _dot(hidden, wd_ref[cs, :])
        down = part if down is None else down + part
    acc_ref[...] += down

    @pl.when(j == pl.num_programs(1) - 1)
    def _():
        y = x_ref[...].reshape(rows, d) + acc_ref[...]
        if final_norm:
            y = _rms(y, gf_ref[...])
        o_ref[...] = y.reshape(nb, r, d)


def _ffn(x3, conv_prev, norm_g, w_up, conv_w, w_down, norm_final, nb, r, tn, final_norm):
    nseq, t, d = x3.shape
    nj = D_FF // tn
    steps_per_seq = t // r
    if conv_prev is None:
        assert nb == 1
        rb = r // SUBLANES
        prev_arr = x3
        prev_spec = pl.BlockSpec((1, SUBLANES, d), lambda i, j: (i // steps_per_seq,
                                                                 jnp.maximum((i % steps_per_seq) * rb - 1, 0), 0))
        prevu_spec = prev_spec
        x_map = lambda i, j: (i // steps_per_seq, i % steps_per_seq, 0)
        grid = (nseq * steps_per_seq, nj)
    else:
        assert r == t
        prev_arr = conv_prev
        prev_spec = pl.BlockSpec((nb, SUBLANES, tn), lambda i, j: (i, 0, j))
        prevu_spec = pl.BlockSpec((nb, SUBLANES, tn), lambda i, j: (i, 0, nj + j))
        x_map = lambda i, j: (i, 0, 0)
        grid = (nseq // nb, nj)
    rows = nb * r
    out, csg, csu = pl.pallas_call(
        functools.partial(_ffn_kernel, seq_start_every=steps_per_seq, prev_is_state=conv_prev is not None,
                          final_norm=final_norm),
        grid=grid,
        in_specs=[pl.BlockSpec((nb, r, d), x_map), prev_spec, prevu_spec,
                  pl.BlockSpec((1, d), lambda i, j: (0, 0)),
                  pl.BlockSpec((d, tn), lambda i, j: (0, j)),
                  pl.BlockSpec((d, tn), lambda i, j: (0, nj + j)),
                  pl.BlockSpec((FFN_CONV, tn), lambda i, j: (0, j)),
                  pl.BlockSpec((FFN_CONV, tn), lambda i, j: (0, nj + j)),
                  pl.BlockSpec((tn, d), lambda i, j: (j, 0)),
                  pl.BlockSpec((1, d), lambda i, j: (0, 0))],
        out_specs=[pl.BlockSpec((nb, r, d), x_map),
                   pl.BlockSpec((nb, SUBLANES, tn), lambda i, j: (i, 0, j)),
                   pl.BlockSpec((nb, SUBLANES, tn), lambda i, j: (i, 0, j))],
        out_shape=[jax.ShapeDtypeStruct((nseq, t, d), F32),
                   jax.ShapeDtypeStruct((grid[0] * nb, SUBLANES, D_FF), F32),
                   jax.ShapeDtypeStruct((grid[0] * nb, SUBLANES, D_FF), F32)],
        scratch_shapes=[pltpu.VMEM((rows, d), BF16), pltpu.VMEM((2 * SUBLANES, d), BF16),
                        pltpu.VMEM((rows, d), F32)],
        compiler_params=_params("arbitrary", "arbitrary"),
        name="conv_ffn",
    )(x3, prev_arr, prev_arr, norm_g.reshape(1, d), w_up, w_up, conv_w, conv_w, w_down, norm_final.reshape(1, d))
    csg = csg.reshape(nseq, -1, SUBLANES, D_FF)[:, -1]
    csu = csu.reshape(nseq, -1, SUBLANES, D_FF)[:, -1]
    return out, csg, csu


def _pack_w_in(w):
    d = w.shape[0]
    cuts = np.cumsum([Q_RANK, KV_RANK, ROPE_DIM, 3 * W_B, H_B, H_B, W_B, W_C, W_C, W_C, W_C])
    c_q, c_kv, k_pe, qkv, a_b, b_b, z_b, q_c, f_c, i_c, z_c = jnp.split(w, [int(c) for c in cuts[:-1]], axis=1)
    z = lambda n: jnp.zeros((d, n), w.dtype)
    misc = jnp.concatenate([c_kv, k_pe, z(MISC_AB - MISC_KPE - ROPE_DIM), a_b, b_b,
                            z(PROJ_TILE - MISC_AB - 2 * H_B)], axis=1)
    return jnp.concatenate([qkv, c_q, misc, z_b, q_c, f_c, i_c, z_c], axis=1).astype(BF16)


def _pack_w_uq(w):
    w3 = w.reshape(Q_RANK, H_A, NOPE_DIM + ROPE_DIM)
    return jnp.concatenate([w3[:, :, :NOPE_DIM].reshape(Q_RANK, -1),
                            w3[:, :, NOPE_DIM:].reshape(Q_RANK, -1)], axis=1).astype(BF16)


def _rope_tables(pos):
    half = ROPE_DIM // 2
    inv = ROPE_THETA ** (-jnp.arange(half, dtype=F32) / half)
    ang = pos.astype(F32)[:, None] * inv[None, :]
    cos, sin = jnp.cos(ang), jnp.sin(ang)
    reps = LANES // ROPE_DIM
    return (jnp.tile(jnp.concatenate([cos, cos], axis=1), (1, reps)),
            jnp.tile(jnp.concatenate([-sin, sin], axis=1), (1, reps)))


def _pad_rows_front(a, total):
    n, k, c = a.shape
    return jnp.concatenate([jnp.zeros((n, total - k, c), a.dtype), a], axis=1)


def _row_tile(m):
    for tm in (512, 256, 128, 64, 32, 16, 8):
        if m % tm == 0:
            return tm
    raise ValueError(m)


def _layer(x3, cos, sin, lw, lb, past, final_norm_g, is_last):
    nseq, t, d = x3.shape
    m = nseq * t
    tm = _row_tile(m)
    x = x3.reshape(m, d)
    proj = _norm_matmul(x, lw["norm_mix"], lw["w_in"], tm, 2 * PROJ_TILE)
    proj3 = proj.reshape(nseq, t, PROJ_WIDTH)

    qlat, qpe, ckv, kpe = _mla_prep(proj, cos, sin, lw["q_norm"], lw["kv_norm"], lw["w_uq"], lw["w_uk_t"],
                                    min(tm, 256), BF16 if past is None else F32)
    if past is None:
        o_a = _prompt_attn(qlat, qpe, ckv, kpe, lw["w_uv_t"], nseq, t, min(256, t))
        gdn_prev = hg_s0 = gdn_s0 = ffn_prev = None
        nb, r = 1, min(tm, t)
    else:
        n_pages = past["page_table"].shape[1]
        pps = 32 if n_pages % 32 == 0 else n_pages
        o_a = _decode_attn(qlat, qpe, ckv, kpe, past["cache_ckv"], past["cache_kpe_t"], past["page_table"],
                           lw["w_uv_t"], past["layer"], t, pps, 4)
        gdn_prev = _pad_rows_front(past["gdn_conv"], SUBLANES)
        gdn_s0, hg_s0 = past["gdn_S"], past["hgrn_S"]
        nb, r = min(nseq, 512 // t), t
    q, k, v, gb, gdn_cs = _gdn_prep(proj3, gdn_prev, lw["gdn_conv_w"], lw["gdn_a_log"], lw["gdn_dt_bias"], nb, r)
    o_b, gdn_s = _gdn(q, k, v, gb, proj, lw["gdn_norm"], gdn_s0, nseq, t)
    o_c, hg_s = _hgrn(proj, lb, lw["hg_norm"], hg_s0, nseq, t)
    x = _out_proj(x, o_a, o_b, o_c, lw["w_out"], tm, 1024)

    tn = 512
    ffn_prev = None if past is None else _pad_rows_front(past["ffn_conv"], SUBLANES)
    x3, csg, csu = _ffn(x.reshape(nseq, t, d), ffn_prev, lw["norm_ffn"], lw["w_up"], lw["ffn_conv_w"],
                        lw["w_down"], final_norm_g, nb, r, tn, is_last)
    ffn_cs = jnp.concatenate([csg, csu], axis=2)[:, SUBLANES - (FFN_CONV - 1):]
    new_state = (ckv.reshape(nseq, t, KV_RANK), kpe.reshape(nseq, t, ROPE_DIM), gdn_s,
                 gdn_cs[:, SUBLANES - (CONV_B - 1):], hg_s, ffn_cs)
    return x3, new_state


def kernel(x_prompt, x_sample, cache_ckv, cache_kpe, page_table, state_gdn_S, state_gdn_conv, state_hgrn_S, state_ffn_conv, norm_mix, w_in, mla_q_norm, mla_kv_norm, mla_w_uq, mla_w_uk, mla_w_uv, gdn_conv_w, gdn_a_log, gdn_dt_bias, gdn_norm, hgrn_lb, hgrn_norm, w_out, norm_ffn, ffn_w_up, ffn_conv_w, ffn_w_down, norm_final):
    depth = w_in.shape[0]
    bp, tp, _ = x_prompt.shape
    bs, ts, _ = x_sample.shape
    past_len = page_table.shape[1] * PAGE_SIZE
    cos_p, sin_p = _rope_tables(jnp.tile(jnp.arange(tp, dtype=jnp.int32), bp))
    cos_s, sin_s = _rope_tables(jnp.tile(past_len + jnp.arange(ts, dtype=jnp.int32), bs))
    p_lb = jax.nn.softmax(hgrn_lb.astype(F32), axis=0)
    lb_all = jnp.cumsum(p_lb, axis=0) - p_lb[0:1]
    cache_kpe_t = jnp.swapaxes(cache_kpe, 2, 3)

    xp, xs = x_prompt, x_sample
    new_p, new_s = [], []
    for l in range(depth):
        lw = dict(norm_mix=norm_mix[l], w_in=_pack_w_in(w_in[l]), q_norm=mla_q_norm[l], kv_norm=mla_kv_norm[l],
                  w_uq=_pack_w_uq(mla_w_uq[l]),
                  w_uk_t=jnp.transpose(mla_w_uk[l], (1, 2, 0)).astype(BF16),
                  w_uv_t=jnp.transpose(mla_w_uv[l], (1, 0, 2)).astype(BF16),
                  gdn_conv_w=gdn_conv_w[l], gdn_a_log=gdn_a_log[l], gdn_dt_bias=gdn_dt_bias[l],
                  gdn_norm=gdn_norm[l], hg_norm=hgrn_norm[l], w_out=w_out[l].astype(BF16),
                  norm_ffn=norm_ffn[l], w_up=ffn_w_up[l].astype(BF16), ffn_conv_w=ffn_conv_w[l],
                  w_down=ffn_w_down[l].astype(BF16))
        last = l == depth - 1
        xp, st_p = _layer(xp, cos_p, sin_p, lw, lb_all[l], None, norm_final, last)
        past = dict(cache_ckv=cache_ckv, cache_kpe_t=cache_kpe_t, page_table=page_table, layer=l,
                    gdn_S=state_gdn_S[l], gdn_conv=state_gdn_conv[l], hgrn_S=state_hgrn_S[l],
                    ffn_conv=state_ffn_conv[l])
        xs, st_s = _layer(xs, cos_s, sin_s, lw, lb_all[l], past, norm_final, last)
        new_p.append(st_p)
        new_s.append(st_s)
    outs_p = tuple(jnp.stack(a) for a in zip(*new_p))
    outs_s = tuple(jnp.stack(a) for a in zip(*new_s))
    return (xp, xs) + outs_p + outs_s
```

```python
import functools

import jax
import jax.numpy as jnp
import numpy as np
from jax import lax
from jax.experimental import pallas as pl
from jax.experimental.pallas import tpu as pltpu

F32 = jnp.float32
BF16 = jnp.bfloat16

D_MODEL = 2048
PAGE_SIZE = 128
HEAD_DIM = 128
H_A = 8
H_B = 4
H_C = 4
Q_RANK = 512
KV_RANK = 256
NOPE_DIM = 128
ROPE_DIM = 64
W_B = H_B * HEAD_DIM
W_C = H_C * HEAD_DIM
CONV_B = 4
D_FF = 5632
FFN_CONV = 3
ROPE_THETA = 10000.0
EPS = 1e-6
NEG = -1e30
LB_FLOOR = 1e-30

PROJ_TILE = 512
COL_QKV = 0
COL_CQ = 3
COL_MISC = 4
COL_ZB = 5
COL_QC = 6
COL_FC = 7
COL_IC = 8
COL_ZC = 9
PROJ_WIDTH = 10 * PROJ_TILE
MISC_KPE = 256
MISC_AB = 384

SUBLANES = 8
LANES = 128
CHUNK = 128
HG_CHUNK = 16
FFN_SUB = 1
VMEM_LIMIT = 56 * 1024 * 1024


def _params(*sem):
    return pltpu.CompilerParams(dimension_semantics=sem, vmem_limit_bytes=VMEM_LIMIT)


def _dot(a, b):
    return jnp.dot(a, b, preferred_element_type=F32)


def _dot_nt(a, b):
    return lax.dot_general(a, b, (((1,), (1,)), ((), ())), preferred_element_type=F32)


def _bdot(a, b):
    return _dot(a.astype(BF16), b.astype(BF16))


def _split2(a):
    hi = a.astype(BF16)
    lo = (a - hi.astype(F32)).astype(BF16)
    return hi, lo


def _dot3(a, b):
    ah, al = _split2(a)
    bh, bl = _split2(b)
    return _dot(ah, bh) + (_dot(ah, bl) + _dot(al, bh))


def _dot_exact_lhs(a01, b):
    a = a01.astype(BF16)
    b1 = b.astype(BF16)
    r1 = b - b1.astype(F32)
    b2 = r1.astype(BF16)
    b3 = (r1 - b2.astype(F32)).astype(BF16)
    return _dot(a, b1) + (_dot(a, b2) + _dot(a, b3))


def _sigmoid(x):
    return 1.0 / (1.0 + jnp.exp(-x))


def _silu(x):
    return x * _sigmoid(x)


def _softplus(x):
    return jnp.maximum(x, 0.0) + jnp.log1p(jnp.exp(-jnp.abs(x)))


def _rms(x, g):
    ms = jnp.mean(x * x, axis=-1, keepdims=True)
    return x * lax.rsqrt(ms + EPS) * g


def _swap_halves(x, lane):
    w = x.shape[-1]
    half = ROPE_DIM // 2
    fwd = pltpu.roll(x, w - half, axis=x.ndim - 1)
    bwd = pltpu.roll(x, half, axis=x.ndim - 1)
    return jnp.where((lane % ROPE_DIM) < half, fwd, bwd)


def _rope(x, cos, sin_signed):
    lane = lax.broadcasted_iota(jnp.int32, x.shape, x.ndim - 1)
    return x * cos + _swap_halves(x, lane) * sin_signed


def _norm_matmul_kernel(x_ref, g_ref, w_ref, o_ref, xn_ref):
    @pl.when(pl.program_id(1) == 0)
    def _():
        xn_ref[...] = _rms(x_ref[...], g_ref[...]).astype(BF16)

    o_ref[...] = _dot(xn_ref[...], w_ref[...])


def _norm_matmul(x, g, w, tm, tn):
    m, k = x.shape
    n = w.shape[1]
    return pl.pallas_call(
        _norm_matmul_kernel,
        grid=(m // tm, n // tn),
        in_specs=[pl.BlockSpec((tm, k), lambda i, j: (i, 0)),
                  pl.BlockSpec((1, k), lambda i, j: (0, 0)),
                  pl.BlockSpec((k, tn), lambda i, j: (0, j))],
        out_specs=pl.BlockSpec((tm, tn), lambda i, j: (i, j)),
        out_shape=jax.ShapeDtypeStruct((m, n), F32),
        scratch_shapes=[pltpu.VMEM((tm, k), BF16)],
        compiler_params=_params("parallel", "arbitrary"),
        name="in_proj",
    )(x, g.reshape(1, k), w)


def _mla_prep_kernel(cq_ref, misc_ref, cos_ref, sin_ref, qn_ref, kvn_ref, wuq_ref, wuk_ref,
                     qlat_ref, qpe_ref, ckv_ref, kpe_ref, *, scale):
    cos = cos_ref[...]
    sin = sin_ref[...]
    misc = misc_ref[...]
    ckv_ref[...] = _rms(misc[:, :KV_RANK], kvn_ref[...])
    kpe = _rope(misc[:, MISC_KPE:MISC_KPE + LANES], cos, sin)
    kpe_ref[...] = kpe[:, :ROPE_DIM]

    cqn = _rms(cq_ref[...], qn_ref[...]).astype(BF16)
    q = _dot(cqn, wuq_ref[...])
    n_nope = H_A * NOPE_DIM
    reps = (H_A * ROPE_DIM) // LANES
    q_pe = _rope(q[:, n_nope:], jnp.tile(cos, (1, reps)), jnp.tile(sin, (1, reps))) * scale
    for h in range(H_A):
        qn = q[:, h * NOPE_DIM:(h + 1) * NOPE_DIM].astype(BF16)
        qlat_ref[h] = (_dot(qn, wuk_ref[h]) * scale).astype(qlat_ref.dtype)
        qpe_ref[h] = q_pe[:, h * ROPE_DIM:(h + 1) * ROPE_DIM].astype(qpe_ref.dtype)


def _mla_prep(proj, cos, sin, q_norm, kv_norm, w_uq, w_uk_t, tm, q_dtype):
    m = proj.shape[0]
    scale = float((NOPE_DIM + ROPE_DIM) ** -0.5)
    return pl.pallas_call(
        functools.partial(_mla_prep_kernel, scale=scale),
        grid=(m // tm,),
        in_specs=[pl.BlockSpec((tm, PROJ_TILE), lambda i: (i, COL_CQ)),
                  pl.BlockSpec((tm, PROJ_TILE), lambda i: (i, COL_MISC)),
                  pl.BlockSpec((tm, LANES), lambda i: (i, 0)),
                  pl.BlockSpec((tm, LANES), lambda i: (i, 0)),
                  pl.BlockSpec((1, Q_RANK), lambda i: (0, 0)),
                  pl.BlockSpec((1, KV_RANK), lambda i: (0, 0)),
                  pl.BlockSpec(w_uq.shape, lambda i: (0, 0)),
                  pl.BlockSpec(w_uk_t.shape, lambda i: (0, 0, 0))],
        out_specs=[pl.BlockSpec((H_A, tm, KV_RANK), lambda i: (0, i, 0)),
                   pl.BlockSpec((H_A, tm, ROPE_DIM), lambda i: (0, i, 0)),
                   pl.BlockSpec((tm, KV_RANK), lambda i: (i, 0)),
                   pl.BlockSpec((tm, ROPE_DIM), lambda i: (i, 0))],
        out_shape=[jax.ShapeDtypeStruct((H_A, m, KV_RANK), q_dtype),
                   jax.ShapeDtypeStruct((H_A, m, ROPE_DIM), q_dtype),
                   jax.ShapeDtypeStruct((m, KV_RANK), F32),
                   jax.ShapeDtypeStruct((m, ROPE_DIM), F32)],
        compiler_params=_params("parallel"),
        name="mla_prep",
    )(proj, proj, cos, sin, q_norm.reshape(1, -1), kv_norm.reshape(1, -1), w_uq, w_uk_t)


def _lanes(x, n):
    return x if n == LANES else jnp.concatenate([x] * (n // LANES), axis=1)


def _prompt_attn_kernel(qi_tab, ki_tab, qlat_ref, qpe_ref, ckv_ref, kpe_ref, wuv_ref, o_ref,
                        m_ref, l_ref, acc_ref):
    step = pl.program_id(1)
    qi = qi_tab[step]
    ki = ki_tab[step]
    tq, tk = qlat_ref.shape[1], ckv_ref.shape[0]

    @pl.when(ki == 0)
    def _():
        m_ref[...] = jnp.full(m_ref.shape, NEG, F32)
        l_ref[...] = jnp.zeros(l_ref.shape, F32)
        acc_ref[...] = jnp.zeros(acc_ref.shape, F32)

    def update(masked):
        kc = ckv_ref[...].astype(BF16)
        kp = kpe_ref[...].astype(BF16)
        if masked:
            keep = (lax.broadcasted_iota(jnp.int32, (tq, tk), 1)
                    <= lax.broadcasted_iota(jnp.int32, (tq, tk), 0))
        scores = lambda h: _dot_nt(qlat_ref[h], kc) + _dot_nt(qpe_ref[h], kp)
        s_next = scores(0)
        for h in range(H_A):
            s = s_next
            if h + 1 < H_A:
                s_next = scores(h + 1)
            if masked:
                s = jnp.where(keep, s, NEG)
            m_prev = m_ref[h]
            m_new = jnp.maximum(m_prev, jnp.max(s, axis=-1, keepdims=True))
            alpha = jnp.exp(m_prev - m_new)
            p = jnp.exp(s - _lanes(m_new, tk))
            l_ref[h] = alpha * l_ref[h] + jnp.sum(p, axis=-1, keepdims=True)
            acc_ref[h] = _lanes(alpha, KV_RANK) * acc_ref[h] + _dot(p.astype(BF16), kc)
            m_ref[h] = m_new

    @pl.when(ki < qi)
    def _():
        update(False)

    @pl.when(ki == qi)
    def _():
        update(True)
        for h in range(H_A):
            o = (acc_ref[h] * _lanes(1.0 / l_ref[h], KV_RANK)).astype(BF16)
            o_ref[:, h * HEAD_DIM:(h + 1) * HEAD_DIM] = _dot(o, wuv_ref[h]).astype(o_ref.dtype)


def _prompt_attn(qlat, qpe, ckv, kpe, w_uv_t, nseq, t, tq):
    nq = t // tq
    pairs = [(qi, ki) for qi in range(nq) for ki in range(qi + 1)]
    qi_tab = jnp.asarray([p[0] for p in pairs], jnp.int32)
    ki_tab = jnp.asarray([p[1] for p in pairs], jnp.int32)
    q_map = lambda b, s, qt, kt: (0, b * nq + qt[s], 0)
    kv_map = lambda b, s, qt, kt: (b * nq + kt[s], 0)
    grid_spec = pltpu.PrefetchScalarGridSpec(
        num_scalar_prefetch=2,
        grid=(nseq, len(pairs)),
        in_specs=[pl.BlockSpec((H_A, tq, KV_RANK), q_map),
                  pl.BlockSpec((H_A, tq, ROPE_DIM), q_map),
                  pl.BlockSpec((tq, KV_RANK), kv_map),
                  pl.BlockSpec((tq, ROPE_DIM), kv_map),
                  pl.BlockSpec(w_uv_t.shape, lambda b, s, qt, kt: (0, 0, 0))],
        out_specs=pl.BlockSpec((tq, H_A * HEAD_DIM), lambda b, s, qt, kt: (b * nq + qt[s], 0)),
        scratch_shapes=[pltpu.VMEM((H_A, tq, LANES), F32), pltpu.VMEM((H_A, tq, LANES), F32),
                        pltpu.VMEM((H_A, tq, KV_RANK), F32)],
    )
    return pl.pallas_call(
        _prompt_attn_kernel,
        grid_spec=grid_spec,
        out_shape=jax.ShapeDtypeStruct((nseq * t, H_A * HEAD_DIM), BF16),
        compiler_params=_params("parallel", "arbitrary"),
        name="prompt_attn",
    )(qi_tab, ki_tab, qlat, qpe, ckv, kpe, w_uv_t)


def _decode_attn_kernel(pt_ref, qlat_ref, qpe_ref, nckv_ref, nkpe_ref, wuv_ref, ckv_hbm, kpe_t_hbm,
                        o_ref, kbuf, pbuf, sem, m_ref, l_ref, acc_ref, *, layer, pps, ts, n_chains):
    b = pl.program_id(0)
    j = pl.program_id(1)
    nj = pl.num_programs(1)
    step = b * nj + j
    last_step = pl.num_programs(0) * nj - 1
    slot = step % 2
    rows = H_A * ts

    def page_copies(bb, jj, sl, lookup=True):
        out = []
        for i in range(pps):
            page = pt_ref[bb, jj * pps + i] if lookup else 0
            out.append(pltpu.make_async_copy(ckv_hbm.at[layer, page], kbuf.at[sl, i], sem.at[sl, 0]))
            out.append(pltpu.make_async_copy(kpe_t_hbm.at[layer, page], pbuf.at[sl, i], sem.at[sl, 1]))
        return out

    @pl.when(step == 0)
    def _():
        for cp in page_copies(b, j, slot):
            cp.start()

    for cp in page_copies(b, j, slot, lookup=False):
        cp.wait()
    nxt = jnp.minimum(step + 1, last_step)
    prefetch = page_copies(nxt // nj, nxt % nj, 1 - slot)

    ql = qlat_ref[...].reshape(rows, KV_RANK).astype(BF16)
    qp = qpe_ref[...].reshape(rows, ROPE_DIM).astype(BF16)

    @pl.when(j == 0)
    def _():
        kc = nckv_ref[...].astype(BF16)
        kp = nkpe_ref[...].astype(BF16)
        s = _dot_nt(ql, kc) + _dot_nt(qp, kp)
        nb = kc.shape[0] // ts
        t_row = lax.broadcasted_iota(jnp.int32, s.shape, 0) % ts
        col = lax.broadcasted_iota(jnp.int32, s.shape, 1)
        ok = (col // ts == b % nb) & (col % ts <= t_row)
        s = jnp.where(ok, s, NEG)
        m = jnp.max(s, axis=-1, keepdims=True)
        p = jnp.where(ok, jnp.exp(s - m), 0.0)
        m_ref[0] = m
        l_ref[0] = jnp.sum(p, axis=-1, keepdims=True)
        acc_ref[0] = _dot(p.astype(BF16), kc)
        for c in range(1, n_chains):
            m_ref[c] = jnp.full((rows, 1), NEG, F32)
            l_ref[c] = jnp.zeros((rows, 1), F32)
            acc_ref[c] = jnp.zeros((rows, KV_RANK), F32)

    per = pps // n_chains
    def chain_scores(c):
        kcs, scores = [], []
        for i in range(c * per, (c + 1) * per):
            prefetch[2 * i].start()
            prefetch[2 * i + 1].start(priority=1)
            kcs.append(kbuf[slot, i].astype(BF16))
            scores.append(_dot_nt(ql, kcs[-1]) + _dot(qp, pbuf[slot, i].astype(BF16)))
        return kcs, jnp.concatenate(scores, axis=1)

    nxt_chain = chain_scores(0)
    for c in range(n_chains):
        kcs, s = nxt_chain
        if c + 1 < n_chains:
            nxt_chain = chain_scores(c + 1)
        m_prev = m_ref[c]
        m_new = jnp.maximum(m_prev, jnp.max(s, axis=-1, keepdims=True))
        alpha = jnp.exp(m_prev - m_new)
        p = jnp.exp(s - m_new).astype(BF16)
        l_ref[c] = alpha * l_ref[c] + jnp.sum(p.astype(F32), axis=-1, keepdims=True)
        pv = _dot(p[:, :PAGE_SIZE], kcs[0])
        for i in range(1, per):
            pv += _dot(p[:, i * PAGE_SIZE:(i + 1) * PAGE_SIZE], kcs[i])
        acc_ref[c] = alpha * acc_ref[c] + pv
        m_ref[c] = m_new

    @pl.when(step == last_step)
    def _():
        for cp in page_copies(b, j, 1 - slot, lookup=False):
            cp.wait()

    @pl.when(j == nj - 1)
    def _():
        m = m_ref[0]
        for c in range(1, n_chains):
            m = jnp.maximum(m, m_ref[c])
        l = jnp.zeros((rows, 1), F32)
        acc = jnp.zeros((rows, KV_RANK), F32)
        for c in range(n_chains):
            w = jnp.exp(m_ref[c] - m)
            l += w * l_ref[c]
            acc += w * acc_ref[c]
        o = acc * (1.0 / l)
        for h in range(H_A):
            oh = o[h * ts:(h + 1) * ts].astype(BF16)
            o_ref[:, h * HEAD_DIM:(h + 1) * HEAD_DIM] = _dot(oh, wuv_ref[h]).astype(o_ref.dtype)


def _decode_attn(qlat, qpe, ckv_new, kpe_new, cache_ckv, cache_kpe_t, page_table, w_uv_t, layer, ts, pps, n_chains):
    nseq, n_pages = page_table.shape
    rows = H_A * ts
    new_rows = min(LANES, nseq * ts)
    nb = new_rows // ts

    grid_spec = pltpu.PrefetchScalarGridSpec(
        num_scalar_prefetch=1,
        grid=(nseq, n_pages // pps),
        in_specs=[pl.BlockSpec((H_A, ts, KV_RANK), lambda b, j, pt: (0, b, 0)),
                  pl.BlockSpec((H_A, ts, ROPE_DIM), lambda b, j, pt: (0, b, 0)),
                  pl.BlockSpec((new_rows, KV_RANK), lambda b, j, pt: (b // nb, 0)),
                  pl.BlockSpec((new_rows, ROPE_DIM), lambda b, j, pt: (b // nb, 0)),
                  pl.BlockSpec(w_uv_t.shape, lambda b, j, pt: (0, 0, 0)),
                  pl.BlockSpec(memory_space=pl.ANY),
                  pl.BlockSpec(memory_space=pl.ANY)],
        out_specs=pl.BlockSpec((ts, H_A * HEAD_DIM), lambda b, j, pt: (b, 0)),
        scratch_shapes=[pltpu.VMEM((2, pps, PAGE_SIZE, KV_RANK), F32),
                        pltpu.VMEM((2, pps, ROPE_DIM, PAGE_SIZE), F32),
                        pltpu.SemaphoreType.DMA((2, 2)),
                        pltpu.VMEM((n_chains, rows, 1), F32), pltpu.VMEM((n_chains, rows, 1), F32),
                        pltpu.VMEM((n_chains, rows, KV_RANK), F32)],
    )
    return pl.pallas_call(
        functools.partial(_decode_attn_kernel, layer=layer, pps=pps, ts=ts, n_chains=n_chains),
        grid_spec=grid_spec,
        out_shape=jax.ShapeDtypeStruct((nseq * ts, H_A * HEAD_DIM), F32),
        compiler_params=_params("arbitrary", "arbitrary"),
        name="decode_attn",
    )(page_table, qlat, qpe, ckv_new, kpe_new, w_uv_t, cache_ckv, cache_kpe_t)


def _shifted(x, prev, j):
    if j == 0:
        return x
    rx = pltpu.roll(x, j, axis=1)
    rp = pltpu.roll(prev, j, axis=1)
    row = lax.broadcasted_iota(jnp.int32, prev.shape, 1)
    head = jnp.where(row < j, rp, rx[:, :SUBLANES])
    if x.shape[1] == SUBLANES:
        return head
    return jnp.concatenate([head, rx[:, SUBLANES:]], axis=1)


def _dwconv(x, prev, w):
    k = w.shape[0]
    y = x * w[k - 1:k]
    for j in range(k - 1):
        y = y + _shifted(x, prev, k - 1 - j) * w[j:j + 1]
    return y


def _gdn_prep_kernel(x_ref, prev_ref, misc_ref, cw_ref, alog_ref, dtb_ref,
                     q_ref, k_ref, v_ref, gb_ref, cs_ref, *, seq_start_every):
    x = x_ref[...]
    nb, r, c = x.shape
    prev = prev_ref[...]
    if seq_start_every:
        first = (pl.program_id(0) % seq_start_every) == 0
        prev = jnp.where(first, 0.0, prev)
    y = _silu(_dwconv(x, prev, cw_ref[...])).reshape(nb * r, c)
    cs_ref[...] = x[:, r - SUBLANES:, :]
    for h in range(H_B):
        sl = slice(h * HEAD_DIM, (h + 1) * HEAD_DIM)
        qh = y[:, h * HEAD_DIM:(h + 1) * HEAD_DIM]
        kh = y[:, W_B + h * HEAD_DIM:W_B + (h + 1) * HEAD_DIM]
        q_ref[:, sl] = qh * lax.rsqrt(jnp.sum(qh * qh, axis=-1, keepdims=True) + EPS) * (HEAD_DIM ** -0.5)
        k_ref[:, sl] = kh * lax.rsqrt(jnp.sum(kh * kh, axis=-1, keepdims=True) + EPS)
    v_ref[...] = y[:, 2 * W_B:]
    ab = misc_ref[...].reshape(nb * r, PROJ_TILE)[:, MISC_AB:MISC_AB + LANES]
    g = -jnp.exp(alog_ref[...]) * _softplus(ab + dtb_ref[...])
    lane = lax.broadcasted_iota(jnp.int32, ab.shape, 1)
    gb_ref[...] = jnp.where(lane < H_B, g, _sigmoid(ab))


def _gdn_prep(proj3, conv_prev, conv_w, a_log, dt_bias, nb, r):
    nseq, t, _ = proj3.shape
    c = 3 * W_B
    steps_per_seq = t // r
    m = nseq * t
    rows = nb * r
    if conv_prev is None:
        assert nb == 1
        rb = r // SUBLANES
        prev_arr = proj3
        prev_spec = pl.BlockSpec((1, SUBLANES, c), lambda i: (i // steps_per_seq,
                                                               jnp.maximum((i % steps_per_seq) * rb - 1, 0), COL_QKV))
        x_map = lambda i: (i // steps_per_seq, i % steps_per_seq, COL_QKV)
        misc_map = lambda i: (i // steps_per_seq, i % steps_per_seq, COL_MISC)
        grid = (nseq * steps_per_seq,)
        seq_start_every = steps_per_seq
    else:
        assert r == t
        prev_arr = conv_prev
        prev_spec = pl.BlockSpec((nb, SUBLANES, c), lambda i: (i, 0, 0))
        x_map = lambda i: (i, 0, COL_QKV)
        misc_map = lambda i: (i, 0, COL_MISC)
        grid = (nseq // nb,)
        seq_start_every = 0
    alog = jnp.zeros((1, LANES), F32).at[0, :H_B].set(a_log.astype(F32))
    dtb = jnp.zeros((1, LANES), F32).at[0, :H_B].set(dt_bias.astype(F32))
    row_spec = lambda w: pl.BlockSpec((rows, w), lambda i: (i, 0))
    q, k, v, gb, cs = pl.pallas_call(
        functools.partial(_gdn_prep_kernel, seq_start_every=seq_start_every),
        grid=grid,
        in_specs=[pl.BlockSpec((nb, r, c), x_map), prev_spec,
                  pl.BlockSpec((nb, r, PROJ_TILE), misc_map),
                  pl.BlockSpec((CONV_B, c), lambda i: (0, 0)),
                  pl.BlockSpec((1, LANES), lambda i: (0, 0)),
                  pl.BlockSpec((1, LANES), lambda i: (0, 0))],
        out_specs=[row_spec(W_B), row_spec(W_B), row_spec(W_B), row_spec(LANES),
                   pl.BlockSpec((nb, SUBLANES, c), lambda i: (i, 0, 0))],
        out_shape=[jax.ShapeDtypeStruct((m, W_B), F32), jax.ShapeDtypeStruct((m, W_B), F32),
                   jax.ShapeDtypeStruct((m, W_B), F32), jax.ShapeDtypeStruct((m, LANES), F32),
                   jax.ShapeDtypeStruct((grid[0] * nb, SUBLANES, c), F32)],
        compiler_params=_params("arbitrary"),
        name="gdn_prep",
    )(proj3, prev_arr, proj3, conv_w, alog, dtb)
    cs = cs.reshape(nseq, -1, SUBLANES, c)[:, -1]
    return q, k, v, gb, cs


def _gdn_kernel(q_ref, k_ref, v_ref, gb_ref, z_ref, gn_ref, *rest, nseq_blk, r, zero_init):
    if zero_init:
        o_ref, sout_ref, s_scr, wS_scr, qS_scr, vn_scr = rest
    else:
        s0_ref, o_ref, sout_ref, s_scr, wS_scr, qS_scr, vn_scr = rest
    c = nseq_blk * r
    ci = pl.program_id(1)

    @pl.when(ci == 0)
    def _():
        if zero_init:
            s_scr[...] = jnp.zeros(s_scr.shape, F32)
        else:
            s_scr[...] = s0_ref[...]

    row = lax.broadcasted_iota(jnp.int32, (c, c), 0)
    col = lax.broadcasted_iota(jnp.int32, (c, c), 1)
    same = (row // r) == (col // r)
    causal = same & (col <= row)
    strict = same & (col < row)
    eye = (row == col).astype(F32)

    gb = gb_ref[...]
    gcs = _dot_exact_lhs(causal.astype(F32), gb)
    gtot = _dot_exact_lhs(same.astype(F32), gb)
    gcs_t = gcs.T
    n_sq = max(int(np.ceil(np.log2(r))) - 1, 0)

    heads = range(H_B)
    sls = [slice(h * HEAD_DIM, (h + 1) * HEAD_DIM) for h in heads]
    qs = [q_ref[:, sl] for sl in sls]
    ks = [k_ref[:, sl] for sl in sls]
    vs = [v_ref[:, sl] for sl in sls]
    gcl = [gcs[:, h:h + 1] for h in heads]
    gll = [gtot[:, h:h + 1] for h in heads]
    betas = [gb[:, H_B + h:H_B + h + 1] for h in heads]
    decays = [jnp.exp(jnp.where(causal, gcl[h] - gcs_t[h:h + 1, :], NEG)) for h in heads]
    kbs = [ks[h] * betas[h] for h in heads]
    mms = [jnp.where(strict, _dot_nt(kbs[h].astype(BF16), ks[h].astype(BF16)) * decays[h], 0.0) for h in heads]
    tinvs = [eye - mm for mm in mms]
    pws = list(mms)
    for _ in range(n_sq):
        pws = [_dot3(pw, pw) for pw in pws]
        tinvs = [tinv + _dot3(tinv, pw) for tinv, pw in zip(tinvs, pws)]
    egcs = [jnp.exp(gc) for gc in gcl]
    us = [_dot3(tinvs[h], vs[h] * betas[h]) for h in heads]
    ws = [_dot3(tinvs[h], kbs[h] * egcs[h]).astype(BF16) for h in heads]
    qks = [(_dot_nt(qs[h].astype(BF16), ks[h].astype(BF16)) * decays[h]).astype(BF16) for h in heads]
    q_decs = [(qs[h] * egcs[h]).astype(BF16) for h in heads]
    k_dec_ts = [(ks[h] * jnp.exp(gll[h] - gcl[h])).T for h in heads]

    if nseq_blk == 1:
        sbs = [s_scr[0, h].astype(BF16) for h in heads]
        v_news = [us[h] - _dot(ws[h], sbs[h]) for h in heads]
        vnb = [v.astype(BF16) for v in v_news]
        os_ = [_dot(q_decs[h], sbs[h]) + _dot(qks[h], vnb[h]) for h in heads]
        for h in heads:
            s_scr[0, h] = (s_scr[0, h] * jnp.exp(gll[h][c - 1:c, :])
                           + _dot(k_dec_ts[h].astype(BF16), vnb[h]))
        for h in heads:
            on = _rms(os_[h], gn_ref[...])
            o_ref[:, sls[h]] = (on * _silu(z_ref[:, sls[h]])).astype(o_ref.dtype)

    else:
        rowc = lax.broadcasted_iota(jnp.int32, (c, HEAD_DIM), 0) // r
        colc = lax.broadcasted_iota(jnp.int32, (HEAD_DIM, c), 1) // r
        wS_scr[...] = jnp.zeros(wS_scr.shape, F32)
        qS_scr[...] = jnp.zeros(qS_scr.shape, F32)

        def read_state(p, _):
            mine = rowc == p
            for h in heads:
                sb = s_scr[p, h].astype(BF16)
                wS_scr[h] += _dot(jnp.where(mine, ws[h], jnp.zeros_like(ws[h])), sb)
                qS_scr[h] += _dot(jnp.where(mine, q_decs[h], jnp.zeros_like(q_decs[h])), sb)
            return 0

        lax.fori_loop(0, nseq_blk, read_state, 0, unroll=2)
        for h in heads:
            vn_scr[h] = (us[h] - wS_scr[h]).astype(BF16)
            on = _rms(qS_scr[h] + _dot(qks[h], vn_scr[h]), gn_ref[...])
            o_ref[:, sls[h]] = (on * _silu(z_ref[:, sls[h]])).astype(o_ref.dtype)
        e_gls = [jnp.exp(gl) for gl in gll]

        def write_state(p, _):
            for h in heads:
                kd = jnp.where(colc == p, k_dec_ts[h], 0.0).astype(BF16)
                g_last = jnp.max(jnp.where(rowc[:, :1] == p, e_gls[h], 0.0), axis=0, keepdims=True)
                s_scr[p, h] = s_scr[p, h] * g_last + _dot(kd, vn_scr[h])
            return 0

        lax.fori_loop(0, nseq_blk, write_state, 0, unroll=2)

    @pl.when(ci == pl.num_programs(1) - 1)
    def _():
        sout_ref[...] = s_scr[...]


def _gdn(q, k, v, gb, proj, gdn_norm, s0, nseq, t):
    m = nseq * t
    if t >= CHUNK:
        nseq_blk, r = 1, CHUNK
    else:
        nseq_blk, r = CHUNK // t, t
    n_chunks = t // r
    n_blocks = nseq // nseq_blk
    row_map = lambda i, ci: (i * n_chunks + ci, 0)
    state_spec = pl.BlockSpec((nseq_blk, H_B, HEAD_DIM, HEAD_DIM), lambda i, ci: (i, 0, 0, 0))
    in_specs = [pl.BlockSpec((CHUNK, W_B), row_map), pl.BlockSpec((CHUNK, W_B), row_map),
                pl.BlockSpec((CHUNK, W_B), row_map), pl.BlockSpec((CHUNK, LANES), row_map),
                pl.BlockSpec((CHUNK, W_B), lambda i, ci: (i * n_chunks + ci, COL_ZB)),
                pl.BlockSpec((1, HEAD_DIM), lambda i, ci: (0, 0))]
    args = [q, k, v, gb, proj, gdn_norm.reshape(1, HEAD_DIM)]
    if s0 is not None:
        s0_all, layer = s0
        in_specs.append(pl.BlockSpec((None, nseq_blk, H_B, HEAD_DIM, HEAD_DIM), lambda i, ci: (layer, i, 0, 0, 0)))
        args.append(s0_all)
    return pl.pallas_call(
        functools.partial(_gdn_kernel, nseq_blk=nseq_blk, r=r, zero_init=s0 is None),
        grid=(n_blocks, n_chunks),
        in_specs=in_specs,
        out_specs=[pl.BlockSpec((CHUNK, W_B), row_map), state_spec],
        out_shape=[jax.ShapeDtypeStruct((m, W_B), BF16),
                   jax.ShapeDtypeStruct((nseq, H_B, HEAD_DIM, HEAD_DIM), F32)],
        scratch_shapes=[pltpu.VMEM((nseq_blk, H_B, HEAD_DIM, HEAD_DIM), F32),
                        pltpu.VMEM((H_B, CHUNK, HEAD_DIM), F32), pltpu.VMEM((H_B, CHUNK, HEAD_DIM), F32),
                        pltpu.VMEM((H_B, CHUNK, HEAD_DIM), BF16)],
        compiler_params=_params("parallel", "arbitrary"),
        name="gdn",
    )(*args)


def _hgrn_kernel(qc_ref, fc_ref, ic_ref, zc_ref, lb_ref, hn_ref, *rest, cs, per_chunk_state, zero_init):
    if zero_init:
        o_ref, sout_ref, st_scr = rest
    else:
        s0_ref, o_ref, sout_ref, st_scr = rest
    c = qc_ref.shape[0]
    n_ch = c // cs
    step = pl.program_id(1)

    if not per_chunk_state:
        @pl.when(step == 0)
        def _():
            st_scr[...] = jnp.zeros(st_scr.shape, F32)

    lb = lb_ref[...]
    fr = fc_ref[...]
    log_lb = jnp.log(jnp.maximum(lb, LB_FLOOR))
    a = log_lb
    b = jnp.log1p(-lb) - _softplus(-fr)
    mx = jnp.maximum(a, b)
    logf = mx + jnp.log1p(jnp.exp(-jnp.abs(a - b)))
    kk = (1.0 - lb) * _sigmoid(-fr)
    qq = _silu(qc_ref[...])

    row = lax.broadcasted_iota(jnp.int32, (c, c), 0)
    col = lax.broadcasted_iota(jnp.int32, (c, c), 1)
    same = (row // cs) == (col // cs)
    bc_all = _dot_exact_lhs((same & (col <= row)).astype(F32), logf)
    btot_all = _dot_exact_lhs(same.astype(F32), logf)
    t_idx = lax.broadcasted_iota(jnp.int32, (n_ch, cs, HEAD_DIM), 1)
    rowc = lax.broadcasted_iota(jnp.int32, (c, HEAD_DIM), 0) // cs

    def head(h):
        sl = slice(h * HEAD_DIM, (h + 1) * HEAD_DIM)
        return qq[:, sl], kk[:, sl], ic_ref[:, sl], bc_all[:, sl], btot_all[:, sl]

    def within_chunks(h):
        q, k, v, bc, _ = head(h)
        q3, k3, v3, bc3 = (x.reshape(n_ch, cs, HEAD_DIM) for x in (q, k, v, bc))
        o3 = jnp.zeros((n_ch, cs, HEAD_DIM), F32)
        for s in range(cs):
            dec = jnp.exp(jnp.where(t_idx >= s, bc3 - bc3[:, s:s + 1, :], NEG))
            a_col = jnp.sum(q3 * k3[:, s:s + 1, :] * dec, axis=-1, keepdims=True)
            o3 = o3 + a_col * v3[:, s:s + 1, :]
        return o3.reshape(c, HEAD_DIM)

    def across_chunks(h):
        q, k, v, bc, btot = head(h)
        q_dec = (q * jnp.exp(bc)).astype(BF16)
        k_dec = (k * jnp.exp(btot - bc)).astype(BF16)
        v_t = v.T.astype(BF16)
        e_tot = jnp.exp(btot)
        if per_chunk_state:
            def chunk_step(n, o_acc):
                mine = rowc == n
                st = s0_ref[n, h].T
                o_acc = o_acc + _dot_nt(jnp.where(mine, q_dec, jnp.zeros_like(q_dec)), st.astype(BF16))
                dec_n = jnp.max(jnp.where(mine, e_tot, 0.0), axis=0, keepdims=True)
                st = st * dec_n + _dot(v_t, jnp.where(mine, k_dec, jnp.zeros_like(k_dec)))
                sout_ref[n, h] = st.T
                return o_acc

            return lax.fori_loop(0, n_ch, chunk_step, jnp.zeros((c, HEAD_DIM), F32), unroll=4)
        st = st_scr[h]
        pieces = []
        for n in range(n_ch):
            rs = slice(n * cs, (n + 1) * cs)
            pieces.append(_dot_nt(q_dec[rs], st.astype(BF16)))
            kv = _dot(v_t, jnp.where(rowc == n, k_dec, jnp.zeros_like(k_dec)))
            st = st * e_tot[n * cs:n * cs + 1] + kv
        st_scr[h] = st
        return jnp.concatenate(pieces, axis=0)

    o_state = across_chunks(0)
    for h in range(H_C):
        sl = slice(h * HEAD_DIM, (h + 1) * HEAD_DIM)
        o = o_state
        if h + 1 < H_C:
            o_state = across_chunks(h + 1)
        on = _rms(o + within_chunks(h), hn_ref[...])
        o_ref[:, sl] = (on * _silu(zc_ref[:, sl])).astype(o_ref.dtype)

    if not per_chunk_state:
        @pl.when(step == pl.num_programs(1) - 1)
        def _():
            for h in range(H_C):
                sout_ref[0, h] = st_scr[h].T


def _hgrn(proj, lb, hg_norm, s0, nseq, t):
    m = nseq * t
    cs = min(HG_CHUNK, t)
    per_chunk_state = t <= HG_CHUNK
    if per_chunk_state:
        assert s0 is not None
        seq_blk = CHUNK // t
        grid = (nseq // seq_blk, 1)
        state_spec = pl.BlockSpec((seq_blk, H_C, HEAD_DIM, HEAD_DIM), lambda i, j: (i, 0, 0, 0))
        steps = 1
    else:
        assert s0 is None
        steps = t // CHUNK
        grid = (nseq, steps)
        state_spec = pl.BlockSpec((1, H_C, HEAD_DIM, HEAD_DIM), lambda i, j: (i, 0, 0, 0))
    col = lambda tile: pl.BlockSpec((CHUNK, W_C), lambda i, j: (i * steps + j, tile))
    in_specs = [col(COL_QC), col(COL_FC), col(COL_IC), col(COL_ZC),
                pl.BlockSpec((1, W_C), lambda i, j: (0, 0)),
                pl.BlockSpec((1, HEAD_DIM), lambda i, j: (0, 0))]
    args = [proj, proj, proj, proj, lb.reshape(1, W_C), hg_norm.reshape(1, HEAD_DIM)]
    if s0 is not None:
        s0_all, layer = s0
        in_specs.append(pl.BlockSpec((None,) + state_spec.block_shape, lambda i, j: (layer, i, 0, 0, 0)))
        args.append(s0_all)
    return pl.pallas_call(
        functools.partial(_hgrn_kernel, cs=cs, per_chunk_state=per_chunk_state, zero_init=s0 is None),
        grid=grid,
        in_specs=in_specs,
        out_specs=[pl.BlockSpec((CHUNK, W_C), lambda i, j: (i * steps + j, 0)), state_spec],
        out_shape=[jax.ShapeDtypeStruct((m, W_C), BF16),
                   jax.ShapeDtypeStruct((nseq, H_C, HEAD_DIM, HEAD_DIM), F32)],
        scratch_shapes=[pltpu.VMEM((H_C, HEAD_DIM, HEAD_DIM), F32)],
        compiler_params=_params("parallel", "arbitrary"),
        name="hgrn",
    )(*args)


def _out_proj_kernel(x_ref, oa_ref, ob_ref, oc_ref, w_ref, o_ref):
    wa = H_A * HEAD_DIM
    acc = _dot(oa_ref[...].astype(BF16), w_ref[:wa])
    acc += _dot(ob_ref[...], w_ref[wa:wa + W_B])
    acc += _dot(oc_ref[...], w_ref[wa + W_B:])
    o_ref[...] = x_ref[...] + acc


def _out_proj(x, oa, ob, oc, w, layer, tm, tn):
    m, d = x.shape
    kdim = w.shape[1]
    return pl.pallas_call(
        _out_proj_kernel,
        grid=(m // tm, d // tn),
        in_specs=[pl.BlockSpec((tm, tn), lambda i, j: (i, j)),
                  pl.BlockSpec((tm, oa.shape[1]), lambda i, j: (i, 0)),
                  pl.BlockSpec((tm, ob.shape[1]), lambda i, j: (i, 0)),
                  pl.BlockSpec((tm, oc.shape[1]), lambda i, j: (i, 0)),
                  pl.BlockSpec((None, kdim, tn), lambda i, j: (layer, 0, j))],
        out_specs=pl.BlockSpec((tm, tn), lambda i, j: (i, j)),
        out_shape=jax.ShapeDtypeStruct((m, d), F32),
        compiler_params=_params("parallel", "arbitrary"),
        name="out_proj",
    )(x, oa, ob, oc, w)


def _ffn_kernel(x_ref, prev_ref, prevu_ref, g_ref, wg_ref, wu_ref, cwg_ref, cwu_ref, wd_ref, gf_ref,
                o_ref, csg_ref, csu_ref, xn_ref, pn_ref, *, seq_start_every, prev_is_state, final_norm):
    j = pl.program_id(1)
    nb, r, d = x_ref.shape
    rows = nb * r

    @pl.when(j == 0)
    def _():
        xn_ref[...] = _rms(x_ref[...].reshape(rows, d), g_ref[...]).astype(BF16)
        o_ref[...] = jnp.zeros(o_ref.shape, F32)
        if not prev_is_state:
            pn = _rms(prev_ref[...].reshape(SUBLANES, d), g_ref[...])
            pn_ref[...] = jnp.concatenate([pn, jnp.zeros_like(pn)], axis=0).astype(BF16)

    xn = xn_ref[...]
    tn = wg_ref.shape[1]
    if not prev_is_state:
        first = (pl.program_id(0) % seq_start_every) == 0
    ws = tn // FFN_SUB
    cols = [slice(k * ws, (k + 1) * ws) for k in range(FFN_SUB)]

    def up_proj(k):
        cs = cols[k]
        ug = _dot(xn, wg_ref[:, cs]).reshape(nb, r, ws)
        uu = _dot(xn, wu_ref[:, cs]).reshape(nb, r, ws)
        if prev_is_state:
            pg = prev_ref[:, :, cs]
            pu = prevu_ref[:, :, cs]
        else:
            pg = jnp.where(first, 0.0, _dot(pn_ref[...], wg_ref[:, cs])[:SUBLANES]).reshape(1, SUBLANES, ws)
            pu = jnp.where(first, 0.0, _dot(pn_ref[...], wu_ref[:, cs])[:SUBLANES]).reshape(1, SUBLANES, ws)
        return ug, uu, pg, pu

    nxt = up_proj(0)
    down = None
    for k in range(FFN_SUB):
        cs = cols[k]
        ug, uu, pg, pu = nxt
        if k + 1 < FFN_SUB:
            nxt = up_proj(k + 1)
        csg_ref[:, :, cs] = ug[:, r - SUBLANES:, :]
        csu_ref[:, :, cs] = uu[:, r - SUBLANES:, :]
        gate = _dwconv(ug, pg, cwg_ref[:, cs]).reshape(rows, ws)
        up = _dwconv(uu, pu, cwu_ref[:, cs]).reshape(rows, ws)
        hidden = (_silu(gate) * up).astype(BF16)
        part = _dot(hidden, wd_ref[cs, :])
        down = part if down is None else down + part
    o_ref[...] += down.reshape(nb, r, d)

    @pl.when(j == pl.num_programs(1) - 1)
    def _():
        y = (x_ref[...] + o_ref[...]).reshape(rows, d)
        if final_norm:
            y = _rms(y, gf_ref[...])
        o_ref[...] = y.reshape(nb, r, d)


def _ffn(x3, conv_prev, norm_g, w_up, conv_w, w_down, layer, norm_final, nb, r, tn, final_norm):
    nseq, t, d = x3.shape
    nj = D_FF // tn
    steps_per_seq = t // r
    if conv_prev is None:
        assert nb == 1
        rb = r // SUBLANES
        prev_arr = x3
        prev_spec = pl.BlockSpec((1, SUBLANES, d), lambda i, j: (i // steps_per_seq,
                                                                 jnp.maximum((i % steps_per_seq) * rb - 1, 0), 0))
        prevu_spec = prev_spec
        x_map = lambda i, j: (i // steps_per_seq, i % steps_per_seq, 0)
        grid = (nseq * steps_per_seq, nj)
    else:
        assert r == t
        prev_arr = conv_prev
        prev_spec = pl.BlockSpec((nb, SUBLANES, tn), lambda i, j: (i, 0, j))
        prevu_spec = pl.BlockSpec((nb, SUBLANES, tn), lambda i, j: (i, 0, nj + j))
        x_map = lambda i, j: (i, 0, 0)
        grid = (nseq // nb, nj)
    rows = nb * r
    out, csg, csu = pl.pallas_call(
        functools.partial(_ffn_kernel, seq_start_every=steps_per_seq, prev_is_state=conv_prev is not None,
                          final_norm=final_norm),
        grid=grid,
        in_specs=[pl.BlockSpec((nb, r, d), x_map, pipeline_mode=pl.Buffered(1)), prev_spec, prevu_spec,
                  pl.BlockSpec((1, d), lambda i, j: (0, 0)),
                  pl.BlockSpec((None, d, tn), lambda i, j: (layer, 0, j)),
                  pl.BlockSpec((None, d, tn), lambda i, j: (layer, 0, nj + j)),
                  pl.BlockSpec((FFN_CONV, tn), lambda i, j: (0, j)),
                  pl.BlockSpec((FFN_CONV, tn), lambda i, j: (0, nj + j)),
                  pl.BlockSpec((None, tn, d), lambda i, j: (layer, j, 0)),
                  pl.BlockSpec((1, d), lambda i, j: (0, 0))],
        out_specs=[pl.BlockSpec((nb, r, d), x_map),
                   pl.BlockSpec((nb, SUBLANES, tn), lambda i, j: (i, 0, j)),
                   pl.BlockSpec((nb, SUBLANES, tn), lambda i, j: (i, 0, j))],
        out_shape=[jax.ShapeDtypeStruct((nseq, t, d), F32),
                   jax.ShapeDtypeStruct((grid[0] * nb, SUBLANES, D_FF), F32),
                   jax.ShapeDtypeStruct((grid[0] * nb, SUBLANES, D_FF), F32)],
        scratch_shapes=[pltpu.VMEM((rows, d), BF16), pltpu.VMEM((2 * SUBLANES, d), BF16)],
        compiler_params=_params("arbitrary", "arbitrary"),
        name="conv_ffn",
    )(x3, prev_arr, prev_arr, norm_g.reshape(1, d), w_up, w_up, conv_w, conv_w, w_down, norm_final.reshape(1, d))
    csg = csg.reshape(nseq, -1, SUBLANES, D_FF)[:, -1]
    csu = csu.reshape(nseq, -1, SUBLANES, D_FF)[:, -1]
    return out, csg, csu


def _pack_w_in(w):
    d = w.shape[0]
    cuts = np.cumsum([Q_RANK, KV_RANK, ROPE_DIM, 3 * W_B, H_B, H_B, W_B, W_C, W_C, W_C, W_C])
    c_q, c_kv, k_pe, qkv, a_b, b_b, z_b, q_c, f_c, i_c, z_c = jnp.split(w, [int(c) for c in cuts[:-1]], axis=1)
    z = lambda n: jnp.zeros((d, n), w.dtype)
    misc = jnp.concatenate([c_kv, k_pe, z(MISC_AB - MISC_KPE - ROPE_DIM), a_b, b_b,
                            z(PROJ_TILE - MISC_AB - 2 * H_B)], axis=1)
    return jnp.concatenate([qkv, c_q, misc, z_b, q_c, f_c, i_c, z_c], axis=1).astype(BF16)


def _pack_w_uq(w):
    w3 = w.reshape(Q_RANK, H_A, NOPE_DIM + ROPE_DIM)
    return jnp.concatenate([w3[:, :, :NOPE_DIM].reshape(Q_RANK, -1),
                            w3[:, :, NOPE_DIM:].reshape(Q_RANK, -1)], axis=1).astype(BF16)


def _rope_tables(pos):
    half = ROPE_DIM // 2
    inv = ROPE_THETA ** (-jnp.arange(half, dtype=F32) / half)
    ang = pos.astype(F32)[:, None] * inv[None, :]
    cos, sin = jnp.cos(ang), jnp.sin(ang)
    reps = LANES // ROPE_DIM
    return (jnp.tile(jnp.concatenate([cos, cos], axis=1), (1, reps)),
            jnp.tile(jnp.concatenate([-sin, sin], axis=1), (1, reps)))


def _pad_rows_front(a, total):
    n, k, c = a.shape
    return jnp.concatenate([jnp.zeros((n, total - k, c), a.dtype), a], axis=1)


def _row_tile(m, cap=512):
    for tm in (1024, 512, 256, 128, 64, 32, 16, 8):
        if tm <= cap and m % tm == 0:
            return tm
    raise ValueError(m)


def _layer(x3, cos, sin, lw, lb, layer, past, final_norm_g, is_last):
    nseq, t, d = x3.shape
    m = nseq * t
    tm = _row_tile(m)
    x = x3.reshape(m, d)
    proj = _norm_matmul(x, lw["norm_mix"], lw["w_in"], _row_tile(m, 1024), 2 * PROJ_TILE)
    proj3 = proj.reshape(nseq, t, PROJ_WIDTH)

    qlat, qpe, ckv, kpe = _mla_prep(proj, cos, sin, lw["q_norm"], lw["kv_norm"], lw["w_uq"], lw["w_uk_t"],
                                    min(tm, 256), BF16 if past is None else F32)
    if past is None:
        o_a = _prompt_attn(qlat, qpe, ckv, kpe, lw["w_uv_t"], nseq, t, min(256, t))
        gdn_prev = hg_s0 = gdn_s0 = ffn_prev = None
        nb, r = 1, min(tm, t)
        r_ffn = min(_row_tile(m, 1024), t)
    else:
        n_pages = past["page_table"].shape[1]
        pps = 32 if n_pages % 32 == 0 else n_pages
        o_a = _decode_attn(qlat, qpe, ckv, kpe, past["cache_ckv"], past["cache_kpe_t"], past["page_table"],
                           lw["w_uv_t"], layer, t, pps, 4)
        gdn_prev = _pad_rows_front(past["gdn_conv"], SUBLANES)
        gdn_s0, hg_s0 = (past["gdn_S"], layer), (past["hgrn_S"], layer)
        nb, r = min(nseq, 512 // t), t
        r_ffn = r
    q, k, v, gb, gdn_cs = _gdn_prep(proj3, gdn_prev, lw["gdn_conv_w"], lw["gdn_a_log"], lw["gdn_dt_bias"], nb, r)
    o_b, gdn_s = _gdn(q, k, v, gb, proj, lw["gdn_norm"], gdn_s0, nseq, t)
    o_c, hg_s = _hgrn(proj, lb, lw["hg_norm"], hg_s0, nseq, t)
    x = _out_proj(x, o_a, o_b, o_c, lw["w_out"], layer, tm, 1024)

    tn = 512
    ffn_prev = None if past is None else _pad_rows_front(past["ffn_conv"], SUBLANES)
    x3, csg, csu = _ffn(x.reshape(nseq, t, d), ffn_prev, lw["norm_ffn"], lw["w_up"], lw["ffn_conv_w"],
                        lw["w_down"], layer, final_norm_g, nb, r_ffn, tn, is_last)
    ffn_cs = jnp.concatenate([csg, csu], axis=2)[:, SUBLANES - (FFN_CONV - 1):]
    new_state = (ckv.reshape(nseq, t, KV_RANK), kpe.reshape(nseq, t, ROPE_DIM), gdn_s,
                 gdn_cs[:, SUBLANES - (CONV_B - 1):], hg_s, ffn_cs)
    return x3, new_state


def kernel(x_prompt, x_sample, cache_ckv, cache_kpe, page_table, state_gdn_S, state_gdn_conv, state_hgrn_S, state_ffn_conv, norm_mix, w_in, mla_q_norm, mla_kv_norm, mla_w_uq, mla_w_uk, mla_w_uv, gdn_conv_w, gdn_a_log, gdn_dt_bias, gdn_norm, hgrn_lb, hgrn_norm, w_out, norm_ffn, ffn_w_up, ffn_conv_w, ffn_w_down, norm_final):
    depth = w_in.shape[0]
    bp, tp, _ = x_prompt.shape
    bs, ts, _ = x_sample.shape
    past_len = page_table.shape[1] * PAGE_SIZE
    cos_p, sin_p = _rope_tables(jnp.tile(jnp.arange(tp, dtype=jnp.int32), bp))
    cos_s, sin_s = _rope_tables(jnp.tile(past_len + jnp.arange(ts, dtype=jnp.int32), bs))
    p_lb = jax.nn.softmax(hgrn_lb.astype(F32), axis=0)
    lb_all = jnp.cumsum(p_lb, axis=0) - p_lb[0:1]
    cache_kpe_t = jnp.swapaxes(cache_kpe, 2, 3)

    w_out_b, w_up_b, w_down_b = w_out.astype(BF16), ffn_w_up.astype(BF16), ffn_w_down.astype(BF16)

    xp, xs = x_prompt, x_sample
    new_p, new_s = [], []
    for l in range(depth):
        lw = dict(norm_mix=norm_mix[l], w_in=_pack_w_in(w_in[l]), q_norm=mla_q_norm[l], kv_norm=mla_kv_norm[l],
                  w_uq=_pack_w_uq(mla_w_uq[l]),
                  w_uk_t=jnp.transpose(mla_w_uk[l], (1, 2, 0)).astype(BF16),
                  w_uv_t=jnp.transpose(mla_w_uv[l], (1, 0, 2)).astype(BF16),
                  gdn_conv_w=gdn_conv_w[l], gdn_a_log=gdn_a_log[l], gdn_dt_bias=gdn_dt_bias[l],
                  gdn_norm=gdn_norm[l], hg_norm=hgrn_norm[l], w_out=w_out_b,
                  norm_ffn=norm_ffn[l], w_up=w_up_b, ffn_conv_w=ffn_conv_w[l], w_down=w_down_b)
        last = l == depth - 1
        xp, st_p = _layer(xp, cos_p, sin_p, lw, lb_all[l], l, None, norm_final, last)
        past = dict(cache_ckv=cache_ckv, cache_kpe_t=cache_kpe_t, page_table=page_table,
                    gdn_S=state_gdn_S, gdn_conv=state_gdn_conv[l], hgrn_S=state_hgrn_S,
                    ffn_conv=state_ffn_conv[l])
        xs, st_s = _layer(xs, cos_s, sin_s, lw, lb_all[l], l, past, norm_final, last)
        new_p.append(st_p)
        new_s.append(st_s)
    outs_p = tuple(jnp.stack(a) for a in zip(*new_p))
    outs_s = tuple(jnp.stack(a) for a in zip(*new_s))
    return (xp, xs) + outs_p + outs_s
```

```python
import functools

import jax
import jax.numpy as jnp
import numpy as np
from jax import lax
from jax.experimental import pallas as pl
from jax.experimental.pallas import tpu as pltpu

F32 = jnp.float32
BF16 = jnp.bfloat16

D_MODEL = 2048
PAGE_SIZE = 128
HEAD_DIM = 128
H_A = 8
H_B = 4
H_C = 4
Q_RANK = 512
KV_RANK = 256
NOPE_DIM = 128
ROPE_DIM = 64
W_B = H_B * HEAD_DIM
W_C = H_C * HEAD_DIM
CONV_B = 4
D_FF = 5632
FFN_CONV = 3
ROPE_THETA = 10000.0
EPS = 1e-6
NEG = -1e30
LB_FLOOR = 1e-30

PROJ_TILE = 512
COL_QKV = 0
COL_CQ = 3
COL_MISC = 4
COL_ZB = 5
COL_QC = 6
COL_FC = 7
COL_IC = 8
COL_ZC = 9
PROJ_WIDTH = 10 * PROJ_TILE
MISC_KPE = 256
MISC_AB = 384

SUBLANES = 8
LANES = 128
CHUNK = 128
HG_CHUNK = 16
FFN_SUB = 1
DECODE_SLOTS = 3
VMEM_LIMIT = 56 * 1024 * 1024


def _params(*sem):
    return pltpu.CompilerParams(dimension_semantics=sem, vmem_limit_bytes=VMEM_LIMIT)


def _dot(a, b):
    return jnp.dot(a, b, preferred_element_type=F32)


def _dot_nt(a, b):
    return lax.dot_general(a, b, (((1,), (1,)), ((), ())), preferred_element_type=F32)


def _bdot(a, b):
    return _dot(a.astype(BF16), b.astype(BF16))


def _split2(a):
    hi = a.astype(BF16)
    lo = (a - hi.astype(F32)).astype(BF16)
    return hi, lo


def _dot3(a, b):
    ah, al = _split2(a)
    bh, bl = _split2(b)
    return _dot(ah, bh) + (_dot(ah, bl) + _dot(al, bh))


def _dot_exact_lhs(a01, b):
    a = a01.astype(BF16)
    b1 = b.astype(BF16)
    r1 = b - b1.astype(F32)
    b2 = r1.astype(BF16)
    b3 = (r1 - b2.astype(F32)).astype(BF16)
    return _dot(a, b1) + (_dot(a, b2) + _dot(a, b3))


def _sigmoid(x):
    return 1.0 / (1.0 + jnp.exp(-x))


def _silu(x):
    return x * _sigmoid(x)


def _softplus(x):
    return jnp.maximum(x, 0.0) + jnp.log1p(jnp.exp(-jnp.abs(x)))


def _rms(x, g):
    ms = jnp.mean(x * x, axis=-1, keepdims=True)
    return x * lax.rsqrt(ms + EPS) * g


def _swap_halves(x, lane):
    w = x.shape[-1]
    half = ROPE_DIM // 2
    fwd = pltpu.roll(x, w - half, axis=x.ndim - 1)
    bwd = pltpu.roll(x, half, axis=x.ndim - 1)
    return jnp.where((lane % ROPE_DIM) < half, fwd, bwd)


def _rope(x, cos, sin_signed):
    lane = lax.broadcasted_iota(jnp.int32, x.shape, x.ndim - 1)
    return x * cos + _swap_halves(x, lane) * sin_signed


def _norm_matmul_kernel(x_ref, g_ref, w_ref, o_ref, xn_ref):
    @pl.when(pl.program_id(1) == 0)
    def _():
        xn_ref[...] = _rms(x_ref[...], g_ref[...]).astype(BF16)

    o_ref[...] = _dot(xn_ref[...], w_ref[...])


def _norm_matmul(x, g, w, tm, tn):
    m, k = x.shape
    n = w.shape[1]
    return pl.pallas_call(
        _norm_matmul_kernel,
        grid=(m // tm, n // tn),
        in_specs=[pl.BlockSpec((tm, k), lambda i, j: (i, 0)),
                  pl.BlockSpec((1, k), lambda i, j: (0, 0)),
                  pl.BlockSpec((k, tn), lambda i, j: (0, j))],
        out_specs=pl.BlockSpec((tm, tn), lambda i, j: (i, j)),
        out_shape=jax.ShapeDtypeStruct((m, n), F32),
        scratch_shapes=[pltpu.VMEM((tm, k), BF16)],
        compiler_params=_params("parallel", "arbitrary"),
        name="in_proj",
    )(x, g.reshape(1, k), w)


def _mla_prep_kernel(cq_ref, misc_ref, cos_ref, sin_ref, qn_ref, kvn_ref, wuq_ref, wuk_ref,
                     qlat_ref, qpe_ref, ckv_ref, kpe_ref, *, scale):
    cos = cos_ref[...]
    sin = sin_ref[...]
    misc = misc_ref[...]
    ckv_ref[...] = _rms(misc[:, :KV_RANK], kvn_ref[...])
    kpe = _rope(misc[:, MISC_KPE:MISC_KPE + LANES], cos, sin)
    kpe_ref[...] = kpe[:, :ROPE_DIM]

    cqn = _rms(cq_ref[...], qn_ref[...]).astype(BF16)
    q = _dot(cqn, wuq_ref[...])
    n_nope = H_A * NOPE_DIM
    reps = (H_A * ROPE_DIM) // LANES
    q_pe = _rope(q[:, n_nope:], jnp.tile(cos, (1, reps)), jnp.tile(sin, (1, reps))) * scale
    for h in range(H_A):
        qn = q[:, h * NOPE_DIM:(h + 1) * NOPE_DIM].astype(BF16)
        qlat_ref[h] = (_dot(qn, wuk_ref[h]) * scale).astype(qlat_ref.dtype)
        qpe_ref[h] = q_pe[:, h * ROPE_DIM:(h + 1) * ROPE_DIM].astype(qpe_ref.dtype)


def _mla_prep(proj, cos, sin, q_norm, kv_norm, w_uq, w_uk_t, tm, q_dtype):
    m = proj.shape[0]
    scale = float((NOPE_DIM + ROPE_DIM) ** -0.5)
    return pl.pallas_call(
        functools.partial(_mla_prep_kernel, scale=scale),
        grid=(m // tm,),
        in_specs=[pl.BlockSpec((tm, PROJ_TILE), lambda i: (i, COL_CQ)),
                  pl.BlockSpec((tm, PROJ_TILE), lambda i: (i, COL_MISC)),
                  pl.BlockSpec((tm, LANES), lambda i: (i, 0)),
                  pl.BlockSpec((tm, LANES), lambda i: (i, 0)),
                  pl.BlockSpec((1, Q_RANK), lambda i: (0, 0)),
                  pl.BlockSpec((1, KV_RANK), lambda i: (0, 0)),
                  pl.BlockSpec(w_uq.shape, lambda i: (0, 0)),
                  pl.BlockSpec(w_uk_t.shape, lambda i: (0, 0, 0))],
        out_specs=[pl.BlockSpec((H_A, tm, KV_RANK), lambda i: (0, i, 0)),
                   pl.BlockSpec((H_A, tm, ROPE_DIM), lambda i: (0, i, 0)),
                   pl.BlockSpec((tm, KV_RANK), lambda i: (i, 0)),
                   pl.BlockSpec((tm, ROPE_DIM), lambda i: (i, 0))],
        out_shape=[jax.ShapeDtypeStruct((H_A, m, KV_RANK), q_dtype),
                   jax.ShapeDtypeStruct((H_A, m, ROPE_DIM), q_dtype),
                   jax.ShapeDtypeStruct((m, KV_RANK), F32),
                   jax.ShapeDtypeStruct((m, ROPE_DIM), F32)],
        compiler_params=_params("parallel"),
        name="mla_prep",
    )(proj, proj, cos, sin, q_norm.reshape(1, -1), kv_norm.reshape(1, -1), w_uq, w_uk_t)


def _lanes(x, n):
    return x if n == LANES else jnp.concatenate([x] * (n // LANES), axis=1)


def _prompt_attn_kernel(qi_tab, ki_tab, qlat_ref, qpe_ref, ckv_ref, kpe_ref, wuv_ref, o_ref,
                        m_ref, l_ref, acc_ref):
    step = pl.program_id(1)
    qi = qi_tab[step]
    ki = ki_tab[step]
    tq, tk = qlat_ref.shape[1], ckv_ref.shape[0]

    @pl.when(ki == 0)
    def _():
        m_ref[...] = jnp.full(m_ref.shape, NEG, F32)
        l_ref[...] = jnp.zeros(l_ref.shape, F32)
        acc_ref[...] = jnp.zeros(acc_ref.shape, F32)

    def update(masked):
        kc = ckv_ref[...].astype(BF16)
        kp = kpe_ref[...].astype(BF16)
        if masked:
            keep = (lax.broadcasted_iota(jnp.int32, (tq, tk), 1)
                    <= lax.broadcasted_iota(jnp.int32, (tq, tk), 0))
        scores = lambda h: _dot_nt(qlat_ref[h], kc) + _dot_nt(qpe_ref[h], kp)

        def accumulate(h, p, alpha):
            acc_ref[h] = _lanes(alpha, KV_RANK) * acc_ref[h] + _dot(p, kc)

        s_next = scores(0)
        pending = None
        for h in range(H_A):
            s = s_next
            if h + 1 < H_A:
                s_next = scores(h + 1)
            if masked:
                s = jnp.where(keep, s, NEG)
            m_prev = m_ref[h]
            m_new = jnp.maximum(m_prev, jnp.max(s, axis=-1, keepdims=True))
            alpha = jnp.exp(m_prev - m_new)
            p = jnp.exp(s - _lanes(m_new, tk))
            l_ref[h] = alpha * l_ref[h] + jnp.sum(p, axis=-1, keepdims=True)
            m_ref[h] = m_new
            if pending is not None:
                accumulate(*pending)
            pending = (h, p.astype(BF16), alpha)
        accumulate(*pending)

    @pl.when(ki < qi)
    def _():
        update(False)

    @pl.when(ki == qi)
    def _():
        update(True)
        for h in range(H_A):
            o = (acc_ref[h] * _lanes(1.0 / l_ref[h], KV_RANK)).astype(BF16)
            o_ref[:, h * HEAD_DIM:(h + 1) * HEAD_DIM] = _dot(o, wuv_ref[h]).astype(o_ref.dtype)


def _prompt_attn(qlat, qpe, ckv, kpe, w_uv_t, nseq, t, tq):
    nq = t // tq
    pairs = [(qi, ki) for qi in range(nq) for ki in range(qi + 1)]
    qi_tab = jnp.asarray([p[0] for p in pairs], jnp.int32)
    ki_tab = jnp.asarray([p[1] for p in pairs], jnp.int32)
    q_map = lambda b, s, qt, kt: (0, b * nq + qt[s], 0)
    kv_map = lambda b, s, qt, kt: (b * nq + kt[s], 0)
    grid_spec = pltpu.PrefetchScalarGridSpec(
        num_scalar_prefetch=2,
        grid=(nseq, len(pairs)),
        in_specs=[pl.BlockSpec((H_A, tq, KV_RANK), q_map),
                  pl.BlockSpec((H_A, tq, ROPE_DIM), q_map),
                  pl.BlockSpec((tq, KV_RANK), kv_map),
                  pl.BlockSpec((tq, ROPE_DIM), kv_map),
                  pl.BlockSpec(w_uv_t.shape, lambda b, s, qt, kt: (0, 0, 0))],
        out_specs=pl.BlockSpec((tq, H_A * HEAD_DIM), lambda b, s, qt, kt: (b * nq + qt[s], 0)),
        scratch_shapes=[pltpu.VMEM((H_A, tq, LANES), F32), pltpu.VMEM((H_A, tq, LANES), F32),
                        pltpu.VMEM((H_A, tq, KV_RANK), F32)],
    )
    return pl.pallas_call(
        _prompt_attn_kernel,
        grid_spec=grid_spec,
        out_shape=jax.ShapeDtypeStruct((nseq * t, H_A * HEAD_DIM), BF16),
        compiler_params=_params("parallel", "arbitrary"),
        name="prompt_attn",
    )(qi_tab, ki_tab, qlat, qpe, ckv, kpe, w_uv_t)


def _decode_attn_kernel(pt_ref, qlat_ref, qpe_ref, nckv_ref, nkpe_ref, wuv_ref, ckv_hbm, kpe_t_hbm,
                        o_ref, kbuf, pbuf, sem, m_ref, l_ref, acc_ref, *, layer, pps, ts, n_chains):
    b = pl.program_id(0)
    j = pl.program_id(1)
    nj = pl.num_programs(1)
    step = b * nj + j
    last_step = pl.num_programs(0) * nj - 1
    slot = step % DECODE_SLOTS
    rows = H_A * ts

    def page_copies(bb, jj, sl, lookup=True):
        out = []
        for i in range(pps):
            page = pt_ref[bb, jj * pps + i] if lookup else 0
            out.append(pltpu.make_async_copy(ckv_hbm.at[layer, page], kbuf.at[sl, i], sem.at[sl, 0]))
            out.append(pltpu.make_async_copy(kpe_t_hbm.at[layer, page], pbuf.at[sl, i], sem.at[sl, 1]))
        return out

    def fetch_slot(k):
        return (k % DECODE_SLOTS) if isinstance(k, int) else lax.rem(k, DECODE_SLOTS)

    @pl.when(step == 0)
    def _():
        for ahead in range(DECODE_SLOTS - 1):
            tgt = jnp.minimum(ahead, last_step)
            for cp in page_copies(tgt // nj, tgt % nj, fetch_slot(ahead)):
                cp.start()

    for cp in page_copies(b, j, slot, lookup=False):
        cp.wait()
    ahead_slot = fetch_slot(step + DECODE_SLOTS - 1)
    nxt = jnp.minimum(step + DECODE_SLOTS - 1, last_step)
    prefetch = page_copies(nxt // nj, nxt % nj, ahead_slot)

    ql = qlat_ref[...].reshape(rows, KV_RANK).astype(BF16)
    qp = qpe_ref[...].reshape(rows, ROPE_DIM).astype(BF16)

    @pl.when(j == 0)
    def _():
        kc = nckv_ref[...].astype(BF16)
        kp = nkpe_ref[...].astype(BF16)
        s = _dot_nt(ql, kc) + _dot_nt(qp, kp)
        nb = kc.shape[0] // ts
        t_row = lax.broadcasted_iota(jnp.int32, s.shape, 0) % ts
        col = lax.broadcasted_iota(jnp.int32, s.shape, 1)
        ok = (col // ts == b % nb) & (col % ts <= t_row)
        s = jnp.where(ok, s, NEG)
        m = jnp.max(s, axis=-1, keepdims=True)
        p = jnp.where(ok, jnp.exp(s - m), 0.0)
        m_ref[0] = m
        l_ref[0] = jnp.sum(p, axis=-1, keepdims=True)
        acc_ref[0] = _dot(p.astype(BF16), kc)
        for c in range(1, n_chains):
            m_ref[c] = jnp.full((rows, 1), NEG, F32)
            l_ref[c] = jnp.zeros((rows, 1), F32)
            acc_ref[c] = jnp.zeros((rows, KV_RANK), F32)

    per = pps // n_chains
    def chain_scores(c):
        kcs, scores = [], []
        for i in range(c * per, (c + 1) * per):
            prefetch[2 * i].start()
            prefetch[2 * i + 1].start(priority=1)
            kcs.append(kbuf[slot, i].astype(BF16))
            scores.append(_dot_nt(ql, kcs[-1]) + _dot(qp, pbuf[slot, i].astype(BF16)))
        return kcs, jnp.concatenate(scores, axis=1)

    nxt_chain = chain_scores(0)
    for c in range(n_chains):
        kcs, s = nxt_chain
        if c + 1 < n_chains:
            nxt_chain = chain_scores(c + 1)
        m_prev = m_ref[c]
        m_new = jnp.maximum(m_prev, jnp.max(s, axis=-1, keepdims=True))
        alpha = jnp.exp(m_prev - m_new)
        p = jnp.exp(s - m_new).astype(BF16)
        l_ref[c] = alpha * l_ref[c] + jnp.sum(p.astype(F32), axis=-1, keepdims=True)
        pv = _dot(p[:, :PAGE_SIZE], kcs[0])
        for i in range(1, per):
            pv += _dot(p[:, i * PAGE_SIZE:(i + 1) * PAGE_SIZE], kcs[i])
        acc_ref[c] = alpha * acc_ref[c] + pv
        m_ref[c] = m_new

    @pl.when(step == last_step)
    def _():
        for extra in range(1, DECODE_SLOTS):
            for cp in page_copies(b, j, fetch_slot(step + extra), lookup=False):
                cp.wait()

    @pl.when(j == nj - 1)
    def _():
        m = m_ref[0]
        for c in range(1, n_chains):
            m = jnp.maximum(m, m_ref[c])
        l = jnp.zeros((rows, 1), F32)
        acc = jnp.zeros((rows, KV_RANK), F32)
        for c in range(n_chains):
            w = jnp.exp(m_ref[c] - m)
            l += w * l_ref[c]
            acc += w * acc_ref[c]
        o = acc * (1.0 / l)
        for h in range(H_A):
            oh = o[h * ts:(h + 1) * ts].astype(BF16)
            o_ref[:, h * HEAD_DIM:(h + 1) * HEAD_DIM] = _dot(oh, wuv_ref[h]).astype(o_ref.dtype)


def _decode_attn(qlat, qpe, ckv_new, kpe_new, cache_ckv, cache_kpe_t, page_table, w_uv_t, layer, ts, pps, n_chains):
    nseq, n_pages = page_table.shape
    rows = H_A * ts
    new_rows = min(LANES, nseq * ts)
    nb = new_rows // ts

    grid_spec = pltpu.PrefetchScalarGridSpec(
        num_scalar_prefetch=1,
        grid=(nseq, n_pages // pps),
        in_specs=[pl.BlockSpec((H_A, ts, KV_RANK), lambda b, j, pt: (0, b, 0)),
                  pl.BlockSpec((H_A, ts, ROPE_DIM), lambda b, j, pt: (0, b, 0)),
                  pl.BlockSpec((new_rows, KV_RANK), lambda b, j, pt: (b // nb, 0)),
                  pl.BlockSpec((new_rows, ROPE_DIM), lambda b, j, pt: (b // nb, 0)),
                  pl.BlockSpec(w_uv_t.shape, lambda b, j, pt: (0, 0, 0)),
                  pl.BlockSpec(memory_space=pl.ANY),
                  pl.BlockSpec(memory_space=pl.ANY)],
        out_specs=pl.BlockSpec((ts, H_A * HEAD_DIM), lambda b, j, pt: (b, 0)),
        scratch_shapes=[pltpu.VMEM((DECODE_SLOTS, pps, PAGE_SIZE, KV_RANK), F32),
                        pltpu.VMEM((DECODE_SLOTS, pps, ROPE_DIM, PAGE_SIZE), F32),
                        pltpu.SemaphoreType.DMA((DECODE_SLOTS, 2)),
                        pltpu.VMEM((n_chains, rows, 1), F32), pltpu.VMEM((n_chains, rows, 1), F32),
                        pltpu.VMEM((n_chains, rows, KV_RANK), F32)],
    )
    return pl.pallas_call(
        functools.partial(_decode_attn_kernel, layer=layer, pps=pps, ts=ts, n_chains=n_chains),
        grid_spec=grid_spec,
        out_shape=jax.ShapeDtypeStruct((nseq * ts, H_A * HEAD_DIM), F32),
        compiler_params=_params("arbitrary", "arbitrary"),
        name="decode_attn",
    )(page_table, qlat, qpe, ckv_new, kpe_new, w_uv_t, cache_ckv, cache_kpe_t)


def _shifted(x, prev, j):
    if j == 0:
        return x
    rx = pltpu.roll(x, j, axis=1)
    rp = pltpu.roll(prev, j, axis=1)
    row = lax.broadcasted_iota(jnp.int32, prev.shape, 1)
    head = jnp.where(row < j, rp, rx[:, :SUBLANES])
    if x.shape[1] == SUBLANES:
        return head
    return jnp.concatenate([head, rx[:, SUBLANES:]], axis=1)


def _dwconv(x, prev, w):
    k = w.shape[0]
    y = x * w[k - 1:k]
    for j in range(k - 1):
        y = y + _shifted(x, prev, k - 1 - j) * w[j:j + 1]
    return y


def _gdn_prep_kernel(x_ref, prev_ref, misc_ref, cw_ref, alog_ref, dtb_ref,
                     q_ref, k_ref, v_ref, gb_ref, cs_ref, *, seq_start_every):
    x = x_ref[...]
    nb, r, c = x.shape
    prev = prev_ref[...]
    if seq_start_every:
        first = (pl.program_id(0) % seq_start_every) == 0
        prev = jnp.where(first, 0.0, prev)
    y = _silu(_dwconv(x, prev, cw_ref[...])).reshape(nb * r, c)
    cs_ref[...] = x[:, r - SUBLANES:, :]
    for h in range(H_B):
        sl = slice(h * HEAD_DIM, (h + 1) * HEAD_DIM)
        qh = y[:, h * HEAD_DIM:(h + 1) * HEAD_DIM]
        kh = y[:, W_B + h * HEAD_DIM:W_B + (h + 1) * HEAD_DIM]
        q_ref[:, sl] = qh * lax.rsqrt(jnp.sum(qh * qh, axis=-1, keepdims=True) + EPS) * (HEAD_DIM ** -0.5)
        k_ref[:, sl] = kh * lax.rsqrt(jnp.sum(kh * kh, axis=-1, keepdims=True) + EPS)
    v_ref[...] = y[:, 2 * W_B:]
    ab = misc_ref[...].reshape(nb * r, PROJ_TILE)[:, MISC_AB:MISC_AB + LANES]
    g = -jnp.exp(alog_ref[...]) * _softplus(ab + dtb_ref[...])
    lane = lax.broadcasted_iota(jnp.int32, ab.shape, 1)
    gb_ref[...] = jnp.where(lane < H_B, g, _sigmoid(ab))


def _gdn_prep(proj3, conv_prev, conv_w, a_log, dt_bias, nb, r):
    nseq, t, _ = proj3.shape
    c = 3 * W_B
    steps_per_seq = t // r
    m = nseq * t
    rows = nb * r
    if conv_prev is None:
        assert nb == 1
        rb = r // SUBLANES
        prev_arr = proj3
        prev_spec = pl.BlockSpec((1, SUBLANES, c), lambda i: (i // steps_per_seq,
                                                               jnp.maximum((i % steps_per_seq) * rb - 1, 0), COL_QKV))
        x_map = lambda i: (i // steps_per_seq, i % steps_per_seq, COL_QKV)
        misc_map = lambda i: (i // steps_per_seq, i % steps_per_seq, COL_MISC)
        grid = (nseq * steps_per_seq,)
        seq_start_every = steps_per_seq
    else:
        assert r == t
        prev_arr = conv_prev
        prev_spec = pl.BlockSpec((nb, SUBLANES, c), lambda i: (i, 0, 0))
        x_map = lambda i: (i, 0, COL_QKV)
        misc_map = lambda i: (i, 0, COL_MISC)
        grid = (nseq // nb,)
        seq_start_every = 0
    alog = jnp.zeros((1, LANES), F32).at[0, :H_B].set(a_log.astype(F32))
    dtb = jnp.zeros((1, LANES), F32).at[0, :H_B].set(dt_bias.astype(F32))
    row_spec = lambda w: pl.BlockSpec((rows, w), lambda i: (i, 0))
    q, k, v, gb, cs = pl.pallas_call(
        functools.partial(_gdn_prep_kernel, seq_start_every=seq_start_every),
        grid=grid,
        in_specs=[pl.BlockSpec((nb, r, c), x_map), prev_spec,
                  pl.BlockSpec((nb, r, PROJ_TILE), misc_map),
                  pl.BlockSpec((CONV_B, c), lambda i: (0, 0)),
                  pl.BlockSpec((1, LANES), lambda i: (0, 0)),
                  pl.BlockSpec((1, LANES), lambda i: (0, 0))],
        out_specs=[row_spec(W_B), row_spec(W_B), row_spec(W_B), row_spec(LANES),
                   pl.BlockSpec((nb, SUBLANES, c), lambda i: (i, 0, 0))],
        out_shape=[jax.ShapeDtypeStruct((m, W_B), F32), jax.ShapeDtypeStruct((m, W_B), F32),
                   jax.ShapeDtypeStruct((m, W_B), F32), jax.ShapeDtypeStruct((m, LANES), F32),
                   jax.ShapeDtypeStruct((grid[0] * nb, SUBLANES, c), F32)],
        compiler_params=_params("arbitrary"),
        name="gdn_prep",
    )(proj3, prev_arr, proj3, conv_w, alog, dtb)
    cs = cs.reshape(nseq, -1, SUBLANES, c)[:, -1]
    return q, k, v, gb, cs


def _gdn_kernel(q_ref, k_ref, v_ref, gb_ref, z_ref, gn_ref, *rest, nseq_blk, r, zero_init):
    if zero_init:
        o_ref, sout_ref, s_scr, wS_scr, qS_scr, vn_scr = rest
    else:
        s0_ref, o_ref, sout_ref, s_scr, wS_scr, qS_scr, vn_scr = rest
    c = nseq_blk * r
    ci = pl.program_id(1)

    @pl.when(ci == 0)
    def _():
        if zero_init:
            s_scr[...] = jnp.zeros(s_scr.shape, F32)
        else:
            s_scr[...] = s0_ref[...]

    row = lax.broadcasted_iota(jnp.int32, (c, c), 0)
    col = lax.broadcasted_iota(jnp.int32, (c, c), 1)
    same = (row // r) == (col // r)
    causal = same & (col <= row)
    strict = same & (col < row)
    eye = (row == col).astype(F32)

    gb = gb_ref[...]
    gcs = _dot_exact_lhs(causal.astype(F32), gb)
    gtot = _dot_exact_lhs(same.astype(F32), gb)
    gcs_t = gcs.T
    n_sq = max(int(np.ceil(np.log2(r))) - 1, 0)

    heads = range(H_B)
    sls = [slice(h * HEAD_DIM, (h + 1) * HEAD_DIM) for h in heads]
    qs = [q_ref[:, sl] for sl in sls]
    ks = [k_ref[:, sl] for sl in sls]
    vs = [v_ref[:, sl] for sl in sls]
    gcl = [gcs[:, h:h + 1] for h in heads]
    gll = [gtot[:, h:h + 1] for h in heads]
    betas = [gb[:, H_B + h:H_B + h + 1] for h in heads]
    decays = [jnp.exp(jnp.where(causal, gcl[h] - gcs_t[h:h + 1, :], NEG)) for h in heads]
    kbs = [ks[h] * betas[h] for h in heads]
    mms = [jnp.where(strict, _dot_nt(kbs[h].astype(BF16), ks[h].astype(BF16)) * decays[h], 0.0) for h in heads]
    tinvs = [eye - mm for mm in mms]
    pws = list(mms)
    for _ in range(n_sq):
        pws = [_dot3(pw, pw) for pw in pws]
        tinvs = [tinv + _dot3(tinv, pw) for tinv, pw in zip(tinvs, pws)]
    egcs = [jnp.exp(gc) for gc in gcl]
    us = [_dot3(tinvs[h], vs[h] * betas[h]) for h in heads]
    ws = [_dot3(tinvs[h], kbs[h] * egcs[h]).astype(BF16) for h in heads]
    qks = [(_dot_nt(qs[h].astype(BF16), ks[h].astype(BF16)) * decays[h]).astype(BF16) for h in heads]
    q_decs = [(qs[h] * egcs[h]).astype(BF16) for h in heads]
    k_dec_ts = [(ks[h] * jnp.exp(gll[h] - gcl[h])).T for h in heads]

    if nseq_blk == 1:
        sbs = [s_scr[0, h].astype(BF16) for h in heads]
        v_news = [us[h] - _dot(ws[h], sbs[h]) for h in heads]
        vnb = [v.astype(BF16) for v in v_news]
        os_ = [_dot(q_decs[h], sbs[h]) + _dot(qks[h], vnb[h]) for h in heads]
        for h in heads:
            s_scr[0, h] = (s_scr[0, h] * jnp.exp(gll[h][c - 1:c, :])
                           + _dot(k_dec_ts[h].astype(BF16), vnb[h]))
        for h in heads:
            on = _rms(os_[h], gn_ref[...])
            o_ref[:, sls[h]] = (on * _silu(z_ref[:, sls[h]])).astype(o_ref.dtype)

    else:
        rowc = lax.broadcasted_iota(jnp.int32, (c, HEAD_DIM), 0) // r
        colc = lax.broadcasted_iota(jnp.int32, (HEAD_DIM, c), 1) // r
        wS_scr[...] = jnp.zeros(wS_scr.shape, F32)
        qS_scr[...] = jnp.zeros(qS_scr.shape, F32)

        def read_state(p, _):
            mine = rowc == p
            for h in heads:
                sb = s_scr[p, h].astype(BF16)
                wS_scr[h] += _dot(jnp.where(mine, ws[h], jnp.zeros_like(ws[h])), sb)
                qS_scr[h] += _dot(jnp.where(mine, q_decs[h], jnp.zeros_like(q_decs[h])), sb)
            return 0

        lax.fori_loop(0, nseq_blk, read_state, 0, unroll=2)
        for h in heads:
            vn_scr[h] = (us[h] - wS_scr[h]).astype(BF16)
            on = _rms(qS_scr[h] + _dot(qks[h], vn_scr[h]), gn_ref[...])
            o_ref[:, sls[h]] = (on * _silu(z_ref[:, sls[h]])).astype(o_ref.dtype)
        e_gls = [jnp.exp(gl) for gl in gll]

        def write_state(p, _):
            for h in heads:
                kd = jnp.where(colc == p, k_dec_ts[h], 0.0).astype(BF16)
                g_last = jnp.max(jnp.where(rowc[:, :1] == p, e_gls[h], 0.0), axis=0, keepdims=True)
                s_scr[p, h] = s_scr[p, h] * g_last + _dot(kd, vn_scr[h])
            return 0

        lax.fori_loop(0, nseq_blk, write_state, 0, unroll=2)

    @pl.when(ci == pl.num_programs(1) - 1)
    def _():
        sout_ref[...] = s_scr[...]


def _gdn(q, k, v, gb, proj, gdn_norm, s0, nseq, t):
    m = nseq * t
    if t >= CHUNK:
        nseq_blk, r = 1, CHUNK
    else:
        nseq_blk, r = CHUNK // t, t
    n_chunks = t // r
    n_blocks = nseq // nseq_blk
    row_map = lambda i, ci: (i * n_chunks + ci, 0)
    state_spec = pl.BlockSpec((nseq_blk, H_B, HEAD_DIM, HEAD_DIM), lambda i, ci: (i, 0, 0, 0))
    in_specs = [pl.BlockSpec((CHUNK, W_B), row_map), pl.BlockSpec((CHUNK, W_B), row_map),
                pl.BlockSpec((CHUNK, W_B), row_map), pl.BlockSpec((CHUNK, LANES), row_map),
                pl.BlockSpec((CHUNK, W_B), lambda i, ci: (i * n_chunks + ci, COL_ZB)),
                pl.BlockSpec((1, HEAD_DIM), lambda i, ci: (0, 0))]
    args = [q, k, v, gb, proj, gdn_norm.reshape(1, HEAD_DIM)]
    if s0 is not None:
        s0_all, layer = s0
        in_specs.append(pl.BlockSpec((None, nseq_blk, H_B, HEAD_DIM, HEAD_DIM), lambda i, ci: (layer, i, 0, 0, 0)))
        args.append(s0_all)
    return pl.pallas_call(
        functools.partial(_gdn_kernel, nseq_blk=nseq_blk, r=r, zero_init=s0 is None),
        grid=(n_blocks, n_chunks),
        in_specs=in_specs,
        out_specs=[pl.BlockSpec((CHUNK, W_B), row_map), state_spec],
        out_shape=[jax.ShapeDtypeStruct((m, W_B), BF16),
                   jax.ShapeDtypeStruct((nseq, H_B, HEAD_DIM, HEAD_DIM), F32)],
        scratch_shapes=[pltpu.VMEM((nseq_blk, H_B, HEAD_DIM, HEAD_DIM), F32),
                        pltpu.VMEM((H_B, CHUNK, HEAD_DIM), F32), pltpu.VMEM((H_B, CHUNK, HEAD_DIM), F32),
                        pltpu.VMEM((H_B, CHUNK, HEAD_DIM), BF16)],
        compiler_params=_params("parallel", "arbitrary"),
        name="gdn",
    )(*args)


def _hgrn_kernel(qc_ref, fc_ref, ic_ref, zc_ref, lb_ref, hn_ref, *rest, cs, per_chunk_state, zero_init):
    if zero_init:
        o_ref, sout_ref, st_scr = rest
    else:
        s0_ref, o_ref, sout_ref, st_scr = rest
    c = qc_ref.shape[0]
    n_ch = c // cs
    step = pl.program_id(1)

    if not per_chunk_state:
        @pl.when(step == 0)
        def _():
            st_scr[...] = jnp.zeros(st_scr.shape, F32)

    lb = lb_ref[...]
    fr = fc_ref[...]
    log_lb = jnp.log(jnp.maximum(lb, LB_FLOOR))
    a = log_lb
    b = jnp.log1p(-lb) - _softplus(-fr)
    mx = jnp.maximum(a, b)
    logf = mx + jnp.log1p(jnp.exp(-jnp.abs(a - b)))
    kk = (1.0 - lb) * _sigmoid(-fr)
    qq = _silu(qc_ref[...])

    row = lax.broadcasted_iota(jnp.int32, (c, c), 0)
    col = lax.broadcasted_iota(jnp.int32, (c, c), 1)
    same = (row // cs) == (col // cs)
    bc_all = _dot_exact_lhs((same & (col <= row)).astype(F32), logf)
    btot_all = _dot_exact_lhs(same.astype(F32), logf)
    t_idx = lax.broadcasted_iota(jnp.int32, (n_ch, cs, HEAD_DIM), 1)
    rowc = lax.broadcasted_iota(jnp.int32, (c, HEAD_DIM), 0) // cs

    def head(h):
        sl = slice(h * HEAD_DIM, (h + 1) * HEAD_DIM)
        return qq[:, sl], kk[:, sl], ic_ref[:, sl], bc_all[:, sl], btot_all[:, sl]

    def within_chunks(h):
        q, k, v, bc, _ = head(h)
        q3, k3, v3, bc3 = (x.reshape(n_ch, cs, HEAD_DIM) for x in (q, k, v, bc))
        o3 = jnp.zeros((n_ch, cs, HEAD_DIM), F32)
        for s in range(cs):
            dec = jnp.exp(jnp.where(t_idx >= s, bc3 - bc3[:, s:s + 1, :], NEG))
            a_col = jnp.sum(q3 * k3[:, s:s + 1, :] * dec, axis=-1, keepdims=True)
            o3 = o3 + a_col * v3[:, s:s + 1, :]
        return o3.reshape(c, HEAD_DIM)

    def across_chunks(h):
        q, k, v, bc, btot = head(h)
        q_dec = (q * jnp.exp(bc)).astype(BF16)
        k_dec = (k * jnp.exp(btot - bc)).astype(BF16)
        v_t = v.T.astype(BF16)
        e_tot = jnp.exp(btot)
        if per_chunk_state:
            def chunk_step(n, o_acc):
                mine = rowc == n
                st = s0_ref[n, h].T
                o_acc = o_acc + _dot_nt(jnp.where(mine, q_dec, jnp.zeros_like(q_dec)), st.astype(BF16))
                dec_n = jnp.max(jnp.where(mine, e_tot, 0.0), axis=0, keepdims=True)
                st = st * dec_n + _dot(v_t, jnp.where(mine, k_dec, jnp.zeros_like(k_dec)))
                sout_ref[n, h] = st.T
                return o_acc

            return lax.fori_loop(0, n_ch, chunk_step, jnp.zeros((c, HEAD_DIM), F32), unroll=4)
        st = st_scr[h]
        pieces = []
        for n in range(n_ch):
            rs = slice(n * cs, (n + 1) * cs)
            pieces.append(_dot_nt(q_dec[rs], st.astype(BF16)))
            kv = _dot(v_t, jnp.where(rowc == n, k_dec, jnp.zeros_like(k_dec)))
            st = st * e_tot[n * cs:n * cs + 1] + kv
        st_scr[h] = st
        return jnp.concatenate(pieces, axis=0)

    o_state = across_chunks(0)
    for h in range(H_C):
        sl = slice(h * HEAD_DIM, (h + 1) * HEAD_DIM)
        o = o_state
        if h + 1 < H_C:
            o_state = across_chunks(h + 1)
        on = _rms(o + within_chunks(h), hn_ref[...])
        o_ref[:, sl] = (on * _silu(zc_ref[:, sl])).astype(o_ref.dtype)

    if not per_chunk_state:
        @pl.when(step == pl.num_programs(1) - 1)
        def _():
            for h in range(H_C):
                sout_ref[0, h] = st_scr[h].T


def _hgrn(proj, lb, hg_norm, s0, nseq, t):
    m = nseq * t
    cs = min(HG_CHUNK, t)
    per_chunk_state = t <= HG_CHUNK
    if per_chunk_state:
        assert s0 is not None
        seq_blk = CHUNK // t
        grid = (nseq // seq_blk, 1)
        state_spec = pl.BlockSpec((seq_blk, H_C, HEAD_DIM, HEAD_DIM), lambda i, j: (i, 0, 0, 0))
        steps = 1
    else:
        assert s0 is None
        steps = t // CHUNK
        grid = (nseq, steps)
        state_spec = pl.BlockSpec((1, H_C, HEAD_DIM, HEAD_DIM), lambda i, j: (i, 0, 0, 0))
    col = lambda tile: pl.BlockSpec((CHUNK, W_C), lambda i, j: (i * steps + j, tile))
    in_specs = [col(COL_QC), col(COL_FC), col(COL_IC), col(COL_ZC),
                pl.BlockSpec((1, W_C), lambda i, j: (0, 0)),
                pl.BlockSpec((1, HEAD_DIM), lambda i, j: (0, 0))]
    args = [proj, proj, proj, proj, lb.reshape(1, W_C), hg_norm.reshape(1, HEAD_DIM)]
    if s0 is not None:
        s0_all, layer = s0
        in_specs.append(pl.BlockSpec((None,) + state_spec.block_shape, lambda i, j: (layer, i, 0, 0, 0)))
        args.append(s0_all)
    return pl.pallas_call(
        functools.partial(_hgrn_kernel, cs=cs, per_chunk_state=per_chunk_state, zero_init=s0 is None),
        grid=grid,
        in_specs=in_specs,
        out_specs=[pl.BlockSpec((CHUNK, W_C), lambda i, j: (i * steps + j, 0)), state_spec],
        out_shape=[jax.ShapeDtypeStruct((m, W_C), BF16),
                   jax.ShapeDtypeStruct((nseq, H_C, HEAD_DIM, HEAD_DIM), F32)],
        scratch_shapes=[pltpu.VMEM((H_C, HEAD_DIM, HEAD_DIM), F32)],
        compiler_params=_params("parallel", "arbitrary"),
        name="hgrn",
    )(*args)


def _out_proj_kernel(x_ref, oa_ref, ob_ref, oc_ref, w_ref, o_ref):
    wa = H_A * HEAD_DIM
    acc = _dot(oa_ref[...].astype(BF16), w_ref[:wa])
    acc += _dot(ob_ref[...], w_ref[wa:wa + W_B])
    acc += _dot(oc_ref[...], w_ref[wa + W_B:])
    o_ref[...] = x_ref[...] + acc


def _out_proj(x, oa, ob, oc, w, layer, tm, tn):
    m, d = x.shape
    kdim = w.shape[1]
    return pl.pallas_call(
        _out_proj_kernel,
        grid=(m // tm, d // tn),
        in_specs=[pl.BlockSpec((tm, tn), lambda i, j: (i, j)),
                  pl.BlockSpec((tm, oa.shape[1]), lambda i, j: (i, 0)),
                  pl.BlockSpec((tm, ob.shape[1]), lambda i, j: (i, 0)),
                  pl.BlockSpec((tm, oc.shape[1]), lambda i, j: (i, 0)),
                  pl.BlockSpec((None, kdim, tn), lambda i, j: (layer, 0, j))],
        out_specs=pl.BlockSpec((tm, tn), lambda i, j: (i, j)),
        out_shape=jax.ShapeDtypeStruct((m, d), F32),
        compiler_params=_params("parallel", "arbitrary"),
        name="out_proj",
    )(x, oa, ob, oc, w)


def _ffn_kernel(x_ref, prev_ref, prevu_ref, g_ref, wg_ref, wu_ref, cwg_ref, cwu_ref, wd_ref, gf_ref,
                o_ref, csg_ref, csu_ref, xn_ref, pn_ref, *, seq_start_every, prev_is_state, final_norm):
    j = pl.program_id(1)
    nb, r, d = x_ref.shape
    rows = nb * r

    @pl.when(j == 0)
    def _():
        xn_ref[...] = _rms(x_ref[...].reshape(rows, d), g_ref[...]).astype(BF16)
        o_ref[...] = jnp.zeros(o_ref.shape, F32)
        if not prev_is_state:
            pn = _rms(prev_ref[...].reshape(SUBLANES, d), g_ref[...])
            pn_ref[...] = jnp.concatenate([pn, jnp.zeros_like(pn)], axis=0).astype(BF16)

    xn = xn_ref[...]
    tn = wg_ref.shape[1]
    if not prev_is_state:
        first = (pl.program_id(0) % seq_start_every) == 0
    ws = tn // FFN_SUB
    cols = [slice(k * ws, (k + 1) * ws) for k in range(FFN_SUB)]

    def up_proj(k):
        cs = cols[k]
        ug = _dot(xn, wg_ref[:, cs]).reshape(nb, r, ws)
        uu = _dot(xn, wu_ref[:, cs]).reshape(nb, r, ws)
        if prev_is_state:
            pg = prev_ref[:, :, cs]
            pu = prevu_ref[:, :, cs]
        else:
            pg = jnp.where(first, 0.0, _dot(pn_ref[...], wg_ref[:, cs])[:SUBLANES]).reshape(1, SUBLANES, ws)
            pu = jnp.where(first, 0.0, _dot(pn_ref[...], wu_ref[:, cs])[:SUBLANES]).reshape(1, SUBLANES, ws)
        return ug, uu, pg, pu

    nxt = up_proj(0)
    down = None
    for k in range(FFN_SUB):
        cs = cols[k]
        ug, uu, pg, pu = nxt
        if k + 1 < FFN_SUB:
            nxt = up_proj(k + 1)
        csg_ref[:, :, cs] = ug[:, r - SUBLANES:, :]
        csu_ref[:, :, cs] = uu[:, r - SUBLANES:, :]
        gate = _dwconv(ug, pg, cwg_ref[:, cs]).reshape(rows, ws)
        up = _dwconv(uu, pu, cwu_ref[:, cs]).reshape(rows, ws)
        hidden = (_silu(gate) * up).astype(BF16)
        part = _dot(hidden, wd_ref[cs, :])
        down = part if down is None else down + part
    o_ref[...] += down.reshape(nb, r, d)

    @pl.when(j == pl.num_programs(1) - 1)
    def _():
        y = (x_ref[...] + o_ref[...]).reshape(rows, d)
        if final_norm:
            y = _rms(y, gf_ref[...])
        o_ref[...] = y.reshape(nb, r, d)


def _ffn(x3, conv_prev, norm_g, w_up, conv_w, w_down, layer, norm_final, nb, r, tn, final_norm):
    nseq, t, d = x3.shape
    nj = D_FF // tn
    steps_per_seq = t // r
    if conv_prev is None:
        assert nb == 1
        rb = r // SUBLANES
        prev_arr = x3
        prev_spec = pl.BlockSpec((1, SUBLANES, d), lambda i, j: (i // steps_per_seq,
                                                                 jnp.maximum((i % steps_per_seq) * rb - 1, 0), 0))
        prevu_spec = prev_spec
        x_map = lambda i, j: (i // steps_per_seq, i % steps_per_seq, 0)
        grid = (nseq * steps_per_seq, nj)
    else:
        assert r == t
        prev_arr = conv_prev
        prev_spec = pl.BlockSpec((nb, SUBLANES, tn), lambda i, j: (i, 0, j))
        prevu_spec = pl.BlockSpec((nb, SUBLANES, tn), lambda i, j: (i, 0, nj + j))
        x_map = lambda i, j: (i, 0, 0)
        grid = (nseq // nb, nj)
    rows = nb * r
    out, csg, csu = pl.pallas_call(
        functools.partial(_ffn_kernel, seq_start_every=steps_per_seq, prev_is_state=conv_prev is not None,
                          final_norm=final_norm),
        grid=grid,
        in_specs=[pl.BlockSpec((nb, r, d), x_map, pipeline_mode=pl.Buffered(1)), prev_spec, prevu_spec,
                  pl.BlockSpec((1, d), lambda i, j: (0, 0)),
                  pl.BlockSpec((None, d, tn), lambda i, j: (layer, 0, j)),
                  pl.BlockSpec((None, d, tn), lambda i, j: (layer, 0, nj + j)),
                  pl.BlockSpec((FFN_CONV, tn), lambda i, j: (0, j)),
                  pl.BlockSpec((FFN_CONV, tn), lambda i, j: (0, nj + j)),
                  pl.BlockSpec((None, tn, d), lambda i, j: (layer, j, 0)),
                  pl.BlockSpec((1, d), lambda i, j: (0, 0))],
        out_specs=[pl.BlockSpec((nb, r, d), x_map),
                   pl.BlockSpec((nb, SUBLANES, tn), lambda i, j: (i, 0, j)),
                   pl.BlockSpec((nb, SUBLANES, tn), lambda i, j: (i, 0, j))],
        out_shape=[jax.ShapeDtypeStruct((nseq, t, d), F32),
                   jax.ShapeDtypeStruct((grid[0] * nb, SUBLANES, D_FF), F32),
                   jax.ShapeDtypeStruct((grid[0] * nb, SUBLANES, D_FF), F32)],
        scratch_shapes=[pltpu.VMEM((rows, d), BF16), pltpu.VMEM((2 * SUBLANES, d), BF16)],
        compiler_params=_params("arbitrary", "arbitrary"),
        name="conv_ffn",
    )(x3, prev_arr, prev_arr, norm_g.reshape(1, d), w_up, w_up, conv_w, conv_w, w_down, norm_final.reshape(1, d))
    csg = csg.reshape(nseq, -1, SUBLANES, D_FF)[:, -1]
    csu = csu.reshape(nseq, -1, SUBLANES, D_FF)[:, -1]
    return out, csg, csu


def _pack_w_in(w):
    d = w.shape[0]
    cuts = np.cumsum([Q_RANK, KV_RANK, ROPE_DIM, 3 * W_B, H_B, H_B, W_B, W_C, W_C, W_C, W_C])
    c_q, c_kv, k_pe, qkv, a_b, b_b, z_b, q_c, f_c, i_c, z_c = jnp.split(w, [int(c) for c in cuts[:-1]], axis=1)
    z = lambda n: jnp.zeros((d, n), w.dtype)
    misc = jnp.concatenate([c_kv, k_pe, z(MISC_AB - MISC_KPE - ROPE_DIM), a_b, b_b,
                            z(PROJ_TILE - MISC_AB - 2 * H_B)], axis=1)
    return jnp.concatenate([qkv, c_q, misc, z_b, q_c, f_c, i_c, z_c], axis=1).astype(BF16)


def _pack_w_uq(w):
    w3 = w.reshape(Q_RANK, H_A, NOPE_DIM + ROPE_DIM)
    return jnp.concatenate([w3[:, :, :NOPE_DIM].reshape(Q_RANK, -1),
                            w3[:, :, NOPE_DIM:].reshape(Q_RANK, -1)], axis=1).astype(BF16)


def _rope_tables(pos):
    half = ROPE_DIM // 2
    inv = ROPE_THETA ** (-jnp.arange(half, dtype=F32) / half)
    ang = pos.astype(F32)[:, None] * inv[None, :]
    cos, sin = jnp.cos(ang), jnp.sin(ang)
    reps = LANES // ROPE_DIM
    return (jnp.tile(jnp.concatenate([cos, cos], axis=1), (1, reps)),
            jnp.tile(jnp.concatenate([-sin, sin], axis=1), (1, reps)))


def _pad_rows_front(a, total):
    n, k, c = a.shape
    return jnp.concatenate([jnp.zeros((n, total - k, c), a.dtype), a], axis=1)


def _row_tile(m, cap=512):
    for tm in (1024, 512, 256, 128, 64, 32, 16, 8):
        if tm <= cap and m % tm == 0:
            return tm
    raise ValueError(m)


def _layer(x3, cos, sin, lw, lb, layer, past, final_norm_g, is_last):
    nseq, t, d = x3.shape
    m = nseq * t
    tm = _row_tile(m)
    x = x3.reshape(m, d)
    proj = _norm_matmul(x, lw["norm_mix"], lw["w_in"], _row_tile(m, 1024), 2 * PROJ_TILE)
    proj3 = proj.reshape(nseq, t, PROJ_WIDTH)

    qlat, qpe, ckv, kpe = _mla_prep(proj, cos, sin, lw["q_norm"], lw["kv_norm"], lw["w_uq"], lw["w_uk_t"],
                                    min(tm, 256), BF16 if past is None else F32)
    if past is None:
        o_a = _prompt_attn(qlat, qpe, ckv, kpe, lw["w_uv_t"], nseq, t, min(256, t))
        gdn_prev = hg_s0 = gdn_s0 = ffn_prev = None
        nb, r = 1, min(tm, t)
        r_ffn = min(_row_tile(m, 1024), t)
    else:
        n_pages = past["page_table"].shape[1]
        pps = 32 if n_pages % 32 == 0 else n_pages
        o_a = _decode_attn(qlat, qpe, ckv, kpe, past["cache_ckv"], past["cache_kpe_t"], past["page_table"],
                           lw["w_uv_t"], layer, t, pps, 4)
        gdn_prev = _pad_rows_front(past["gdn_conv"], SUBLANES)
        gdn_s0, hg_s0 = (past["gdn_S"], layer), (past["hgrn_S"], layer)
        nb, r = min(nseq, 512 // t), t
        r_ffn = r
    q, k, v, gb, gdn_cs = _gdn_prep(proj3, gdn_prev, lw["gdn_conv_w"], lw["gdn_a_log"], lw["gdn_dt_bias"], nb, r)
    o_b, gdn_s = _gdn(q, k, v, gb, proj, lw["gdn_norm"], gdn_s0, nseq, t)
    o_c, hg_s = _hgrn(proj, lb, lw["hg_norm"], hg_s0, nseq, t)
    x = _out_proj(x, o_a, o_b, o_c, lw["w_out"], layer, tm, 1024)

    tn = 512
    ffn_prev = None if past is None else _pad_rows_front(past["ffn_conv"], SUBLANES)
    x3, csg, csu = _ffn(x.reshape(nseq, t, d), ffn_prev, lw["norm_ffn"], lw["w_up"], lw["ffn_conv_w"],
                        lw["w_down"], layer, final_norm_g, nb, r_ffn, tn, is_last)
    ffn_cs = jnp.concatenate([csg, csu], axis=2)[:, SUBLANES - (FFN_CONV - 1):]
    new_state = (ckv.reshape(nseq, t, KV_RANK), kpe.reshape(nseq, t, ROPE_DIM), gdn_s,
                 gdn_cs[:, SUBLANES - (CONV_B - 1):], hg_s, ffn_cs)
    return x3, new_state


def kernel(x_prompt, x_sample, cache_ckv, cache_kpe, page_table, state_gdn_S, state_gdn_conv, state_hgrn_S, state_ffn_conv, norm_mix, w_in, mla_q_norm, mla_kv_norm, mla_w_uq, mla_w_uk, mla_w_uv, gdn_conv_w, gdn_a_log, gdn_dt_bias, gdn_norm, hgrn_lb, hgrn_norm, w_out, norm_ffn, ffn_w_up, ffn_conv_w, ffn_w_down, norm_final):
    depth = w_in.shape[0]
    bp, tp, _ = x_prompt.shape
    bs, ts, _ = x_sample.shape
    past_len = page_table.shape[1] * PAGE_SIZE
    cos_p, sin_p = _rope_tables(jnp.tile(jnp.arange(tp, dtype=jnp.int32), bp))
    cos_s, sin_s = _rope_tables(jnp.tile(past_len + jnp.arange(ts, dtype=jnp.int32), bs))
    p_lb = jax.nn.softmax(hgrn_lb.astype(F32), axis=0)
    lb_all = jnp.cumsum(p_lb, axis=0) - p_lb[0:1]
    cache_kpe_t = jnp.swapaxes(cache_kpe, 2, 3)

    w_out_b, w_up_b, w_down_b = w_out.astype(BF16), ffn_w_up.astype(BF16), ffn_w_down.astype(BF16)

    xp, xs = x_prompt, x_sample
    new_p, new_s = [], []
    for l in range(depth):
        lw = dict(norm_mix=norm_mix[l], w_in=_pack_w_in(w_in[l]), q_norm=mla_q_norm[l], kv_norm=mla_kv_norm[l],
                  w_uq=_pack_w_uq(mla_w_uq[l]),
                  w_uk_t=jnp.transpose(mla_w_uk[l], (1, 2, 0)).astype(BF16),
                  w_uv_t=jnp.transpose(mla_w_uv[l], (1, 0, 2)).astype(BF16),
                  gdn_conv_w=gdn_conv_w[l], gdn_a_log=gdn_a_log[l], gdn_dt_bias=gdn_dt_bias[l],
                  gdn_norm=gdn_norm[l], hg_norm=hgrn_norm[l], w_out=w_out_b,
                  norm_ffn=norm_ffn[l], w_up=w_up_b, ffn_conv_w=ffn_conv_w[l], w_down=w_down_b)
        last = l == depth - 1
        xp, st_p = _layer(xp, cos_p, sin_p, lw, lb_all[l], l, None, norm_final, last)
        past = dict(cache_ckv=cache_ckv, cache_kpe_t=cache_kpe_t, page_table=page_table,
                    gdn_S=state_gdn_S, gdn_conv=state_gdn_conv[l], hgrn_S=state_hgrn_S,
                    ffn_conv=state_ffn_conv[l])
        xs, st_s = _layer(xs, cos_s, sin_s, lw, lb_all[l], l, past, norm_final, last)
        new_p.append(st_p)
        new_s.append(st_s)
    outs_p = tuple(jnp.stack(a) for a in zip(*new_p))
    outs_s = tuple(jnp.stack(a) for a in zip(*new_s))
    return (xp, xs) + outs_p + outs_s
```

```python
import functools

import jax
import jax.numpy as jnp
import numpy as np
from jax import lax
from jax.experimental import pallas as pl
from jax.experimental.pallas import tpu as pltpu

F32 = jnp.float32
BF16 = jnp.bfloat16

D_MODEL = 2048
PAGE_SIZE = 128
HEAD_DIM = 128
H_A = 8
H_B = 4
H_C = 4
Q_RANK = 512
KV_RANK = 256
NOPE_DIM = 128
ROPE_DIM = 64
W_B = H_B * HEAD_DIM
W_C = H_C * HEAD_DIM
CONV_B = 4
D_FF = 5632
FFN_CONV = 3
ROPE_THETA = 10000.0
EPS = 1e-6
NEG = -1e30
LB_FLOOR = 1e-30

PROJ_TILE = 512
COL_QKV = 0
COL_CQ = 3
COL_MISC = 4
COL_ZB = 5
COL_QC = 6
COL_FC = 7
COL_IC = 8
COL_ZC = 9
PROJ_WIDTH = 10 * PROJ_TILE
MISC_KPE = 256
MISC_AB = 384

SUBLANES = 8
LANES = 128
CHUNK = 128
HG_CHUNK = 16
FFN_SUB = 1
DECODE_SLOTS = 4
VMEM_LIMIT = 56 * 1024 * 1024


def _params(*sem):
    return pltpu.CompilerParams(dimension_semantics=sem, vmem_limit_bytes=VMEM_LIMIT)


def _dot(a, b):
    return jnp.dot(a, b, preferred_element_type=F32)


def _dot_nt(a, b):
    return lax.dot_general(a, b, (((1,), (1,)), ((), ())), preferred_element_type=F32)


def _bdot(a, b):
    return _dot(a.astype(BF16), b.astype(BF16))


def _split2(a):
    hi = a.astype(BF16)
    lo = (a - hi.astype(F32)).astype(BF16)
    return hi, lo


def _dot3(a, b):
    ah, al = _split2(a)
    bh, bl = _split2(b)
    return _dot(ah, bh) + (_dot(ah, bl) + _dot(al, bh))


def _dot_exact_lhs(a01, b):
    a = a01.astype(BF16)
    b1 = b.astype(BF16)
    r1 = b - b1.astype(F32)
    b2 = r1.astype(BF16)
    b3 = (r1 - b2.astype(F32)).astype(BF16)
    return _dot(a, b1) + (_dot(a, b2) + _dot(a, b3))


def _sigmoid(x):
    return 1.0 / (1.0 + jnp.exp(-x))


def _silu(x):
    return x * _sigmoid(x)


def _softplus(x):
    return jnp.maximum(x, 0.0) + jnp.log1p(jnp.exp(-jnp.abs(x)))


def _rms(x, g):
    ms = jnp.mean(x * x, axis=-1, keepdims=True)
    return x * lax.rsqrt(ms + EPS) * g


def _swap_halves(x, lane):
    w = x.shape[-1]
    half = ROPE_DIM // 2
    fwd = pltpu.roll(x, w - half, axis=x.ndim - 1)
    bwd = pltpu.roll(x, half, axis=x.ndim - 1)
    return jnp.where((lane % ROPE_DIM) < half, fwd, bwd)


def _rope(x, cos, sin_signed):
    lane = lax.broadcasted_iota(jnp.int32, x.shape, x.ndim - 1)
    return x * cos + _swap_halves(x, lane) * sin_signed


def _norm_matmul_kernel(x_ref, g_ref, w_ref, o_ref, xn_ref):
    @pl.when(pl.program_id(1) == 0)
    def _():
        xn_ref[...] = _rms(x_ref[...], g_ref[...]).astype(BF16)

    o_ref[...] = _dot(xn_ref[...], w_ref[...])


def _norm_matmul(x, g, w, tm, tn):
    m, k = x.shape
    n = w.shape[1]
    return pl.pallas_call(
        _norm_matmul_kernel,
        grid=(m // tm, n // tn),
        in_specs=[pl.BlockSpec((tm, k), lambda i, j: (i, 0)),
                  pl.BlockSpec((1, k), lambda i, j: (0, 0)),
                  pl.BlockSpec((k, tn), lambda i, j: (0, j))],
        out_specs=pl.BlockSpec((tm, tn), lambda i, j: (i, j)),
        out_shape=jax.ShapeDtypeStruct((m, n), F32),
        scratch_shapes=[pltpu.VMEM((tm, k), BF16)],
        compiler_params=_params("parallel", "arbitrary"),
        name="in_proj",
    )(x, g.reshape(1, k), w)


def _mla_prep_kernel(cq_ref, misc_ref, cos_ref, sin_ref, qn_ref, kvn_ref, wuq_ref, wuk_ref,
                     qlat_ref, qpe_ref, ckv_ref, kpe_ref, *, scale):
    cos = cos_ref[...]
    sin = sin_ref[...]
    misc = misc_ref[...]
    ckv_ref[...] = _rms(misc[:, :KV_RANK], kvn_ref[...])
    kpe = _rope(misc[:, MISC_KPE:MISC_KPE + LANES], cos, sin)
    kpe_ref[...] = kpe[:, :ROPE_DIM]

    cqn = _rms(cq_ref[...], qn_ref[...]).astype(BF16)
    q = _dot(cqn, wuq_ref[...])
    n_nope = H_A * NOPE_DIM
    reps = (H_A * ROPE_DIM) // LANES
    q_pe = _rope(q[:, n_nope:], jnp.tile(cos, (1, reps)), jnp.tile(sin, (1, reps))) * scale
    for h in range(H_A):
        qn = q[:, h * NOPE_DIM:(h + 1) * NOPE_DIM].astype(BF16)
        qlat_ref[h] = (_dot(qn, wuk_ref[h]) * scale).astype(qlat_ref.dtype)
        qpe_ref[h] = q_pe[:, h * ROPE_DIM:(h + 1) * ROPE_DIM].astype(qpe_ref.dtype)


def _mla_prep(proj, cos, sin, q_norm, kv_norm, w_uq, w_uk_t, tm, q_dtype):
    m = proj.shape[0]
    scale = float((NOPE_DIM + ROPE_DIM) ** -0.5)
    return pl.pallas_call(
        functools.partial(_mla_prep_kernel, scale=scale),
        grid=(m // tm,),
        in_specs=[pl.BlockSpec((tm, PROJ_TILE), lambda i: (i, COL_CQ)),
                  pl.BlockSpec((tm, PROJ_TILE), lambda i: (i, COL_MISC)),
                  pl.BlockSpec((tm, LANES), lambda i: (i, 0)),
                  pl.BlockSpec((tm, LANES), lambda i: (i, 0)),
                  pl.BlockSpec((1, Q_RANK), lambda i: (0, 0)),
                  pl.BlockSpec((1, KV_RANK), lambda i: (0, 0)),
                  pl.BlockSpec(w_uq.shape, lambda i: (0, 0)),
                  pl.BlockSpec(w_uk_t.shape, lambda i: (0, 0, 0))],
        out_specs=[pl.BlockSpec((H_A, tm, KV_RANK), lambda i: (0, i, 0)),
                   pl.BlockSpec((H_A, tm, ROPE_DIM), lambda i: (0, i, 0)),
                   pl.BlockSpec((tm, KV_RANK), lambda i: (i, 0)),
                   pl.BlockSpec((tm, ROPE_DIM), lambda i: (i, 0))],
        out_shape=[jax.ShapeDtypeStruct((H_A, m, KV_RANK), q_dtype),
                   jax.ShapeDtypeStruct((H_A, m, ROPE_DIM), q_dtype),
                   jax.ShapeDtypeStruct((m, KV_RANK), F32),
                   jax.ShapeDtypeStruct((m, ROPE_DIM), F32)],
        compiler_params=_params("parallel"),
        name="mla_prep",
    )(proj, proj, cos, sin, q_norm.reshape(1, -1), kv_norm.reshape(1, -1), w_uq, w_uk_t)


def _lanes(x, n):
    return x if n == LANES else jnp.concatenate([x] * (n // LANES), axis=1)


def _prompt_attn_kernel(qi_tab, ki_tab, qlat_ref, qpe_ref, ckv_ref, kpe_ref, wuv_ref, o_ref,
                        m_ref, l_ref, acc_ref):
    step = pl.program_id(1)
    qi = qi_tab[step]
    ki = ki_tab[step]
    tq, tk = qlat_ref.shape[1], ckv_ref.shape[0]

    @pl.when(ki == 0)
    def _():
        m_ref[...] = jnp.full(m_ref.shape, NEG, F32)
        l_ref[...] = jnp.zeros(l_ref.shape, F32)
        acc_ref[...] = jnp.zeros(acc_ref.shape, F32)

    def update(masked):
        kc = ckv_ref[...].astype(BF16)
        kp = kpe_ref[...].astype(BF16)
        if masked:
            keep = (lax.broadcasted_iota(jnp.int32, (tq, tk), 1)
                    <= lax.broadcasted_iota(jnp.int32, (tq, tk), 0))
        scores = lambda h: _dot_nt(qlat_ref[h], kc) + _dot_nt(qpe_ref[h], kp)

        def accumulate(h, p, alpha):
            acc_ref[h] = _lanes(alpha, KV_RANK) * acc_ref[h] + _dot(p, kc)

        s_next = scores(0)
        pending = None
        for h in range(H_A):
            s = s_next
            if h + 1 < H_A:
                s_next = scores(h + 1)
            if masked:
                s = jnp.where(keep, s, NEG)
            m_prev = m_ref[h]
            m_new = jnp.maximum(m_prev, jnp.max(s, axis=-1, keepdims=True))
            alpha = jnp.exp(m_prev - m_new)
            p = jnp.exp(s - _lanes(m_new, tk))
            l_ref[h] = alpha * l_ref[h] + jnp.sum(p, axis=-1, keepdims=True)
            m_ref[h] = m_new
            if pending is not None:
                accumulate(*pending)
            pending = (h, p.astype(BF16), alpha)
        accumulate(*pending)

    @pl.when(ki < qi)
    def _():
        update(False)

    @pl.when(ki == qi)
    def _():
        update(True)
        for h in range(H_A):
            o = (acc_ref[h] * _lanes(1.0 / l_ref[h], KV_RANK)).astype(BF16)
            o_ref[:, h * HEAD_DIM:(h + 1) * HEAD_DIM] = _dot(o, wuv_ref[h]).astype(o_ref.dtype)


def _prompt_attn(qlat, qpe, ckv, kpe, w_uv_t, nseq, t, tq):
    nq = t // tq
    pairs = [(qi, ki) for qi in range(nq) for ki in range(qi + 1)]
    qi_tab = jnp.asarray([p[0] for p in pairs], jnp.int32)
    ki_tab = jnp.asarray([p[1] for p in pairs], jnp.int32)
    q_map = lambda b, s, qt, kt: (0, b * nq + qt[s], 0)
    kv_map = lambda b, s, qt, kt: (b * nq + kt[s], 0)
    grid_spec = pltpu.PrefetchScalarGridSpec(
        num_scalar_prefetch=2,
        grid=(nseq, len(pairs)),
        in_specs=[pl.BlockSpec((H_A, tq, KV_RANK), q_map),
                  pl.BlockSpec((H_A, tq, ROPE_DIM), q_map),
                  pl.BlockSpec((tq, KV_RANK), kv_map),
                  pl.BlockSpec((tq, ROPE_DIM), kv_map),
                  pl.BlockSpec(w_uv_t.shape, lambda b, s, qt, kt: (0, 0, 0))],
        out_specs=pl.BlockSpec((tq, H_A * HEAD_DIM), lambda b, s, qt, kt: (b * nq + qt[s], 0)),
        scratch_shapes=[pltpu.VMEM((H_A, tq, LANES), F32), pltpu.VMEM((H_A, tq, LANES), F32),
                        pltpu.VMEM((H_A, tq, KV_RANK), F32)],
    )
    return pl.pallas_call(
        _prompt_attn_kernel,
        grid_spec=grid_spec,
        out_shape=jax.ShapeDtypeStruct((nseq * t, H_A * HEAD_DIM), BF16),
        compiler_params=_params("parallel", "arbitrary"),
        name="prompt_attn",
    )(qi_tab, ki_tab, qlat, qpe, ckv, kpe, w_uv_t)


def _decode_attn_kernel(pt_ref, qlat_ref, qpe_ref, nckv_ref, nkpe_ref, wuv_ref, ckv_hbm, kpe_t_hbm,
                        o_ref, kbuf, pbuf, sem, m_ref, l_ref, acc_ref, *, layer, pps, ts, n_chains):
    b = pl.program_id(0)
    j = pl.program_id(1)
    nj = pl.num_programs(1)
    step = b * nj + j
    last_step = pl.num_programs(0) * nj - 1
    slot = step % DECODE_SLOTS
    rows = H_A * ts

    def page_copies(bb, jj, sl, lookup=True):
        out = []
        for i in range(pps):
            page = pt_ref[bb, jj * pps + i] if lookup else 0
            out.append(pltpu.make_async_copy(ckv_hbm.at[layer, page], kbuf.at[sl, i], sem.at[sl, 0]))
            out.append(pltpu.make_async_copy(kpe_t_hbm.at[layer, page], pbuf.at[sl, i], sem.at[sl, 1]))
        return out

    def fetch_slot(k):
        return (k % DECODE_SLOTS) if isinstance(k, int) else lax.rem(k, DECODE_SLOTS)

    @pl.when(step == 0)
    def _():
        for ahead in range(DECODE_SLOTS - 1):
            tgt = jnp.minimum(ahead, last_step)
            for cp in page_copies(tgt // nj, tgt % nj, fetch_slot(ahead)):
                cp.start()

    for cp in page_copies(b, j, slot, lookup=False):
        cp.wait()
    ahead_slot = fetch_slot(step + DECODE_SLOTS - 1)
    nxt = jnp.minimum(step + DECODE_SLOTS - 1, last_step)
    prefetch = page_copies(nxt // nj, nxt % nj, ahead_slot)

    ql = qlat_ref[...].reshape(rows, KV_RANK).astype(BF16)
    qp = qpe_ref[...].reshape(rows, ROPE_DIM).astype(BF16)

    @pl.when(j == 0)
    def _():
        kc = nckv_ref[...].astype(BF16)
        kp = nkpe_ref[...].astype(BF16)
        s = _dot_nt(ql, kc) + _dot_nt(qp, kp)
        nb = kc.shape[0] // ts
        t_row = lax.broadcasted_iota(jnp.int32, s.shape, 0) % ts
        col = lax.broadcasted_iota(jnp.int32, s.shape, 1)
        ok = (col // ts == b % nb) & (col % ts <= t_row)
        s = jnp.where(ok, s, NEG)
        m = jnp.max(s, axis=-1, keepdims=True)
        p = jnp.where(ok, jnp.exp(s - m), 0.0)
        m_ref[0] = m
        l_ref[0] = jnp.sum(p, axis=-1, keepdims=True)
        acc_ref[0] = _dot(p.astype(BF16), kc)
        for c in range(1, n_chains):
            m_ref[c] = jnp.full((rows, 1), NEG, F32)
            l_ref[c] = jnp.zeros((rows, 1), F32)
            acc_ref[c] = jnp.zeros((rows, KV_RANK), F32)

    per = pps // n_chains
    def chain_scores(c):
        kcs, scores = [], []
        for i in range(c * per, (c + 1) * per):
            prefetch[2 * i].start()
            prefetch[2 * i + 1].start(priority=1)
            kcs.append(kbuf[slot, i].astype(BF16))
            scores.append(_dot_nt(ql, kcs[-1]) + _dot(qp, pbuf[slot, i].astype(BF16)))
        return kcs, jnp.concatenate(scores, axis=1)

    nxt_chain = chain_scores(0)
    for c in range(n_chains):
        kcs, s = nxt_chain
        if c + 1 < n_chains:
            nxt_chain = chain_scores(c + 1)
        m_prev = m_ref[c]
        m_new = jnp.maximum(m_prev, jnp.max(s, axis=-1, keepdims=True))
        alpha = jnp.exp(m_prev - m_new)
        p = jnp.exp(s - m_new).astype(BF16)
        l_ref[c] = alpha * l_ref[c] + jnp.sum(p.astype(F32), axis=-1, keepdims=True)
        pv = _dot(p[:, :PAGE_SIZE], kcs[0])
        for i in range(1, per):
            pv += _dot(p[:, i * PAGE_SIZE:(i + 1) * PAGE_SIZE], kcs[i])
        acc_ref[c] = alpha * acc_ref[c] + pv
        m_ref[c] = m_new

    @pl.when(step == last_step)
    def _():
        for extra in range(1, DECODE_SLOTS):
            for cp in page_copies(b, j, fetch_slot(step + extra), lookup=False):
                cp.wait()

    @pl.when(j == nj - 1)
    def _():
        m = m_ref[0]
        for c in range(1, n_chains):
            m = jnp.maximum(m, m_ref[c])
        l = jnp.zeros((rows, 1), F32)
        acc = jnp.zeros((rows, KV_RANK), F32)
        for c in range(n_chains):
            w = jnp.exp(m_ref[c] - m)
            l += w * l_ref[c]
            acc += w * acc_ref[c]
        o = acc * (1.0 / l)
        for h in range(H_A):
            oh = o[h * ts:(h + 1) * ts].astype(BF16)
            o_ref[:, h * HEAD_DIM:(h + 1) * HEAD_DIM] = _dot(oh, wuv_ref[h]).astype(o_ref.dtype)


def _decode_attn(qlat, qpe, ckv_new, kpe_new, cache_ckv, cache_kpe_t, page_table, w_uv_t, layer, ts, pps, n_chains):
    nseq, n_pages = page_table.shape
    rows = H_A * ts
    new_rows = min(LANES, nseq * ts)
    nb = new_rows // ts

    grid_spec = pltpu.PrefetchScalarGridSpec(
        num_scalar_prefetch=1,
        grid=(nseq, n_pages // pps),
        in_specs=[pl.BlockSpec((H_A, ts, KV_RANK), lambda b, j, pt: (0, b, 0)),
                  pl.BlockSpec((H_A, ts, ROPE_DIM), lambda b, j, pt: (0, b, 0)),
                  pl.BlockSpec((new_rows, KV_RANK), lambda b, j, pt: (b // nb, 0)),
                  pl.BlockSpec((new_rows, ROPE_DIM), lambda b, j, pt: (b // nb, 0)),
                  pl.BlockSpec(w_uv_t.shape, lambda b, j, pt: (0, 0, 0)),
                  pl.BlockSpec(memory_space=pl.ANY),
                  pl.BlockSpec(memory_space=pl.ANY)],
        out_specs=pl.BlockSpec((ts, H_A * HEAD_DIM), lambda b, j, pt: (b, 0)),
        scratch_shapes=[pltpu.VMEM((DECODE_SLOTS, pps, PAGE_SIZE, KV_RANK), F32),
                        pltpu.VMEM((DECODE_SLOTS, pps, ROPE_DIM, PAGE_SIZE), F32),
                        pltpu.SemaphoreType.DMA((DECODE_SLOTS, 2)),
                        pltpu.VMEM((n_chains, rows, 1), F32), pltpu.VMEM((n_chains, rows, 1), F32),
                        pltpu.VMEM((n_chains, rows, KV_RANK), F32)],
    )
    return pl.pallas_call(
        functools.partial(_decode_attn_kernel, layer=layer, pps=pps, ts=ts, n_chains=n_chains),
        grid_spec=grid_spec,
        out_shape=jax.ShapeDtypeStruct((nseq * ts, H_A * HEAD_DIM), F32),
        compiler_params=_params("arbitrary", "arbitrary"),
        name="decode_attn",
    )(page_table, qlat, qpe, ckv_new, kpe_new, w_uv_t, cache_ckv, cache_kpe_t)


def _shifted(x, prev, j):
    if j == 0:
        return x
    rx = pltpu.roll(x, j, axis=1)
    rp = pltpu.roll(prev, j, axis=1)
    row = lax.broadcasted_iota(jnp.int32, prev.shape, 1)
    head = jnp.where(row < j, rp, rx[:, :SUBLANES])
    if x.shape[1] == SUBLANES:
        return head
    return jnp.concatenate([head, rx[:, SUBLANES:]], axis=1)


def _dwconv(x, prev, w):
    k = w.shape[0]
    y = x * w[k - 1:k]
    for j in range(k - 1):
        y = y + _shifted(x, prev, k - 1 - j) * w[j:j + 1]
    return y


def _gdn_prep_kernel(x_ref, prev_ref, misc_ref, cw_ref, alog_ref, dtb_ref,
                     q_ref, k_ref, v_ref, gb_ref, cs_ref, *, seq_start_every):
    x = x_ref[...]
    nb, r, c = x.shape
    prev = prev_ref[...]
    if seq_start_every:
        first = (pl.program_id(0) % seq_start_every) == 0
        prev = jnp.where(first, 0.0, prev)
    y = _silu(_dwconv(x, prev, cw_ref[...])).reshape(nb * r, c)
    cs_ref[...] = x[:, r - SUBLANES:, :]
    for h in range(H_B):
        sl = slice(h * HEAD_DIM, (h + 1) * HEAD_DIM)
        qh = y[:, h * HEAD_DIM:(h + 1) * HEAD_DIM]
        kh = y[:, W_B + h * HEAD_DIM:W_B + (h + 1) * HEAD_DIM]
        q_ref[:, sl] = qh * lax.rsqrt(jnp.sum(qh * qh, axis=-1, keepdims=True) + EPS) * (HEAD_DIM ** -0.5)
        k_ref[:, sl] = kh * lax.rsqrt(jnp.sum(kh * kh, axis=-1, keepdims=True) + EPS)
    v_ref[...] = y[:, 2 * W_B:]
    ab = misc_ref[...].reshape(nb * r, PROJ_TILE)[:, MISC_AB:MISC_AB + LANES]
    g = -jnp.exp(alog_ref[...]) * _softplus(ab + dtb_ref[...])
    lane = lax.broadcasted_iota(jnp.int32, ab.shape, 1)
    gb_ref[...] = jnp.where(lane < H_B, g, _sigmoid(ab))


def _gdn_prep(proj3, conv_prev, conv_w, a_log, dt_bias, nb, r):
    nseq, t, _ = proj3.shape
    c = 3 * W_B
    steps_per_seq = t // r
    m = nseq * t
    rows = nb * r
    if conv_prev is None:
        assert nb == 1
        rb = r // SUBLANES
        prev_arr = proj3
        prev_spec = pl.BlockSpec((1, SUBLANES, c), lambda i: (i // steps_per_seq,
                                                               jnp.maximum((i % steps_per_seq) * rb - 1, 0), COL_QKV))
        x_map = lambda i: (i // steps_per_seq, i % steps_per_seq, COL_QKV)
        misc_map = lambda i: (i // steps_per_seq, i % steps_per_seq, COL_MISC)
        grid = (nseq * steps_per_seq,)
        seq_start_every = steps_per_seq
    else:
        assert r == t
        prev_arr = conv_prev
        prev_spec = pl.BlockSpec((nb, SUBLANES, c), lambda i: (i, 0, 0))
        x_map = lambda i: (i, 0, COL_QKV)
        misc_map = lambda i: (i, 0, COL_MISC)
        grid = (nseq // nb,)
        seq_start_every = 0
    alog = jnp.zeros((1, LANES), F32).at[0, :H_B].set(a_log.astype(F32))
    dtb = jnp.zeros((1, LANES), F32).at[0, :H_B].set(dt_bias.astype(F32))
    row_spec = lambda w: pl.BlockSpec((rows, w), lambda i: (i, 0))
    q, k, v, gb, cs = pl.pallas_call(
        functools.partial(_gdn_prep_kernel, seq_start_every=seq_start_every),
        grid=grid,
        in_specs=[pl.BlockSpec((nb, r, c), x_map), prev_spec,
                  pl.BlockSpec((nb, r, PROJ_TILE), misc_map),
                  pl.BlockSpec((CONV_B, c), lambda i: (0, 0)),
                  pl.BlockSpec((1, LANES), lambda i: (0, 0)),
                  pl.BlockSpec((1, LANES), lambda i: (0, 0))],
        out_specs=[row_spec(W_B), row_spec(W_B), row_spec(W_B), row_spec(LANES),
                   pl.BlockSpec((nb, SUBLANES, c), lambda i: (i, 0, 0))],
        out_shape=[jax.ShapeDtypeStruct((m, W_B), F32), jax.ShapeDtypeStruct((m, W_B), F32),
                   jax.ShapeDtypeStruct((m, W_B), F32), jax.ShapeDtypeStruct((m, LANES), F32),
                   jax.ShapeDtypeStruct((grid[0] * nb, SUBLANES, c), F32)],
        compiler_params=_params("arbitrary"),
        name="gdn_prep",
    )(proj3, prev_arr, proj3, conv_w, alog, dtb)
    cs = cs.reshape(nseq, -1, SUBLANES, c)[:, -1]
    return q, k, v, gb, cs


def _gdn_kernel(q_ref, k_ref, v_ref, gb_ref, z_ref, gn_ref, *rest, nseq_blk, r, zero_init):
    if zero_init:
        o_ref, sout_ref, s_scr, wS_scr, qS_scr, vn_scr = rest
    else:
        s0_ref, o_ref, sout_ref, s_scr, wS_scr, qS_scr, vn_scr = rest
    c = nseq_blk * r
    ci = pl.program_id(1)

    @pl.when(ci == 0)
    def _():
        if zero_init:
            s_scr[...] = jnp.zeros(s_scr.shape, F32)
        else:
            s_scr[...] = s0_ref[...]

    row = lax.broadcasted_iota(jnp.int32, (c, c), 0)
    col = lax.broadcasted_iota(jnp.int32, (c, c), 1)
    same = (row // r) == (col // r)
    causal = same & (col <= row)
    strict = same & (col < row)
    eye = (row == col).astype(F32)

    gb = gb_ref[...]
    gcs = _dot_exact_lhs(causal.astype(F32), gb)
    gtot = _dot_exact_lhs(same.astype(F32), gb)
    gcs_t = gcs.T
    n_sq = max(int(np.ceil(np.log2(r))) - 1, 0)

    heads = range(H_B)
    sls = [slice(h * HEAD_DIM, (h + 1) * HEAD_DIM) for h in heads]
    qs = [q_ref[:, sl] for sl in sls]
    ks = [k_ref[:, sl] for sl in sls]
    vs = [v_ref[:, sl] for sl in sls]
    gcl = [gcs[:, h:h + 1] for h in heads]
    gll = [gtot[:, h:h + 1] for h in heads]
    betas = [gb[:, H_B + h:H_B + h + 1] for h in heads]
    decays = [jnp.exp(jnp.where(causal, gcl[h] - gcs_t[h:h + 1, :], NEG)) for h in heads]
    kbs = [ks[h] * betas[h] for h in heads]
    mms = [jnp.where(strict, _dot_nt(kbs[h].astype(BF16), ks[h].astype(BF16)) * decays[h], 0.0) for h in heads]
    tinvs = [eye - mm for mm in mms]
    pws = list(mms)
    for _ in range(n_sq):
        pws = [_dot3(pw, pw) for pw in pws]
        tinvs = [tinv + _dot3(tinv, pw) for tinv, pw in zip(tinvs, pws)]
    egcs = [jnp.exp(gc) for gc in gcl]
    us = [_dot3(tinvs[h], vs[h] * betas[h]) for h in heads]
    ws = [_dot3(tinvs[h], kbs[h] * egcs[h]).astype(BF16) for h in heads]
    qks = [(_dot_nt(qs[h].astype(BF16), ks[h].astype(BF16)) * decays[h]).astype(BF16) for h in heads]
    q_decs = [(qs[h] * egcs[h]).astype(BF16) for h in heads]
    k_dec_ts = [(ks[h] * jnp.exp(gll[h] - gcl[h])).T for h in heads]

    if nseq_blk == 1:
        sbs = [s_scr[0, h].astype(BF16) for h in heads]
        v_news = [us[h] - _dot(ws[h], sbs[h]) for h in heads]
        vnb = [v.astype(BF16) for v in v_news]
        os_ = [_dot(q_decs[h], sbs[h]) + _dot(qks[h], vnb[h]) for h in heads]
        for h in heads:
            s_scr[0, h] = (s_scr[0, h] * jnp.exp(gll[h][c - 1:c, :])
                           + _dot(k_dec_ts[h].astype(BF16), vnb[h]))
        for h in heads:
            on = _rms(os_[h], gn_ref[...])
            o_ref[:, sls[h]] = (on * _silu(z_ref[:, sls[h]])).astype(o_ref.dtype)

    else:
        rowc = lax.broadcasted_iota(jnp.int32, (c, HEAD_DIM), 0) // r
        colc = lax.broadcasted_iota(jnp.int32, (HEAD_DIM, c), 1) // r
        wS_scr[...] = jnp.zeros(wS_scr.shape, F32)
        qS_scr[...] = jnp.zeros(qS_scr.shape, F32)

        def read_state(p, _):
            mine = rowc == p
            for h in heads:
                sb = s_scr[p, h].astype(BF16)
                wS_scr[h] += _dot(jnp.where(mine, ws[h], jnp.zeros_like(ws[h])), sb)
                qS_scr[h] += _dot(jnp.where(mine, q_decs[h], jnp.zeros_like(q_decs[h])), sb)
            return 0

        lax.fori_loop(0, nseq_blk, read_state, 0, unroll=2)
        for h in heads:
            vn_scr[h] = (us[h] - wS_scr[h]).astype(BF16)
            on = _rms(qS_scr[h] + _dot(qks[h], vn_scr[h]), gn_ref[...])
            o_ref[:, sls[h]] = (on * _silu(z_ref[:, sls[h]])).astype(o_ref.dtype)
        e_gls = [jnp.exp(gl) for gl in gll]

        def write_state(p, _):
            for h in heads:
                kd = jnp.where(colc == p, k_dec_ts[h], 0.0).astype(BF16)
                g_last = jnp.max(jnp.where(rowc[:, :1] == p, e_gls[h], 0.0), axis=0, keepdims=True)
                s_scr[p, h] = s_scr[p, h] * g_last + _dot(kd, vn_scr[h])
            return 0

        lax.fori_loop(0, nseq_blk, write_state, 0, unroll=2)

    @pl.when(ci == pl.num_programs(1) - 1)
    def _():
        sout_ref[...] = s_scr[...]


def _gdn(q, k, v, gb, proj, gdn_norm, s0, nseq, t):
    m = nseq * t
    if t >= CHUNK:
        nseq_blk, r = 1, CHUNK
    else:
        nseq_blk, r = CHUNK // t, t
    n_chunks = t // r
    n_blocks = nseq // nseq_blk
    row_map = lambda i, ci: (i * n_chunks + ci, 0)
    state_spec = pl.BlockSpec((nseq_blk, H_B, HEAD_DIM, HEAD_DIM), lambda i, ci: (i, 0, 0, 0))
    in_specs = [pl.BlockSpec((CHUNK, W_B), row_map), pl.BlockSpec((CHUNK, W_B), row_map),
                pl.BlockSpec((CHUNK, W_B), row_map), pl.BlockSpec((CHUNK, LANES), row_map),
                pl.BlockSpec((CHUNK, W_B), lambda i, ci: (i * n_chunks + ci, COL_ZB)),
                pl.BlockSpec((1, HEAD_DIM), lambda i, ci: (0, 0))]
    args = [q, k, v, gb, proj, gdn_norm.reshape(1, HEAD_DIM)]
    if s0 is not None:
        s0_all, layer = s0
        in_specs.append(pl.BlockSpec((None, nseq_blk, H_B, HEAD_DIM, HEAD_DIM), lambda i, ci: (layer, i, 0, 0, 0)))
        args.append(s0_all)
    return pl.pallas_call(
        functools.partial(_gdn_kernel, nseq_blk=nseq_blk, r=r, zero_init=s0 is None),
        grid=(n_blocks, n_chunks),
        in_specs=in_specs,
        out_specs=[pl.BlockSpec((CHUNK, W_B), row_map), state_spec],
        out_shape=[jax.ShapeDtypeStruct((m, W_B), BF16),
                   jax.ShapeDtypeStruct((nseq, H_B, HEAD_DIM, HEAD_DIM), F32)],
        scratch_shapes=[pltpu.VMEM((nseq_blk, H_B, HEAD_DIM, HEAD_DIM), F32),
                        pltpu.VMEM((H_B, CHUNK, HEAD_DIM), F32), pltpu.VMEM((H_B, CHUNK, HEAD_DIM), F32),
                        pltpu.VMEM((H_B, CHUNK, HEAD_DIM), BF16)],
        compiler_params=_params("parallel", "arbitrary"),
        name="gdn",
    )(*args)


def _hgrn_kernel(qc_ref, fc_ref, ic_ref, zc_ref, lb_ref, hn_ref, *rest, cs, per_chunk_state, zero_init):
    if zero_init:
        o_ref, sout_ref, st_scr = rest
    else:
        s0_ref, o_ref, sout_ref, st_scr = rest
    c = qc_ref.shape[0]
    n_ch = c // cs
    step = pl.program_id(1)

    if not per_chunk_state:
        @pl.when(step == 0)
        def _():
            st_scr[...] = jnp.zeros(st_scr.shape, F32)

    lb = lb_ref[...]
    fr = fc_ref[...]
    log_lb = jnp.log(jnp.maximum(lb, LB_FLOOR))
    a = log_lb
    b = jnp.log1p(-lb) - _softplus(-fr)
    mx = jnp.maximum(a, b)
    logf = mx + jnp.log1p(jnp.exp(-jnp.abs(a - b)))
    kk = (1.0 - lb) * _sigmoid(-fr)
    qq = _silu(qc_ref[...])

    row = lax.broadcasted_iota(jnp.int32, (c, c), 0)
    col = lax.broadcasted_iota(jnp.int32, (c, c), 1)
    same = (row // cs) == (col // cs)
    bc_all = _dot_exact_lhs((same & (col <= row)).astype(F32), logf)
    btot_all = _dot_exact_lhs(same.astype(F32), logf)
    t_idx = lax.broadcasted_iota(jnp.int32, (n_ch, cs, HEAD_DIM), 1)
    rowc = lax.broadcasted_iota(jnp.int32, (c, HEAD_DIM), 0) // cs

    def head(h):
        sl = slice(h * HEAD_DIM, (h + 1) * HEAD_DIM)
        return qq[:, sl], kk[:, sl], ic_ref[:, sl], bc_all[:, sl], btot_all[:, sl]

    def within_chunks(h):
        q, k, v, bc, _ = head(h)
        q3, k3, v3, bc3 = (x.reshape(n_ch, cs, HEAD_DIM) for x in (q, k, v, bc))
        o3 = jnp.zeros((n_ch, cs, HEAD_DIM), F32)
        for s in range(cs):
            dec = jnp.exp(jnp.where(t_idx >= s, bc3 - bc3[:, s:s + 1, :], NEG))
            a_col = jnp.sum(q3 * k3[:, s:s + 1, :] * dec, axis=-1, keepdims=True)
            o3 = o3 + a_col * v3[:, s:s + 1, :]
        return o3.reshape(c, HEAD_DIM)

    def across_chunks(h):
        q, k, v, bc, btot = head(h)
        q_dec = (q * jnp.exp(bc)).astype(BF16)
        k_dec = (k * jnp.exp(btot - bc)).astype(BF16)
        v_t = v.T.astype(BF16)
        e_tot = jnp.exp(btot)
        if per_chunk_state:
            def chunk_step(n, o_acc):
                mine = rowc == n
                st = s0_ref[n, h].T
                o_acc = o_acc + _dot_nt(jnp.where(mine, q_dec, jnp.zeros_like(q_dec)), st.astype(BF16))
                dec_n = jnp.max(jnp.where(mine, e_tot, 0.0), axis=0, keepdims=True)
                st = st * dec_n + _dot(v_t, jnp.where(mine, k_dec, jnp.zeros_like(k_dec)))
                sout_ref[n, h] = st.T
                return o_acc

            return lax.fori_loop(0, n_ch, chunk_step, jnp.zeros((c, HEAD_DIM), F32), unroll=4)
        st = st_scr[h]
        pieces = []
        for n in range(n_ch):
            rs = slice(n * cs, (n + 1) * cs)
            pieces.append(_dot_nt(q_dec[rs], st.astype(BF16)))
            kv = _dot(v_t, jnp.where(rowc == n, k_dec, jnp.zeros_like(k_dec)))
            st = st * e_tot[n * cs:n * cs + 1] + kv
        st_scr[h] = st
        return jnp.concatenate(pieces, axis=0)

    o_state = across_chunks(0)
    for h in range(H_C):
        sl = slice(h * HEAD_DIM, (h + 1) * HEAD_DIM)
        o = o_state
        if h + 1 < H_C:
            o_state = across_chunks(h + 1)
        on = _rms(o + within_chunks(h), hn_ref[...])
        o_ref[:, sl] = (on * _silu(zc_ref[:, sl])).astype(o_ref.dtype)

    if not per_chunk_state:
        @pl.when(step == pl.num_programs(1) - 1)
        def _():
            for h in range(H_C):
                sout_ref[0, h] = st_scr[h].T


def _hgrn(proj, lb, hg_norm, s0, nseq, t):
    m = nseq * t
    cs = min(HG_CHUNK, t)
    per_chunk_state = t <= HG_CHUNK
    if per_chunk_state:
        assert s0 is not None
        seq_blk = CHUNK // t
        grid = (nseq // seq_blk, 1)
        state_spec = pl.BlockSpec((seq_blk, H_C, HEAD_DIM, HEAD_DIM), lambda i, j: (i, 0, 0, 0))
        steps = 1
    else:
        assert s0 is None
        steps = t // CHUNK
        grid = (nseq, steps)
        state_spec = pl.BlockSpec((1, H_C, HEAD_DIM, HEAD_DIM), lambda i, j: (i, 0, 0, 0))
    col = lambda tile: pl.BlockSpec((CHUNK, W_C), lambda i, j: (i * steps + j, tile))
    in_specs = [col(COL_QC), col(COL_FC), col(COL_IC), col(COL_ZC),
                pl.BlockSpec((1, W_C), lambda i, j: (0, 0)),
                pl.BlockSpec((1, HEAD_DIM), lambda i, j: (0, 0))]
    args = [proj, proj, proj, proj, lb.reshape(1, W_C), hg_norm.reshape(1, HEAD_DIM)]
    if s0 is not None:
        s0_all, layer = s0
        in_specs.append(pl.BlockSpec((None,) + state_spec.block_shape, lambda i, j: (layer, i, 0, 0, 0)))
        args.append(s0_all)
    return pl.pallas_call(
        functools.partial(_hgrn_kernel, cs=cs, per_chunk_state=per_chunk_state, zero_init=s0 is None),
        grid=grid,
        in_specs=in_specs,
        out_specs=[pl.BlockSpec((CHUNK, W_C), lambda i, j: (i * steps + j, 0)), state_spec],
        out_shape=[jax.ShapeDtypeStruct((m, W_C), BF16),
                   jax.ShapeDtypeStruct((nseq, H_C, HEAD_DIM, HEAD_DIM), F32)],
        scratch_shapes=[pltpu.VMEM((H_C, HEAD_DIM, HEAD_DIM), F32)],
        compiler_params=_params("parallel", "arbitrary"),
        name="hgrn",
    )(*args)


def _out_proj_kernel(x_ref, oa_ref, ob_ref, oc_ref, w_ref, o_ref):
    wa = H_A * HEAD_DIM
    acc = _dot(oa_ref[...].astype(BF16), w_ref[:wa])
    acc += _dot(ob_ref[...], w_ref[wa:wa + W_B])
    acc += _dot(oc_ref[...], w_ref[wa + W_B:])
    o_ref[...] = x_ref[...] + acc


def _out_proj(x, oa, ob, oc, w, layer, tm, tn):
    m, d = x.shape
    kdim = w.shape[1]
    return pl.pallas_call(
        _out_proj_kernel,
        grid=(m // tm, d // tn),
        in_specs=[pl.BlockSpec((tm, tn), lambda i, j: (i, j)),
                  pl.BlockSpec((tm, oa.shape[1]), lambda i, j: (i, 0)),
                  pl.BlockSpec((tm, ob.shape[1]), lambda i, j: (i, 0)),
                  pl.BlockSpec((tm, oc.shape[1]), lambda i, j: (i, 0)),
                  pl.BlockSpec((None, kdim, tn), lambda i, j: (layer, 0, j))],
        out_specs=pl.BlockSpec((tm, tn), lambda i, j: (i, j)),
        out_shape=jax.ShapeDtypeStruct((m, d), F32),
        compiler_params=_params("parallel", "arbitrary"),
        name="out_proj",
    )(x, oa, ob, oc, w)


def _ffn_kernel(x_ref, prev_ref, prevu_ref, g_ref, wg_ref, wu_ref, cwg_ref, cwu_ref, wd_ref, gf_ref,
                o_ref, csg_ref, csu_ref, xn_ref, pn_ref, *, seq_start_every, prev_is_state, final_norm):
    j = pl.program_id(1)
    nb, r, d = x_ref.shape
    rows = nb * r

    @pl.when(j == 0)
    def _():
        xn_ref[...] = _rms(x_ref[...].reshape(rows, d), g_ref[...]).astype(BF16)
        o_ref[...] = jnp.zeros(o_ref.shape, F32)
        if not prev_is_state:
            pn = _rms(prev_ref[...].reshape(SUBLANES, d), g_ref[...])
            pn_ref[...] = jnp.concatenate([pn, jnp.zeros_like(pn)], axis=0).astype(BF16)

    xn = xn_ref[...]
    tn = wg_ref.shape[1]
    if not prev_is_state:
        first = (pl.program_id(0) % seq_start_every) == 0
    ws = tn // FFN_SUB
    cols = [slice(k * ws, (k + 1) * ws) for k in range(FFN_SUB)]

    def up_proj(k):
        cs = cols[k]
        ug = _dot(xn, wg_ref[:, cs]).reshape(nb, r, ws)
        uu = _dot(xn, wu_ref[:, cs]).reshape(nb, r, ws)
        if prev_is_state:
            pg = prev_ref[:, :, cs]
            pu = prevu_ref[:, :, cs]
        else:
            pg = jnp.where(first, 0.0, _dot(pn_ref[...], wg_ref[:, cs])[:SUBLANES]).reshape(1, SUBLANES, ws)
            pu = jnp.where(first, 0.0, _dot(pn_ref[...], wu_ref[:, cs])[:SUBLANES]).reshape(1, SUBLANES, ws)
        return ug, uu, pg, pu

    nxt = up_proj(0)
    down = None
    for k in range(FFN_SUB):
        cs = cols[k]
        ug, uu, pg, pu = nxt
        if k + 1 < FFN_SUB:
            nxt = up_proj(k + 1)
        csg_ref[:, :, cs] = ug[:, r - SUBLANES:, :]
        csu_ref[:, :, cs] = uu[:, r - SUBLANES:, :]
        gate = _dwconv(ug, pg, cwg_ref[:, cs]).reshape(rows, ws)
        up = _dwconv(uu, pu, cwu_ref[:, cs]).reshape(rows, ws)
        hidden = (_silu(gate) * up).astype(BF16)
        part = _dot(hidden, wd_ref[cs, :])
        down = part if down is None else down + part
    o_ref[...] += down.reshape(nb, r, d)

    @pl.when(j == pl.num_programs(1) - 1)
    def _():
        y = (x_ref[...] + o_ref[...]).reshape(rows, d)
        if final_norm:
            y = _rms(y, gf_ref[...])
        o_ref[...] = y.reshape(nb, r, d)


def _ffn(x3, conv_prev, norm_g, w_up, conv_w, w_down, layer, norm_final, nb, r, tn, final_norm):
    nseq, t, d = x3.shape
    nj = D_FF // tn
    steps_per_seq = t // r
    if conv_prev is None:
        assert nb == 1
        rb = r // SUBLANES
        prev_arr = x3
        prev_spec = pl.BlockSpec((1, SUBLANES, d), lambda i, j: (i // steps_per_seq,
                                                                 jnp.maximum((i % steps_per_seq) * rb - 1, 0), 0))
        prevu_spec = prev_spec
        x_map = lambda i, j: (i // steps_per_seq, i % steps_per_seq, 0)
        grid = (nseq * steps_per_seq, nj)
    else:
        assert r == t
        prev_arr = conv_prev
        prev_spec = pl.BlockSpec((nb, SUBLANES, tn), lambda i, j: (i, 0, j))
        prevu_spec = pl.BlockSpec((nb, SUBLANES, tn), lambda i, j: (i, 0, nj + j))
        x_map = lambda i, j: (i, 0, 0)
        grid = (nseq // nb, nj)
    rows = nb * r
    out, csg, csu = pl.pallas_call(
        functools.partial(_ffn_kernel, seq_start_every=steps_per_seq, prev_is_state=conv_prev is not None,
                          final_norm=final_norm),
        grid=grid,
        in_specs=[pl.BlockSpec((nb, r, d), x_map, pipeline_mode=pl.Buffered(1)), prev_spec, prevu_spec,
                  pl.BlockSpec((1, d), lambda i, j: (0, 0)),
                  pl.BlockSpec((None, d, tn), lambda i, j: (layer, 0, j)),
                  pl.BlockSpec((None, d, tn), lambda i, j: (layer, 0, nj + j)),
                  pl.BlockSpec((FFN_CONV, tn), lambda i, j: (0, j)),
                  pl.BlockSpec((FFN_CONV, tn), lambda i, j: (0, nj + j)),
                  pl.BlockSpec((None, tn, d), lambda i, j: (layer, j, 0)),
                  pl.BlockSpec((1, d), lambda i, j: (0, 0))],
        out_specs=[pl.BlockSpec((nb, r, d), x_map),
                   pl.BlockSpec((nb, SUBLANES, tn), lambda i, j: (i, 0, j)),
                   pl.BlockSpec((nb, SUBLANES, tn), lambda i, j: (i, 0, j))],
        out_shape=[jax.ShapeDtypeStruct((nseq, t, d), F32),
                   jax.ShapeDtypeStruct((grid[0] * nb, SUBLANES, D_FF), F32),
                   jax.ShapeDtypeStruct((grid[0] * nb, SUBLANES, D_FF), F32)],
        scratch_shapes=[pltpu.VMEM((rows, d), BF16), pltpu.VMEM((2 * SUBLANES, d), BF16)],
        compiler_params=_params("arbitrary", "arbitrary"),
        name="conv_ffn",
    )(x3, prev_arr, prev_arr, norm_g.reshape(1, d), w_up, w_up, conv_w, conv_w, w_down, norm_final.reshape(1, d))
    csg = csg.reshape(nseq, -1, SUBLANES, D_FF)[:, -1]
    csu = csu.reshape(nseq, -1, SUBLANES, D_FF)[:, -1]
    return out, csg, csu


def _pack_w_in(w):
    d = w.shape[0]
    cuts = np.cumsum([Q_RANK, KV_RANK, ROPE_DIM, 3 * W_B, H_B, H_B, W_B, W_C, W_C, W_C, W_C])
    c_q, c_kv, k_pe, qkv, a_b, b_b, z_b, q_c, f_c, i_c, z_c = jnp.split(w, [int(c) for c in cuts[:-1]], axis=1)
    z = lambda n: jnp.zeros((d, n), w.dtype)
    misc = jnp.concatenate([c_kv, k_pe, z(MISC_AB - MISC_KPE - ROPE_DIM), a_b, b_b,
                            z(PROJ_TILE - MISC_AB - 2 * H_B)], axis=1)
    return jnp.concatenate([qkv, c_q, misc, z_b, q_c, f_c, i_c, z_c], axis=1).astype(BF16)


def _pack_w_uq(w):
    w3 = w.reshape(Q_RANK, H_A, NOPE_DIM + ROPE_DIM)
    return jnp.concatenate([w3[:, :, :NOPE_DIM].reshape(Q_RANK, -1),
                            w3[:, :, NOPE_DIM:].reshape(Q_RANK, -1)], axis=1).astype(BF16)


def _rope_tables(pos):
    half = ROPE_DIM // 2
    inv = ROPE_THETA ** (-jnp.arange(half, dtype=F32) / half)
    ang = pos.astype(F32)[:, None] * inv[None, :]
    cos, sin = jnp.cos(ang), jnp.sin(ang)
    reps = LANES // ROPE_DIM
    return (jnp.tile(jnp.concatenate([cos, cos], axis=1), (1, reps)),
            jnp.tile(jnp.concatenate([-sin, sin], axis=1), (1, reps)))


def _pad_rows_front(a, total):
    n, k, c = a.shape
    return jnp.concatenate([jnp.zeros((n, total - k, c), a.dtype), a], axis=1)


def _row_tile(m, cap=512):
    for tm in (1024, 512, 256, 128, 64, 32, 16, 8):
        if tm <= cap and m % tm == 0:
            return tm
    raise ValueError(m)


def _layer(x3, cos, sin, lw, lb, layer, past, final_norm_g, is_last):
    nseq, t, d = x3.shape
    m = nseq * t
    tm = _row_tile(m)
    x = x3.reshape(m, d)
    proj = _norm_matmul(x, lw["norm_mix"], lw["w_in"], _row_tile(m, 1024), 2 * PROJ_TILE)
    proj3 = proj.reshape(nseq, t, PROJ_WIDTH)

    qlat, qpe, ckv, kpe = _mla_prep(proj, cos, sin, lw["q_norm"], lw["kv_norm"], lw["w_uq"], lw["w_uk_t"],
                                    min(tm, 256), BF16 if past is None else F32)
    if past is None:
        o_a = _prompt_attn(qlat, qpe, ckv, kpe, lw["w_uv_t"], nseq, t, min(256, t))
        gdn_prev = hg_s0 = gdn_s0 = ffn_prev = None
        nb, r = 1, min(tm, t)
        r_ffn = min(_row_tile(m, 1024), t)
    else:
        n_pages = past["page_table"].shape[1]
        pps = 32 if n_pages % 32 == 0 else n_pages
        o_a = _decode_attn(qlat, qpe, ckv, kpe, past["cache_ckv"], past["cache_kpe_t"], past["page_table"],
                           lw["w_uv_t"], layer, t, pps, 4)
        gdn_prev = _pad_rows_front(past["gdn_conv"], SUBLANES)
        gdn_s0, hg_s0 = (past["gdn_S"], layer), (past["hgrn_S"], layer)
        nb, r = min(nseq, 512 // t), t
        r_ffn = r
    q, k, v, gb, gdn_cs = _gdn_prep(proj3, gdn_prev, lw["gdn_conv_w"], lw["gdn_a_log"], lw["gdn_dt_bias"], nb, r)
    o_b, gdn_s = _gdn(q, k, v, gb, proj, lw["gdn_norm"], gdn_s0, nseq, t)
    o_c, hg_s = _hgrn(proj, lb, lw["hg_norm"], hg_s0, nseq, t)
    x = _out_proj(x, o_a, o_b, o_c, lw["w_out"], layer, tm, 1024)

    tn = 512
    ffn_prev = None if past is None else _pad_rows_front(past["ffn_conv"], SUBLANES)
    x3, csg, csu = _ffn(x.reshape(nseq, t, d), ffn_prev, lw["norm_ffn"], lw["w_up"], lw["ffn_conv_w"],
                        lw["w_down"], layer, final_norm_g, nb, r_ffn, tn, is_last)
    ffn_cs = jnp.concatenate([csg, csu], axis=2)[:, SUBLANES - (FFN_CONV - 1):]
    new_state = (ckv.reshape(nseq, t, KV_RANK), kpe.reshape(nseq, t, ROPE_DIM), gdn_s,
                 gdn_cs[:, SUBLANES - (CONV_B - 1):], hg_s, ffn_cs)
    return x3, new_state


def kernel(x_prompt, x_sample, cache_ckv, cache_kpe, page_table, state_gdn_S, state_gdn_conv, state_hgrn_S, state_ffn_conv, norm_mix, w_in, mla_q_norm, mla_kv_norm, mla_w_uq, mla_w_uk, mla_w_uv, gdn_conv_w, gdn_a_log, gdn_dt_bias, gdn_norm, hgrn_lb, hgrn_norm, w_out, norm_ffn, ffn_w_up, ffn_conv_w, ffn_w_down, norm_final):
    depth = w_in.shape[0]
    bp, tp, _ = x_prompt.shape
    bs, ts, _ = x_sample.shape
    past_len = page_table.shape[1] * PAGE_SIZE
    cos_p, sin_p = _rope_tables(jnp.tile(jnp.arange(tp, dtype=jnp.int32), bp))
    cos_s, sin_s = _rope_tables(jnp.tile(past_len + jnp.arange(ts, dtype=jnp.int32), bs))
    p_lb = jax.nn.softmax(hgrn_lb.astype(F32), axis=0)
    lb_all = jnp.cumsum(p_lb, axis=0) - p_lb[0:1]
    cache_kpe_t = jnp.swapaxes(cache_kpe, 2, 3)

    w_out_b, w_up_b, w_down_b = w_out.astype(BF16), ffn_w_up.astype(BF16), ffn_w_down.astype(BF16)

    xp, xs = x_prompt, x_sample
    new_p, new_s = [], []
    for l in range(depth):
        lw = dict(norm_mix=norm_mix[l], w_in=_pack_w_in(w_in[l]), q_norm=mla_q_norm[l], kv_norm=mla_kv_norm[l],
                  w_uq=_pack_w_uq(mla_w_uq[l]),
                  w_uk_t=jnp.transpose(mla_w_uk[l], (1, 2, 0)).astype(BF16),
                  w_uv_t=jnp.transpose(mla_w_uv[l], (1, 0, 2)).astype(BF16),
                  gdn_conv_w=gdn_conv_w[l], gdn_a_log=gdn_a_log[l], gdn_dt_bias=gdn_dt_bias[l],
                  gdn_norm=gdn_norm[l], hg_norm=hgrn_norm[l], w_out=w_out_b,
                  norm_ffn=norm_ffn[l], w_up=w_up_b, ffn_conv_w=ffn_conv_w[l], w_down=w_down_b)
        last = l == depth - 1
        xp, st_p = _layer(xp, cos_p, sin_p, lw, lb_all[l], l, None, norm_final, last)
        past = dict(cache_ckv=cache_ckv, cache_kpe_t=cache_kpe_t, page_table=page_table,
                    gdn_S=state_gdn_S, gdn_conv=state_gdn_conv[l], hgrn_S=state_hgrn_S,
                    ffn_conv=state_ffn_conv[l])
        xs, st_s = _layer(xs, cos_s, sin_s, lw, lb_all[l], l, past, norm_final, last)
        new_p.append(st_p)
        new_s.append(st_s)
    outs_p = tuple(jnp.stack(a) for a in zip(*new_p))
    outs_s = tuple(jnp.stack(a) for a in zip(*new_s))
    return (xp, xs) + outs_p + outs_s
```

```python
import functools

import jax
import jax.numpy as jnp
import numpy as np
from jax import lax
from jax.experimental import pallas as pl
from jax.experimental.pallas import tpu as pltpu

F32 = jnp.float32
BF16 = jnp.bfloat16

D_MODEL = 2048
PAGE_SIZE = 128
HEAD_DIM = 128
H_A = 8
H_B = 4
H_C = 4
Q_RANK = 512
KV_RANK = 256
NOPE_DIM = 128
ROPE_DIM = 64
W_B = H_B * HEAD_DIM
W_C = H_C * HEAD_DIM
CONV_B = 4
D_FF = 5632
FFN_CONV = 3
ROPE_THETA = 10000.0
EPS = 1e-6
NEG = -1e30
LB_FLOOR = 1e-30

PROJ_TILE = 512
COL_QKV = 0
COL_CQ = 3
COL_MISC = 4
COL_ZB = 5
COL_QC = 6
COL_FC = 7
COL_IC = 8
COL_ZC = 9
PROJ_WIDTH = 10 * PROJ_TILE
MISC_KPE = 256
MISC_AB = 384

SUBLANES = 8
LANES = 128
CHUNK = 128
HG_CHUNK = 16
FFN_SUB = 1
DECODE_SLOTS = 3
VMEM_LIMIT = 56 * 1024 * 1024


def _params(*sem):
    return pltpu.CompilerParams(dimension_semantics=sem, vmem_limit_bytes=VMEM_LIMIT)


def _dot(a, b):
    return jnp.dot(a, b, preferred_element_type=F32)


def _dot_nt(a, b):
    return lax.dot_general(a, b, (((1,), (1,)), ((), ())), preferred_element_type=F32)


def _bdot(a, b):
    return _dot(a.astype(BF16), b.astype(BF16))


def _split2(a):
    hi = a.astype(BF16)
    lo = (a - hi.astype(F32)).astype(BF16)
    return hi, lo


def _dot3(a, b):
    ah, al = _split2(a)
    bh, bl = _split2(b)
    return _dot(ah, bh) + (_dot(ah, bl) + _dot(al, bh))


def _dot_exact_lhs(a01, b):
    a = a01.astype(BF16)
    b1 = b.astype(BF16)
    r1 = b - b1.astype(F32)
    b2 = r1.astype(BF16)
    b3 = (r1 - b2.astype(F32)).astype(BF16)
    return _dot(a, b1) + (_dot(a, b2) + _dot(a, b3))


def _sigmoid(x):
    return 1.0 / (1.0 + jnp.exp(-x))


def _silu(x):
    return x * _sigmoid(x)


def _softplus(x):
    return jnp.maximum(x, 0.0) + jnp.log1p(jnp.exp(-jnp.abs(x)))


def _rms(x, g):
    ms = jnp.mean(x * x, axis=-1, keepdims=True)
    return x * lax.rsqrt(ms + EPS) * g


def _swap_halves(x, lane):
    w = x.shape[-1]
    half = ROPE_DIM // 2
    fwd = pltpu.roll(x, w - half, axis=x.ndim - 1)
    bwd = pltpu.roll(x, half, axis=x.ndim - 1)
    return jnp.where((lane % ROPE_DIM) < half, fwd, bwd)


def _rope(x, cos, sin_signed):
    lane = lax.broadcasted_iota(jnp.int32, x.shape, x.ndim - 1)
    return x * cos + _swap_halves(x, lane) * sin_signed


def _norm_matmul_kernel(x_ref, g_ref, w_ref, o_ref, xn_ref):
    @pl.when(pl.program_id(1) == 0)
    def _():
        xn_ref[...] = _rms(x_ref[...], g_ref[...]).astype(BF16)

    o_ref[...] = _dot(xn_ref[...], w_ref[...])


def _norm_matmul(x, g, w, tm, tn):
    m, k = x.shape
    n = w.shape[1]
    return pl.pallas_call(
        _norm_matmul_kernel,
        grid=(m // tm, n // tn),
        in_specs=[pl.BlockSpec((tm, k), lambda i, j: (i, 0)),
                  pl.BlockSpec((1, k), lambda i, j: (0, 0)),
                  pl.BlockSpec((k, tn), lambda i, j: (0, j))],
        out_specs=pl.BlockSpec((tm, tn), lambda i, j: (i, j)),
        out_shape=jax.ShapeDtypeStruct((m, n), F32),
        scratch_shapes=[pltpu.VMEM((tm, k), BF16)],
        compiler_params=_params("parallel", "arbitrary"),
        name="in_proj",
    )(x, g.reshape(1, k), w)


def _mla_prep_kernel(cq_ref, misc_ref, cos_ref, sin_ref, qn_ref, kvn_ref, wuq_ref, wuk_ref,
                     qlat_ref, qpe_ref, ckv_ref, kpe_ref, *, scale):
    cos = cos_ref[...]
    sin = sin_ref[...]
    misc = misc_ref[...]
    ckv_ref[...] = _rms(misc[:, :KV_RANK], kvn_ref[...])
    kpe = _rope(misc[:, MISC_KPE:MISC_KPE + LANES], cos, sin)
    kpe_ref[...] = kpe[:, :ROPE_DIM]

    cqn = _rms(cq_ref[...], qn_ref[...]).astype(BF16)
    q = _dot(cqn, wuq_ref[...])
    n_nope = H_A * NOPE_DIM
    reps = (H_A * ROPE_DIM) // LANES
    q_pe = _rope(q[:, n_nope:], jnp.tile(cos, (1, reps)), jnp.tile(sin, (1, reps))) * scale
    for h in range(H_A):
        qn = q[:, h * NOPE_DIM:(h + 1) * NOPE_DIM].astype(BF16)
        qlat_ref[h] = (_dot(qn, wuk_ref[h]) * scale).astype(qlat_ref.dtype)
        qpe_ref[h] = q_pe[:, h * ROPE_DIM:(h + 1) * ROPE_DIM].astype(qpe_ref.dtype)


def _mla_prep(proj, cos, sin, q_norm, kv_norm, w_uq, w_uk_t, tm, q_dtype):
    m = proj.shape[0]
    scale = float((NOPE_DIM + ROPE_DIM) ** -0.5)
    return pl.pallas_call(
        functools.partial(_mla_prep_kernel, scale=scale),
        grid=(m // tm,),
        in_specs=[pl.BlockSpec((tm, PROJ_TILE), lambda i: (i, COL_CQ)),
                  pl.BlockSpec((tm, PROJ_TILE), lambda i: (i, COL_MISC)),
                  pl.BlockSpec((tm, LANES), lambda i: (i, 0)),
                  pl.BlockSpec((tm, LANES), lambda i: (i, 0)),
                  pl.BlockSpec((1, Q_RANK), lambda i: (0, 0)),
                  pl.BlockSpec((1, KV_RANK), lambda i: (0, 0)),
                  pl.BlockSpec(w_uq.shape, lambda i: (0, 0)),
                  pl.BlockSpec(w_uk_t.shape, lambda i: (0, 0, 0))],
        out_specs=[pl.BlockSpec((H_A, tm, KV_RANK), lambda i: (0, i, 0)),
                   pl.BlockSpec((H_A, tm, ROPE_DIM), lambda i: (0, i, 0)),
                   pl.BlockSpec((tm, KV_RANK), lambda i: (i, 0)),
                   pl.BlockSpec((tm, ROPE_DIM), lambda i: (i, 0))],
        out_shape=[jax.ShapeDtypeStruct((H_A, m, KV_RANK), q_dtype),
                   jax.ShapeDtypeStruct((H_A, m, ROPE_DIM), q_dtype),
                   jax.ShapeDtypeStruct((m, KV_RANK), F32),
                   jax.ShapeDtypeStruct((m, ROPE_DIM), F32)],
        compiler_params=_params("parallel"),
        name="mla_prep",
    )(proj, proj, cos, sin, q_norm.reshape(1, -1), kv_norm.reshape(1, -1), w_uq, w_uk_t)


def _lanes(x, n):
    return x if n == LANES else jnp.concatenate([x] * (n // LANES), axis=1)


def _prompt_attn_kernel(qi_tab, ki_tab, qlat_ref, qpe_ref, ckv_ref, kpe_ref, wuv_ref, o_ref,
                        m_ref, l_ref, acc_ref):
    step = pl.program_id(1)
    qi = qi_tab[step]
    ki = ki_tab[step]
    tq, tk = qlat_ref.shape[1], ckv_ref.shape[0]

    @pl.when(ki == 0)
    def _():
        m_ref[...] = jnp.full(m_ref.shape, NEG, F32)
        l_ref[...] = jnp.zeros(l_ref.shape, F32)
        acc_ref[...] = jnp.zeros(acc_ref.shape, F32)

    def update(masked):
        kc = ckv_ref[...].astype(BF16)
        kp = kpe_ref[...].astype(BF16)
        if masked:
            keep = (lax.broadcasted_iota(jnp.int32, (tq, tk), 1)
                    <= lax.broadcasted_iota(jnp.int32, (tq, tk), 0))
        scores = lambda h: _dot_nt(qlat_ref[h], kc) + _dot_nt(qpe_ref[h], kp)

        def accumulate(h, p, alpha):
            acc_ref[h] = _lanes(alpha, KV_RANK) * acc_ref[h] + _dot(p, kc)

        s_next = scores(0)
        pending = None
        for h in range(H_A):
            s = s_next
            if h + 1 < H_A:
                s_next = scores(h + 1)
            if masked:
                s = jnp.where(keep, s, NEG)
            m_prev = m_ref[h]
            m_new = jnp.maximum(m_prev, jnp.max(s, axis=-1, keepdims=True))
            alpha = jnp.exp(m_prev - m_new)
            p = jnp.exp(s - _lanes(m_new, tk))
            l_ref[h] = alpha * l_ref[h] + jnp.sum(p, axis=-1, keepdims=True)
            m_ref[h] = m_new
            if pending is not None:
                accumulate(*pending)
            pending = (h, p.astype(BF16), alpha)
        accumulate(*pending)

    @pl.when(ki < qi)
    def _():
        update(False)

    @pl.when(ki == qi)
    def _():
        update(True)
        for h in range(H_A):
            o = (acc_ref[h] * _lanes(1.0 / l_ref[h], KV_RANK)).astype(BF16)
            o_ref[:, h * HEAD_DIM:(h + 1) * HEAD_DIM] = _dot(o, wuv_ref[h]).astype(o_ref.dtype)


def _prompt_attn(qlat, qpe, ckv, kpe, w_uv_t, nseq, t, tq):
    nq = t // tq
    pairs = [(qi, ki) for qi in range(nq) for ki in range(qi + 1)]
    qi_tab = jnp.asarray([p[0] for p in pairs], jnp.int32)
    ki_tab = jnp.asarray([p[1] for p in pairs], jnp.int32)
    q_map = lambda b, s, qt, kt: (0, b * nq + qt[s], 0)
    kv_map = lambda b, s, qt, kt: (b * nq + kt[s], 0)
    grid_spec = pltpu.PrefetchScalarGridSpec(
        num_scalar_prefetch=2,
        grid=(nseq, len(pairs)),
        in_specs=[pl.BlockSpec((H_A, tq, KV_RANK), q_map),
                  pl.BlockSpec((H_A, tq, ROPE_DIM), q_map),
                  pl.BlockSpec((tq, KV_RANK), kv_map),
                  pl.BlockSpec((tq, ROPE_DIM), kv_map),
                  pl.BlockSpec(w_uv_t.shape, lambda b, s, qt, kt: (0, 0, 0))],
        out_specs=pl.BlockSpec((tq, H_A * HEAD_DIM), lambda b, s, qt, kt: (b * nq + qt[s], 0)),
        scratch_shapes=[pltpu.VMEM((H_A, tq, LANES), F32), pltpu.VMEM((H_A, tq, LANES), F32),
                        pltpu.VMEM((H_A, tq, KV_RANK), F32)],
    )
    return pl.pallas_call(
        _prompt_attn_kernel,
        grid_spec=grid_spec,
        out_shape=jax.ShapeDtypeStruct((nseq * t, H_A * HEAD_DIM), BF16),
        compiler_params=_params("parallel", "arbitrary"),
        name="prompt_attn",
    )(qi_tab, ki_tab, qlat, qpe, ckv, kpe, w_uv_t)


def _decode_attn_kernel(pt_ref, qlat_ref, qpe_ref, nckv_ref, nkpe_ref, wuv_ref, ckv_hbm, kpe_t_hbm,
                        o_ref, kbuf, pbuf, sem, m_ref, l_ref, acc_ref, *, layer, pps, ts, n_chains):
    b = pl.program_id(0)
    j = pl.program_id(1)
    nj = pl.num_programs(1)
    step = b * nj + j
    last_step = pl.num_programs(0) * nj - 1
    slot = step % DECODE_SLOTS
    rows = H_A * ts

    def page_copies(bb, jj, sl, lookup=True):
        out = []
        for i in range(pps):
            page = pt_ref[bb, jj * pps + i] if lookup else 0
            out.append(pltpu.make_async_copy(ckv_hbm.at[layer, page], kbuf.at[sl, i], sem.at[sl, 0]))
            out.append(pltpu.make_async_copy(kpe_t_hbm.at[layer, page], pbuf.at[sl, i], sem.at[sl, 1]))
        return out

    def fetch_slot(k):
        return (k % DECODE_SLOTS) if isinstance(k, int) else lax.rem(k, DECODE_SLOTS)

    @pl.when(step == 0)
    def _():
        for ahead in range(DECODE_SLOTS - 1):
            tgt = jnp.minimum(ahead, last_step)
            for cp in page_copies(tgt // nj, tgt % nj, fetch_slot(ahead)):
                cp.start()

    for cp in page_copies(b, j, slot, lookup=False):
        cp.wait()
    ahead_slot = fetch_slot(step + DECODE_SLOTS - 1)
    nxt = jnp.minimum(step + DECODE_SLOTS - 1, last_step)
    prefetch = page_copies(nxt // nj, nxt % nj, ahead_slot)

    ql = qlat_ref[...].reshape(rows, KV_RANK).astype(BF16)
    qp = qpe_ref[...].reshape(rows, ROPE_DIM).astype(BF16)

    @pl.when(j == 0)
    def _():
        kc = nckv_ref[...].astype(BF16)
        kp = nkpe_ref[...].astype(BF16)
        s = _dot_nt(ql, kc) + _dot_nt(qp, kp)
        nb = kc.shape[0] // ts
        t_row = lax.broadcasted_iota(jnp.int32, s.shape, 0) % ts
        col = lax.broadcasted_iota(jnp.int32, s.shape, 1)
        ok = (col // ts == b % nb) & (col % ts <= t_row)
        s = jnp.where(ok, s, NEG)
        m = jnp.max(s, axis=-1, keepdims=True)
        p = jnp.where(ok, jnp.exp(s - m), 0.0)
        m_ref[0] = m
        l_ref[0] = jnp.sum(p, axis=-1, keepdims=True)
        acc_ref[0] = _dot(p.astype(BF16), kc)
        for c in range(1, n_chains):
            m_ref[c] = jnp.full((rows, 1), NEG, F32)
            l_ref[c] = jnp.zeros((rows, 1), F32)
            acc_ref[c] = jnp.zeros((rows, KV_RANK), F32)

    per = pps // n_chains
    def chain_scores(c):
        kcs, scores = [], []
        for i in range(c * per, (c + 1) * per):
            prefetch[2 * i].start()
            prefetch[2 * i + 1].start(priority=1)
            kcs.append(kbuf[slot, i].astype(BF16))
            scores.append(_dot_nt(ql, kcs[-1]) + _dot(qp, pbuf[slot, i].astype(BF16)))
        return kcs, jnp.concatenate(scores, axis=1)

    nxt_chain = chain_scores(0)
    for c in range(n_chains):
        kcs, s = nxt_chain
        if c + 1 < n_chains:
            nxt_chain = chain_scores(c + 1)
        m_prev = m_ref[c]
        m_new = jnp.maximum(m_prev, jnp.max(s, axis=-1, keepdims=True))
        alpha = jnp.exp(m_prev - m_new)
        p = jnp.exp(s - m_new).astype(BF16)
        l_ref[c] = alpha * l_ref[c] + jnp.sum(p.astype(F32), axis=-1, keepdims=True)
        pv = _dot(p[:, :PAGE_SIZE], kcs[0])
        for i in range(1, per):
            pv += _dot(p[:, i * PAGE_SIZE:(i + 1) * PAGE_SIZE], kcs[i])
        acc_ref[c] = alpha * acc_ref[c] + pv
        m_ref[c] = m_new

    @pl.when(step == last_step)
    def _():
        for extra in range(1, DECODE_SLOTS):
            for cp in page_copies(b, j, fetch_slot(step + extra), lookup=False):
                cp.wait()

    @pl.when(j == nj - 1)
    def _():
        m = m_ref[0]
        for c in range(1, n_chains):
            m = jnp.maximum(m, m_ref[c])
        l = jnp.zeros((rows, 1), F32)
        acc = jnp.zeros((rows, KV_RANK), F32)
        for c in range(n_chains):
            w = jnp.exp(m_ref[c] - m)
            l += w * l_ref[c]
            acc += w * acc_ref[c]
        o = acc * (1.0 / l)
        for h in range(H_A):
            oh = o[h * ts:(h + 1) * ts].astype(BF16)
            o_ref[:, h * HEAD_DIM:(h + 1) * HEAD_DIM] = _dot(oh, wuv_ref[h]).astype(o_ref.dtype)


def _decode_attn(qlat, qpe, ckv_new, kpe_new, cache_ckv, cache_kpe_t, page_table, w_uv_t, layer, ts, pps, n_chains):
    nseq, n_pages = page_table.shape
    rows = H_A * ts
    new_rows = min(LANES, nseq * ts)
    nb = new_rows // ts

    grid_spec = pltpu.PrefetchScalarGridSpec(
        num_scalar_prefetch=1,
        grid=(nseq, n_pages // pps),
        in_specs=[pl.BlockSpec((H_A, ts, KV_RANK), lambda b, j, pt: (0, b, 0)),
                  pl.BlockSpec((H_A, ts, ROPE_DIM), lambda b, j, pt: (0, b, 0)),
                  pl.BlockSpec((new_rows, KV_RANK), lambda b, j, pt: (b // nb, 0)),
                  pl.BlockSpec((new_rows, ROPE_DIM), lambda b, j, pt: (b // nb, 0)),
                  pl.BlockSpec(w_uv_t.shape, lambda b, j, pt: (0, 0, 0)),
                  pl.BlockSpec(memory_space=pl.ANY),
                  pl.BlockSpec(memory_space=pl.ANY)],
        out_specs=pl.BlockSpec((ts, H_A * HEAD_DIM), lambda b, j, pt: (b, 0)),
        scratch_shapes=[pltpu.VMEM((DECODE_SLOTS, pps, PAGE_SIZE, KV_RANK), F32),
                        pltpu.VMEM((DECODE_SLOTS, pps, ROPE_DIM, PAGE_SIZE), F32),
                        pltpu.SemaphoreType.DMA((DECODE_SLOTS, 2)),
                        pltpu.VMEM((n_chains, rows, 1), F32), pltpu.VMEM((n_chains, rows, 1), F32),
                        pltpu.VMEM((n_chains, rows, KV_RANK), F32)],
    )
    return pl.pallas_call(
        functools.partial(_decode_attn_kernel, layer=layer, pps=pps, ts=ts, n_chains=n_chains),
        grid_spec=grid_spec,
        out_shape=jax.ShapeDtypeStruct((nseq * ts, H_A * HEAD_DIM), F32),
        compiler_params=_params("arbitrary", "arbitrary"),
        name="decode_attn",
    )(page_table, qlat, qpe, ckv_new, kpe_new, w_uv_t, cache_ckv, cache_kpe_t)


def _shifted(x, prev, j):
    if j == 0:
        return x
    rx = pltpu.roll(x, j, axis=1)
    rp = pltpu.roll(prev, j, axis=1)
    row = lax.broadcasted_iota(jnp.int32, prev.shape, 1)
    head = jnp.where(row < j, rp, rx[:, :SUBLANES])
    if x.shape[1] == SUBLANES:
        return head
    return jnp.concatenate([head, rx[:, SUBLANES:]], axis=1)


def _dwconv(x, prev, w):
    k = w.shape[0]
    y = x * w[k - 1:k]
    for j in range(k - 1):
        y = y + _shifted(x, prev, k - 1 - j) * w[j:j + 1]
    return y


def _gdn_prep_kernel(x_ref, prev_ref, misc_ref, cw_ref, alog_ref, dtb_ref,
                     q_ref, k_ref, v_ref, gb_ref, cs_ref, *, seq_start_every):
    x = x_ref[...]
    nb, r, c = x.shape
    prev = prev_ref[...]
    if seq_start_every:
        first = (pl.program_id(0) % seq_start_every) == 0
        prev = jnp.where(first, 0.0, prev)
    y = _silu(_dwconv(x, prev, cw_ref[...])).reshape(nb * r, c)
    cs_ref[...] = x[:, r - SUBLANES:, :]
    for h in range(H_B):
        sl = slice(h * HEAD_DIM, (h + 1) * HEAD_DIM)
        qh = y[:, h * HEAD_DIM:(h + 1) * HEAD_DIM]
        kh = y[:, W_B + h * HEAD_DIM:W_B + (h + 1) * HEAD_DIM]
        q_ref[:, sl] = qh * lax.rsqrt(jnp.sum(qh * qh, axis=-1, keepdims=True) + EPS) * (HEAD_DIM ** -0.5)
        k_ref[:, sl] = kh * lax.rsqrt(jnp.sum(kh * kh, axis=-1, keepdims=True) + EPS)
    v_ref[...] = y[:, 2 * W_B:]
    ab = misc_ref[...].reshape(nb * r, PROJ_TILE)[:, MISC_AB:MISC_AB + LANES]
    g = -jnp.exp(alog_ref[...]) * _softplus(ab + dtb_ref[...])
    lane = lax.broadcasted_iota(jnp.int32, ab.shape, 1)
    gb_ref[...] = jnp.where(lane < H_B, g, _sigmoid(ab))


def _gdn_prep(proj3, conv_prev, conv_w, a_log, dt_bias, nb, r):
    nseq, t, _ = proj3.shape
    c = 3 * W_B
    steps_per_seq = t // r
    m = nseq * t
    rows = nb * r
    if conv_prev is None:
        assert nb == 1
        rb = r // SUBLANES
        prev_arr = proj3
        prev_spec = pl.BlockSpec((1, SUBLANES, c), lambda i: (i // steps_per_seq,
                                                               jnp.maximum((i % steps_per_seq) * rb - 1, 0), COL_QKV))
        x_map = lambda i: (i // steps_per_seq, i % steps_per_seq, COL_QKV)
        misc_map = lambda i: (i // steps_per_seq, i % steps_per_seq, COL_MISC)
        grid = (nseq * steps_per_seq,)
        seq_start_every = steps_per_seq
    else:
        assert r == t
        prev_arr = conv_prev
        prev_spec = pl.BlockSpec((nb, SUBLANES, c), lambda i: (i, 0, 0))
        x_map = lambda i: (i, 0, COL_QKV)
        misc_map = lambda i: (i, 0, COL_MISC)
        grid = (nseq // nb,)
        seq_start_every = 0
    alog = jnp.zeros((1, LANES), F32).at[0, :H_B].set(a_log.astype(F32))
    dtb = jnp.zeros((1, LANES), F32).at[0, :H_B].set(dt_bias.astype(F32))
    row_spec = lambda w: pl.BlockSpec((rows, w), lambda i: (i, 0))
    q, k, v, gb, cs = pl.pallas_call(
        functools.partial(_gdn_prep_kernel, seq_start_every=seq_start_every),
        grid=grid,
        in_specs=[pl.BlockSpec((nb, r, c), x_map), prev_spec,
                  pl.BlockSpec((nb, r, PROJ_TILE), misc_map),
                  pl.BlockSpec((CONV_B, c), lambda i: (0, 0)),
                  pl.BlockSpec((1, LANES), lambda i: (0, 0)),
                  pl.BlockSpec((1, LANES), lambda i: (0, 0))],
        out_specs=[row_spec(W_B), row_spec(W_B), row_spec(W_B), row_spec(LANES),
                   pl.BlockSpec((nb, SUBLANES, c), lambda i: (i, 0, 0))],
        out_shape=[jax.ShapeDtypeStruct((m, W_B), F32), jax.ShapeDtypeStruct((m, W_B), F32),
                   jax.ShapeDtypeStruct((m, W_B), F32), jax.ShapeDtypeStruct((m, LANES), F32),
                   jax.ShapeDtypeStruct((grid[0] * nb, SUBLANES, c), F32)],
        compiler_params=_params("arbitrary"),
        name="gdn_prep",
    )(proj3, prev_arr, proj3, conv_w, alog, dtb)
    cs = cs.reshape(nseq, -1, SUBLANES, c)[:, -1]
    return q, k, v, gb, cs


def _gdn_kernel(q_ref, k_ref, v_ref, gb_ref, z_ref, gn_ref, *rest, nseq_blk, r, zero_init):
    if zero_init:
        o_ref, sout_ref, s_scr, wS_scr, qS_scr, vn_scr = rest
    else:
        s0_ref, o_ref, sout_ref, s_scr, wS_scr, qS_scr, vn_scr = rest
    c = nseq_blk * r
    ci = pl.program_id(1)

    @pl.when(ci == 0)
    def _():
        if zero_init:
            s_scr[...] = jnp.zeros(s_scr.shape, F32)
        else:
            s_scr[...] = s0_ref[...]

    row = lax.broadcasted_iota(jnp.int32, (c, c), 0)
    col = lax.broadcasted_iota(jnp.int32, (c, c), 1)
    same = (row // r) == (col // r)
    causal = same & (col <= row)
    strict = same & (col < row)
    eye = (row == col).astype(F32)

    gb = gb_ref[...]
    gcs = _dot_exact_lhs(causal.astype(F32), gb)
    gtot = _dot_exact_lhs(same.astype(F32), gb)
    gcs_t = gcs.T
    n_sq = max(int(np.ceil(np.log2(r))) - 1, 0)

    heads = range(H_B)
    sls = [slice(h * HEAD_DIM, (h + 1) * HEAD_DIM) for h in heads]
    qs = [q_ref[:, sl] for sl in sls]
    ks = [k_ref[:, sl] for sl in sls]
    vs = [v_ref[:, sl] for sl in sls]
    gcl = [gcs[:, h:h + 1] for h in heads]
    gll = [gtot[:, h:h + 1] for h in heads]
    betas = [gb[:, H_B + h:H_B + h + 1] for h in heads]
    decays = [jnp.exp(jnp.where(causal, gcl[h] - gcs_t[h:h + 1, :], NEG)) for h in heads]
    kbs = [ks[h] * betas[h] for h in heads]
    mms = [jnp.where(strict, _dot_nt(kbs[h].astype(BF16), ks[h].astype(BF16)) * decays[h], 0.0) for h in heads]
    tinvs = [eye - mm for mm in mms]
    pws = list(mms)
    for _ in range(n_sq):
        pws = [_dot3(pw, pw) for pw in pws]
        tinvs = [tinv + _dot3(tinv, pw) for tinv, pw in zip(tinvs, pws)]
    egcs = [jnp.exp(gc) for gc in gcl]
    us = [_dot3(tinvs[h], vs[h] * betas[h]) for h in heads]
    ws = [_dot3(tinvs[h], kbs[h] * egcs[h]).astype(BF16) for h in heads]
    qks = [(_dot_nt(qs[h].astype(BF16), ks[h].astype(BF16)) * decays[h]).astype(BF16) for h in heads]
    q_decs = [(qs[h] * egcs[h]).astype(BF16) for h in heads]
    k_dec_ts = [(ks[h] * jnp.exp(gll[h] - gcl[h])).T for h in heads]

    if nseq_blk == 1:
        sbs = [s_scr[0, h].astype(BF16) for h in heads]
        v_news = [us[h] - _dot(ws[h], sbs[h]) for h in heads]
        vnb = [v.astype(BF16) for v in v_news]
        os_ = [_dot(q_decs[h], sbs[h]) + _dot(qks[h], vnb[h]) for h in heads]
        for h in heads:
            s_scr[0, h] = (s_scr[0, h] * jnp.exp(gll[h][c - 1:c, :])
                           + _dot(k_dec_ts[h].astype(BF16), vnb[h]))
        for h in heads:
            on = _rms(os_[h], gn_ref[...])
            o_ref[:, sls[h]] = (on * _silu(z_ref[:, sls[h]])).astype(o_ref.dtype)

    else:
        rowc = lax.broadcasted_iota(jnp.int32, (c, HEAD_DIM), 0) // r
        colc = lax.broadcasted_iota(jnp.int32, (HEAD_DIM, c), 1) // r
        wS_scr[...] = jnp.zeros(wS_scr.shape, F32)
        qS_scr[...] = jnp.zeros(qS_scr.shape, F32)

        def read_state(p, _):
            mine = rowc == p
            for h in heads:
                sb = s_scr[p, h].astype(BF16)
                wS_scr[h] += _dot(jnp.where(mine, ws[h], jnp.zeros_like(ws[h])), sb)
                qS_scr[h] += _dot(jnp.where(mine, q_decs[h], jnp.zeros_like(q_decs[h])), sb)
            return 0

        lax.fori_loop(0, nseq_blk, read_state, 0, unroll=2)
        for h in heads:
            vn_scr[h] = (us[h] - wS_scr[h]).astype(BF16)
            on = _rms(qS_scr[h] + _dot(qks[h], vn_scr[h]), gn_ref[...])
            o_ref[:, sls[h]] = (on * _silu(z_ref[:, sls[h]])).astype(o_ref.dtype)
        e_gls = [jnp.exp(gl) for gl in gll]

        def write_state(p, _):
            for h in heads:
                kd = jnp.where(colc == p, k_dec_ts[h], 0.0).astype(BF16)
                g_last = jnp.max(jnp.where(rowc[:, :1] == p, e_gls[h], 0.0), axis=0, keepdims=True)
                s_scr[p, h] = s_scr[p, h] * g_last + _dot(kd, vn_scr[h])
            return 0

        lax.fori_loop(0, nseq_blk, write_state, 0, unroll=2)

    @pl.when(ci == pl.num_programs(1) - 1)
    def _():
        sout_ref[...] = s_scr[...]


def _gdn(q, k, v, gb, proj, gdn_norm, s0, nseq, t):
    m = nseq * t
    if t >= CHUNK:
        nseq_blk, r = 1, CHUNK
    else:
        nseq_blk, r = CHUNK // t, t
    n_chunks = t // r
    n_blocks = nseq // nseq_blk
    row_map = lambda i, ci: (i * n_chunks + ci, 0)
    state_spec = pl.BlockSpec((nseq_blk, H_B, HEAD_DIM, HEAD_DIM), lambda i, ci: (i, 0, 0, 0))
    in_specs = [pl.BlockSpec((CHUNK, W_B), row_map), pl.BlockSpec((CHUNK, W_B), row_map),
                pl.BlockSpec((CHUNK, W_B), row_map), pl.BlockSpec((CHUNK, LANES), row_map),
                pl.BlockSpec((CHUNK, W_B), lambda i, ci: (i * n_chunks + ci, COL_ZB)),
                pl.BlockSpec((1, HEAD_DIM), lambda i, ci: (0, 0))]
    args = [q, k, v, gb, proj, gdn_norm.reshape(1, HEAD_DIM)]
    if s0 is not None:
        s0_all, layer = s0
        in_specs.append(pl.BlockSpec((None, nseq_blk, H_B, HEAD_DIM, HEAD_DIM), lambda i, ci: (layer, i, 0, 0, 0)))
        args.append(s0_all)
    return pl.pallas_call(
        functools.partial(_gdn_kernel, nseq_blk=nseq_blk, r=r, zero_init=s0 is None),
        grid=(n_blocks, n_chunks),
        in_specs=in_specs,
        out_specs=[pl.BlockSpec((CHUNK, W_B), row_map), state_spec],
        out_shape=[jax.ShapeDtypeStruct((m, W_B), BF16),
                   jax.ShapeDtypeStruct((nseq, H_B, HEAD_DIM, HEAD_DIM), F32)],
        scratch_shapes=[pltpu.VMEM((nseq_blk, H_B, HEAD_DIM, HEAD_DIM), F32),
                        pltpu.VMEM((H_B, CHUNK, HEAD_DIM), F32), pltpu.VMEM((H_B, CHUNK, HEAD_DIM), F32),
                        pltpu.VMEM((H_B, CHUNK, HEAD_DIM), BF16)],
        compiler_params=_params("parallel", "arbitrary"),
        name="gdn",
    )(*args)


def _hgrn_kernel(qc_ref, fc_ref, ic_ref, zc_ref, lb_ref, hn_ref, *rest, cs, per_chunk_state, zero_init):
    if zero_init:
        o_ref, sout_ref, st_scr = rest
    else:
        s0_ref, o_ref, sout_ref, st_scr = rest
    c = qc_ref.shape[0]
    n_ch = c // cs
    step = pl.program_id(1)

    if not per_chunk_state:
        @pl.when(step == 0)
        def _():
            st_scr[...] = jnp.zeros(st_scr.shape, F32)

    lb = lb_ref[...]
    fr = fc_ref[...]
    log_lb = jnp.log(jnp.maximum(lb, LB_FLOOR))
    a = log_lb
    b = jnp.log1p(-lb) - _softplus(-fr)
    mx = jnp.maximum(a, b)
    logf = mx + jnp.log1p(jnp.exp(-jnp.abs(a - b)))
    kk = (1.0 - lb) * _sigmoid(-fr)
    qq = _silu(qc_ref[...])

    row = lax.broadcasted_iota(jnp.int32, (c, c), 0)
    col = lax.broadcasted_iota(jnp.int32, (c, c), 1)
    same = (row // cs) == (col // cs)
    bc_all = _dot_exact_lhs((same & (col <= row)).astype(F32), logf)
    btot_all = _dot_exact_lhs(same.astype(F32), logf)
    t_idx = lax.broadcasted_iota(jnp.int32, (n_ch, cs, HEAD_DIM), 1)
    rowc = lax.broadcasted_iota(jnp.int32, (c, HEAD_DIM), 0) // cs

    def head(h):
        sl = slice(h * HEAD_DIM, (h + 1) * HEAD_DIM)
        return qq[:, sl], kk[:, sl], ic_ref[:, sl], bc_all[:, sl], btot_all[:, sl]

    def within_chunks(h):
        q, k, v, bc, _ = head(h)
        q3, k3, v3, bc3 = (x.reshape(n_ch, cs, HEAD_DIM) for x in (q, k, v, bc))
        o3 = jnp.zeros((n_ch, cs, HEAD_DIM), F32)
        for s in range(cs):
            dec = jnp.exp(jnp.where(t_idx >= s, bc3 - bc3[:, s:s + 1, :], NEG))
            a_col = jnp.sum(q3 * k3[:, s:s + 1, :] * dec, axis=-1, keepdims=True)
            o3 = o3 + a_col * v3[:, s:s + 1, :]
        return o3.reshape(c, HEAD_DIM)

    def across_chunks(h):
        q, k, v, bc, btot = head(h)
        q_dec = (q * jnp.exp(bc)).astype(BF16)
        k_dec = (k * jnp.exp(btot - bc)).astype(BF16)
        v_t = v.T.astype(BF16)
        e_tot = jnp.exp(btot)
        if per_chunk_state:
            def chunk_step(n, o_acc):
                mine = rowc == n
                st = s0_ref[n, h].T
                o_acc = o_acc + _dot_nt(jnp.where(mine, q_dec, jnp.zeros_like(q_dec)), st.astype(BF16))
                dec_n = jnp.max(jnp.where(mine, e_tot, 0.0), axis=0, keepdims=True)
                st = st * dec_n + _dot(v_t, jnp.where(mine, k_dec, jnp.zeros_like(k_dec)))
                sout_ref[n, h] = st.T
                return o_acc

            return lax.fori_loop(0, n_ch, chunk_step, jnp.zeros((c, HEAD_DIM), F32), unroll=4)
        st = st_scr[h]
        pieces = []
        for n in range(n_ch):
            rs = slice(n * cs, (n + 1) * cs)
            pieces.append(_dot_nt(q_dec[rs], st.astype(BF16)))
            kv = _dot(v_t, jnp.where(rowc == n, k_dec, jnp.zeros_like(k_dec)))
            st = st * e_tot[n * cs:n * cs + 1] + kv
        st_scr[h] = st
        return jnp.concatenate(pieces, axis=0)

    o_state = across_chunks(0)
    for h in range(H_C):
        sl = slice(h * HEAD_DIM, (h + 1) * HEAD_DIM)
        o = o_state
        if h + 1 < H_C:
            o_state = across_chunks(h + 1)
        on = _rms(o + within_chunks(h), hn_ref[...])
        o_ref[:, sl] = (on * _silu(zc_ref[:, sl])).astype(o_ref.dtype)

    if not per_chunk_state:
        @pl.when(step == pl.num_programs(1) - 1)
        def _():
            for h in range(H_C):
                sout_ref[0, h] = st_scr[h].T


def _hgrn(proj, lb, hg_norm, s0, nseq, t):
    m = nseq * t
    cs = min(HG_CHUNK, t)
    per_chunk_state = t <= HG_CHUNK
    if per_chunk_state:
        assert s0 is not None
        seq_blk = CHUNK // t
        grid = (nseq // seq_blk, 1)
        state_spec = pl.BlockSpec((seq_blk, H_C, HEAD_DIM, HEAD_DIM), lambda i, j: (i, 0, 0, 0))
        steps = 1
    else:
        assert s0 is None
        steps = t // CHUNK
        grid = (nseq, steps)
        state_spec = pl.BlockSpec((1, H_C, HEAD_DIM, HEAD_DIM), lambda i, j: (i, 0, 0, 0))
    col = lambda tile: pl.BlockSpec((CHUNK, W_C), lambda i, j: (i * steps + j, tile))
    in_specs = [col(COL_QC), col(COL_FC), col(COL_IC), col(COL_ZC),
                pl.BlockSpec((1, W_C), lambda i, j: (0, 0)),
                pl.BlockSpec((1, HEAD_DIM), lambda i, j: (0, 0))]
    args = [proj, proj, proj, proj, lb.reshape(1, W_C), hg_norm.reshape(1, HEAD_DIM)]
    if s0 is not None:
        s0_all, layer = s0
        in_specs.append(pl.BlockSpec((None,) + state_spec.block_shape, lambda i, j: (layer, i, 0, 0, 0)))
        args.append(s0_all)
    return pl.pallas_call(
        functools.partial(_hgrn_kernel, cs=cs, per_chunk_state=per_chunk_state, zero_init=s0 is None),
        grid=grid,
        in_specs=in_specs,
        out_specs=[pl.BlockSpec((CHUNK, W_C), lambda i, j: (i * steps + j, 0)), state_spec],
        out_shape=[jax.ShapeDtypeStruct((m, W_C), BF16),
                   jax.ShapeDtypeStruct((nseq, H_C, HEAD_DIM, HEAD_DIM), F32)],
        scratch_shapes=[pltpu.VMEM((H_C, HEAD_DIM, HEAD_DIM), F32)],
        compiler_params=_params("parallel", "arbitrary"),
        name="hgrn",
    )(*args)


def _out_proj_kernel(x_ref, oa_ref, ob_ref, oc_ref, w_ref, o_ref):
    wa = H_A * HEAD_DIM
    acc = _dot(oa_ref[...].astype(BF16), w_ref[:wa])
    acc += _dot(ob_ref[...], w_ref[wa:wa + W_B])
    acc += _dot(oc_ref[...], w_ref[wa + W_B:])
    o_ref[...] = x_ref[...] + acc


def _out_proj(x, oa, ob, oc, w, layer, tm, tn):
    m, d = x.shape
    kdim = w.shape[1]
    return pl.pallas_call(
        _out_proj_kernel,
        grid=(m // tm, d // tn),
        in_specs=[pl.BlockSpec((tm, tn), lambda i, j: (i, j)),
                  pl.BlockSpec((tm, oa.shape[1]), lambda i, j: (i, 0)),
                  pl.BlockSpec((tm, ob.shape[1]), lambda i, j: (i, 0)),
                  pl.BlockSpec((tm, oc.shape[1]), lambda i, j: (i, 0)),
                  pl.BlockSpec((None, kdim, tn), lambda i, j: (layer, 0, j))],
        out_specs=pl.BlockSpec((tm, tn), lambda i, j: (i, j)),
        out_shape=jax.ShapeDtypeStruct((m, d), F32),
        compiler_params=_params("parallel", "arbitrary"),
        name="out_proj",
    )(x, oa, ob, oc, w)


def _ffn_kernel(x_ref, prev_ref, prevu_ref, g_ref, wg_ref, wu_ref, cwg_ref, cwu_ref, wd_ref, gf_ref,
                o_ref, csg_ref, csu_ref, xn_ref, pn_ref, *, seq_start_every, prev_is_state, final_norm):
    j = pl.program_id(1)
    nb, r, d = x_ref.shape
    rows = nb * r

    @pl.when(j == 0)
    def _():
        xn_ref[...] = _rms(x_ref[...].reshape(rows, d), g_ref[...]).astype(BF16)
        o_ref[...] = jnp.zeros(o_ref.shape, F32)
        if not prev_is_state:
            pn = _rms(prev_ref[...].reshape(SUBLANES, d), g_ref[...])
            pn_ref[...] = jnp.concatenate([pn, jnp.zeros_like(pn)], axis=0).astype(BF16)

    xn = xn_ref[...]
    tn = wg_ref.shape[1]
    if not prev_is_state:
        first = (pl.program_id(0) % seq_start_every) == 0
    ws = tn // FFN_SUB
    cols = [slice(k * ws, (k + 1) * ws) for k in range(FFN_SUB)]

    def up_proj(k):
        cs = cols[k]
        ug = _dot(xn, wg_ref[:, cs]).reshape(nb, r, ws)
        uu = _dot(xn, wu_ref[:, cs]).reshape(nb, r, ws)
        if prev_is_state:
            pg = prev_ref[:, :, cs]
            pu = prevu_ref[:, :, cs]
        else:
            pg = jnp.where(first, 0.0, _dot(pn_ref[...], wg_ref[:, cs])[:SUBLANES]).reshape(1, SUBLANES, ws)
            pu = jnp.where(first, 0.0, _dot(pn_ref[...], wu_ref[:, cs])[:SUBLANES]).reshape(1, SUBLANES, ws)
        return ug, uu, pg, pu

    nxt = up_proj(0)
    down = None
    for k in range(FFN_SUB):
        cs = cols[k]
        ug, uu, pg, pu = nxt
        if k + 1 < FFN_SUB:
            nxt = up_proj(k + 1)
        csg_ref[:, :, cs] = ug[:, r - SUBLANES:, :]
        csu_ref[:, :, cs] = uu[:, r - SUBLANES:, :]
        gate = _dwconv(ug, pg, cwg_ref[:, cs]).reshape(rows, ws)
        up = _dwconv(uu, pu, cwu_ref[:, cs]).reshape(rows, ws)
        hidden = (_silu(gate) * up).astype(BF16)
        part = _dot(hidden, wd_ref[cs, :])
        down = part if down is None else down + part
    o_ref[...] += down.reshape(nb, r, d)

    @pl.when(j == pl.num_programs(1) - 1)
    def _():
        y = (x_ref[...] + o_ref[...]).reshape(rows, d)
        if final_norm:
            y = _rms(y, gf_ref[...])
        o_ref[...] = y.reshape(nb, r, d)


def _ffn(x3, conv_prev, norm_g, w_up, conv_w, w_down, layer, norm_final, nb, r, tn, final_norm):
    nseq, t, d = x3.shape
    nj = D_FF // tn
    steps_per_seq = t // r
    if conv_prev is None:
        assert nb == 1
        rb = r // SUBLANES
        prev_arr = x3
        prev_spec = pl.BlockSpec((1, SUBLANES, d), lambda i, j: (i // steps_per_seq,
                                                                 jnp.maximum((i % steps_per_seq) * rb - 1, 0), 0))
        prevu_spec = prev_spec
        x_map = lambda i, j: (i // steps_per_seq, i % steps_per_seq, 0)
        grid = (nseq * steps_per_seq, nj)
    else:
        assert r == t
        prev_arr = conv_prev
        prev_spec = pl.BlockSpec((nb, SUBLANES, tn), lambda i, j: (i, 0, j))
        prevu_spec = pl.BlockSpec((nb, SUBLANES, tn), lambda i, j: (i, 0, nj + j))
        x_map = lambda i, j: (i, 0, 0)
        grid = (nseq // nb, nj)
    rows = nb * r
    out, csg, csu = pl.pallas_call(
        functools.partial(_ffn_kernel, seq_start_every=steps_per_seq, prev_is_state=conv_prev is not None,
                          final_norm=final_norm),
        grid=grid,
        in_specs=[pl.BlockSpec((nb, r, d), x_map, pipeline_mode=pl.Buffered(1)), prev_spec, prevu_spec,
                  pl.BlockSpec((1, d), lambda i, j: (0, 0)),
                  pl.BlockSpec((None, d, tn), lambda i, j: (layer, 0, j)),
                  pl.BlockSpec((None, d, tn), lambda i, j: (layer, 0, nj + j)),
                  pl.BlockSpec((FFN_CONV, tn), lambda i, j: (0, j)),
                  pl.BlockSpec((FFN_CONV, tn), lambda i, j: (0, nj + j)),
                  pl.BlockSpec((None, tn, d), lambda i, j: (layer, j, 0)),
                  pl.BlockSpec((1, d), lambda i, j: (0, 0))],
        out_specs=[pl.BlockSpec((nb, r, d), x_map),
                   pl.BlockSpec((nb, SUBLANES, tn), lambda i, j: (i, 0, j)),
                   pl.BlockSpec((nb, SUBLANES, tn), lambda i, j: (i, 0, j))],
        out_shape=[jax.ShapeDtypeStruct((nseq, t, d), F32),
                   jax.ShapeDtypeStruct((grid[0] * nb, SUBLANES, D_FF), F32),
                   jax.ShapeDtypeStruct((grid[0] * nb, SUBLANES, D_FF), F32)],
        scratch_shapes=[pltpu.VMEM((rows, d), BF16), pltpu.VMEM((2 * SUBLANES, d), BF16)],
        compiler_params=_params("arbitrary", "arbitrary"),
        name="conv_ffn",
    )(x3, prev_arr, prev_arr, norm_g.reshape(1, d), w_up, w_up, conv_w, conv_w, w_down, norm_final.reshape(1, d))
    csg = csg.reshape(nseq, -1, SUBLANES, D_FF)[:, -1]
    csu = csu.reshape(nseq, -1, SUBLANES, D_FF)[:, -1]
    return out, csg, csu


def _pack_w_in(w):
    d = w.shape[0]
    cuts = np.cumsum([Q_RANK, KV_RANK, ROPE_DIM, 3 * W_B, H_B, H_B, W_B, W_C, W_C, W_C, W_C])
    c_q, c_kv, k_pe, qkv, a_b, b_b, z_b, q_c, f_c, i_c, z_c = jnp.split(w, [int(c) for c in cuts[:-1]], axis=1)
    z = lambda n: jnp.zeros((d, n), w.dtype)
    misc = jnp.concatenate([c_kv, k_pe, z(MISC_AB - MISC_KPE - ROPE_DIM), a_b, b_b,
                            z(PROJ_TILE - MISC_AB - 2 * H_B)], axis=1)
    return jnp.concatenate([qkv, c_q, misc, z_b, q_c, f_c, i_c, z_c], axis=1).astype(BF16)


def _pack_w_uq(w):
    w3 = w.reshape(Q_RANK, H_A, NOPE_DIM + ROPE_DIM)
    return jnp.concatenate([w3[:, :, :NOPE_DIM].reshape(Q_RANK, -1),
                            w3[:, :, NOPE_DIM:].reshape(Q_RANK, -1)], axis=1).astype(BF16)


def _rope_tables(pos):
    half = ROPE_DIM // 2
    inv = ROPE_THETA ** (-jnp.arange(half, dtype=F32) / half)
    ang = pos.astype(F32)[:, None] * inv[None, :]
    cos, sin = jnp.cos(ang), jnp.sin(ang)
    reps = LANES // ROPE_DIM
    return (jnp.tile(jnp.concatenate([cos, cos], axis=1), (1, reps)),
            jnp.tile(jnp.concatenate([-sin, sin], axis=1), (1, reps)))


def _pad_rows_front(a, total):
    n, k, c = a.shape
    return jnp.concatenate([jnp.zeros((n, total - k, c), a.dtype), a], axis=1)


def _row_tile(m, cap=512):
    for tm in (1024, 512, 256, 128, 64, 32, 16, 8):
        if tm <= cap and m % tm == 0:
            return tm
    raise ValueError(m)


def _layer(x3, cos, sin, lw, lb, layer, past, final_norm_g, is_last):
    nseq, t, d = x3.shape
    m = nseq * t
    tm = _row_tile(m)
    x = x3.reshape(m, d)
    proj = _norm_matmul(x, lw["norm_mix"], lw["w_in"], _row_tile(m, 1024), 2 * PROJ_TILE)
    proj3 = proj.reshape(nseq, t, PROJ_WIDTH)

    qlat, qpe, ckv, kpe = _mla_prep(proj, cos, sin, lw["q_norm"], lw["kv_norm"], lw["w_uq"], lw["w_uk_t"],
                                    min(tm, 256), BF16 if past is None else F32)
    if past is None:
        o_a = _prompt_attn(qlat, qpe, ckv, kpe, lw["w_uv_t"], nseq, t, min(256, t))
        gdn_prev = hg_s0 = gdn_s0 = ffn_prev = None
        nb, r = 1, min(tm, t)
        r_ffn = min(_row_tile(m, 1024), t)
    else:
        n_pages = past["page_table"].shape[1]
        pps = 64 if n_pages % 64 == 0 else (32 if n_pages % 32 == 0 else n_pages)
        o_a = _decode_attn(qlat, qpe, ckv, kpe, past["cache_ckv"], past["cache_kpe_t"], past["page_table"],
                           lw["w_uv_t"], layer, t, pps, 4)
        gdn_prev = _pad_rows_front(past["gdn_conv"], SUBLANES)
        gdn_s0, hg_s0 = (past["gdn_S"], layer), (past["hgrn_S"], layer)
        nb, r = min(nseq, 512 // t), t
        r_ffn = r
    q, k, v, gb, gdn_cs = _gdn_prep(proj3, gdn_prev, lw["gdn_conv_w"], lw["gdn_a_log"], lw["gdn_dt_bias"], nb, r)
    o_b, gdn_s = _gdn(q, k, v, gb, proj, lw["gdn_norm"], gdn_s0, nseq, t)
    o_c, hg_s = _hgrn(proj, lb, lw["hg_norm"], hg_s0, nseq, t)
    x = _out_proj(x, o_a, o_b, o_c, lw["w_out"], layer, tm, 1024)

    tn = 512
    ffn_prev = None if past is None else _pad_rows_front(past["ffn_conv"], SUBLANES)
    x3, csg, csu = _ffn(x.reshape(nseq, t, d), ffn_prev, lw["norm_ffn"], lw["w_up"], lw["ffn_conv_w"],
                        lw["w_down"], layer, final_norm_g, nb, r_ffn, tn, is_last)
    ffn_cs = jnp.concatenate([csg, csu], axis=2)[:, SUBLANES - (FFN_CONV - 1):]
    new_state = (ckv.reshape(nseq, t, KV_RANK), kpe.reshape(nseq, t, ROPE_DIM), gdn_s,
                 gdn_cs[:, SUBLANES - (CONV_B - 1):], hg_s, ffn_cs)
    return x3, new_state


def kernel(x_prompt, x_sample, cache_ckv, cache_kpe, page_table, state_gdn_S, state_gdn_conv, state_hgrn_S, state_ffn_conv, norm_mix, w_in, mla_q_norm, mla_kv_norm, mla_w_uq, mla_w_uk, mla_w_uv, gdn_conv_w, gdn_a_log, gdn_dt_bias, gdn_norm, hgrn_lb, hgrn_norm, w_out, norm_ffn, ffn_w_up, ffn_conv_w, ffn_w_down, norm_final):
    depth = w_in.shape[0]
    bp, tp, _ = x_prompt.shape
    bs, ts, _ = x_sample.shape
    past_len = page_table.shape[1] * PAGE_SIZE
    cos_p, sin_p = _rope_tables(jnp.tile(jnp.arange(tp, dtype=jnp.int32), bp))
    cos_s, sin_s = _rope_tables(jnp.tile(past_len + jnp.arange(ts, dtype=jnp.int32), bs))
    p_lb = jax.nn.softmax(hgrn_lb.astype(F32), axis=0)
    lb_all = jnp.cumsum(p_lb, axis=0) - p_lb[0:1]
    cache_kpe_t = jnp.swapaxes(cache_kpe, 2, 3)

    w_out_b, w_up_b, w_down_b = w_out.astype(BF16), ffn_w_up.astype(BF16), ffn_w_down.astype(BF16)

    xp, xs = x_prompt, x_sample
    new_p, new_s = [], []
    for l in range(depth):
        lw = dict(norm_mix=norm_mix[l], w_in=_pack_w_in(w_in[l]), q_norm=mla_q_norm[l], kv_norm=mla_kv_norm[l],
                  w_uq=_pack_w_uq(mla_w_uq[l]),
                  w_uk_t=jnp.transpose(mla_w_uk[l], (1, 2, 0)).astype(BF16),
                  w_uv_t=jnp.transpose(mla_w_uv[l], (1, 0, 2)).astype(BF16),
                  gdn_conv_w=gdn_conv_w[l], gdn_a_log=gdn_a_log[l], gdn_dt_bias=gdn_dt_bias[l],
                  gdn_norm=gdn_norm[l], hg_norm=hgrn_norm[l], w_out=w_out_b,
                  norm_ffn=norm_ffn[l], w_up=w_up_b, ffn_conv_w=ffn_conv_w[l], w_down=w_down_b)
        last = l == depth - 1
        xp, st_p = _layer(xp, cos_p, sin_p, lw, lb_all[l], l, None, norm_final, last)
        past = dict(cache_ckv=cache_ckv, cache_kpe_t=cache_kpe_t, page_table=page_table,
                    gdn_S=state_gdn_S, gdn_conv=state_gdn_conv[l], hgrn_S=state_hgrn_S,
                    ffn_conv=state_ffn_conv[l])
        xs, st_s = _layer(xs, cos_s, sin_s, lw, lb_all[l], l, past, norm_final, last)
        new_p.append(st_p)
        new_s.append(st_s)
    outs_p = tuple(jnp.stack(a) for a in zip(*new_p))
    outs_s = tuple(jnp.stack(a) for a in zip(*new_s))
    return (xp, xs) + outs_p + outs_s
```
